```python
import jax, jax.numpy as jnp
from jax import lax
import numpy as np

D_MODEL = 1024
BATCH = 8
SEQ = 2048
DEPTH = 2

HEAD_DIM = 64
N_Q_HEADS = D_MODEL // 128
N_KV_HEADS = 2
ATTN_Q_W = N_Q_HEADS * HEAD_DIM
ATTN_KV_W = N_KV_HEADS * HEAD_DIM
WINDOW = 128
ATTN_BLOCK = 128
ROPE_THETA = 10000.0
MLSTM_HEADS = 4
MLSTM_HEAD_DIM = D_MODEL // 16
MLSTM_W = MLSTM_HEADS * MLSTM_HEAD_DIM
MLSTM_CHUNK = 128
N_GATE_TYPES = 4
SGU_GROUPS = 4
SGU_GROUP_DIM = D_MODEL // 16
SGU_W = SGU_GROUPS * SGU_GROUP_DIM
SGU_CHUNK = 128
D_MIX = ATTN_Q_W + MLSTM_W + SGU_W
D_IN = ATTN_Q_W + 2 * ATTN_KV_W + 4 * MLSTM_W + N_GATE_TYPES * MLSTM_HEADS + 2 * SGU_W
D_FF = ((8 * D_MODEL // 3 + 127) // 128) * 128
EPS = 1e-6

kernel_name = "hybrid_mlstm_swa_sgu_macaron_encoder"


def rms_norm(x, g):
    xf = x.astype(jnp.float32)
    y = xf * lax.rsqrt(jnp.mean(xf * xf, axis=-1, keepdims=True) + EPS)
    return (y * g.astype(jnp.float32)).astype(x.dtype)


def swiglu(h, w_gate, w_up, w_down):
    return (jax.nn.silu(h @ w_gate) * (h @ w_up)) @ w_down


def rope(x, pos):
    d = x.shape[-1]
    freqs = ROPE_THETA ** (-jnp.arange(0, d, 2, dtype=jnp.float32) / d)
    ang = pos.astype(jnp.float32)[..., None] * freqs
    cos = jnp.cos(ang)[:, :, None, :]
    sin = jnp.sin(ang)[:, :, None, :]
    xf = x.astype(jnp.float32)
    x1, x2 = xf[..., : d // 2], xf[..., d // 2:]
    out = jnp.concatenate([x1 * cos - x2 * sin, x2 * cos + x1 * sin], axis=-1)
    return out.astype(x.dtype)


def window_attention(q, k, v, sink):
    B, S, HQ, d = q.shape
    HKV = k.shape[2]
    G = HQ // HKV
    NB = S // ATTN_BLOCK
    pad = ((0, 0), (ATTN_BLOCK, ATTN_BLOCK), (0, 0), (0, 0))
    kp = jnp.pad(k, pad).reshape(B, NB + 2, ATTN_BLOCK, HKV, d)
    vp = jnp.pad(v, pad).reshape(B, NB + 2, ATTN_BLOCK, HKV, d)
    kw = jnp.concatenate([kp[:, :-2], kp[:, 1:-1], kp[:, 2:]], axis=2)
    vw = jnp.concatenate([vp[:, :-2], vp[:, 1:-1], vp[:, 2:]], axis=2)
    qb = q.reshape(B, NB, ATTN_BLOCK, HKV, G, d)
    s = jnp.einsum('bnqkgd,bnskd->bnkgqs', qb, kw).astype(jnp.float32) * (d ** -0.5)
    blk = jnp.arange(NB)[:, None, None]
    qpos = blk * ATTN_BLOCK + jnp.arange(ATTN_BLOCK)[None, :, None]
    kpos = (blk - 1) * ATTN_BLOCK + jnp.arange(3 * ATTN_BLOCK)[None, None, :]
    valid = (jnp.abs(kpos - qpos) <= WINDOW) & (kpos >= 0) & (kpos < S)
    s = jnp.where(valid[None, :, None, None], s, -jnp.inf)
    sink_b = sink.astype(jnp.float32).reshape(HKV, G)[None, None, :, :, None]
    m = jnp.maximum(jnp.max(s, axis=-1), sink_b)
    p = jnp.exp(s - m[..., None])
    denom = jnp.sum(p, axis=-1) + jnp.exp(sink_b - m)
    o = jnp.einsum('bnkgqs,bnskd->bnqkgd', p, vw.astype(jnp.float32))
    o = o / jnp.moveaxis(denom, -1, 2)[..., None]
    return o.reshape(B, S, HQ * d).astype(q.dtype)


def mlstm_direction(q, k, v, log_i, log_f):
    B, H, S, d = q.shape
    L = MLSTM_CHUNK
    NC = S // L
    qc = q.reshape(B, H, NC, L, d)
    kc = k.reshape(B, H, NC, L, d)
    vc = v.reshape(B, H, NC, L, d)
    li = log_i.reshape(B, H, NC, L)
    lf = log_f.reshape(B, H, NC, L)
    b = jnp.cumsum(lf, axis=-1)
    b_last = b[..., -1]
    lower = jnp.tril(jnp.ones((L, L), dtype=bool))
    dmat = jnp.where(lower, b[..., :, None] - b[..., None, :] + li[..., None, :], -jnp.inf)
    a = b_last[..., None] - b + li
    m_loc = jnp.max(a, axis=-1)
    w = jnp.exp(a - m_loc[..., None])
    c_loc = jnp.einsum('bhcs,bhcsd,bhcse->bhcde', w, vc, kc)
    n_loc = jnp.einsum('bhcs,bhcse->bhce', w, kc)

    def step(carry, xs):
        c, n, m = carry
        cl, nl, ml, bl = xs
        m_new = jnp.maximum(bl + m, ml)
        sa = jnp.exp(bl + m - m_new)
        sb = jnp.exp(ml - m_new)
        c_new = sa[..., None, None] * c + sb[..., None, None] * cl
        n_new = sa[..., None] * n + sb[..., None] * nl
        return (c_new, n_new, m_new), (c, n, m)

    init = (jnp.zeros((B, H, d, d), jnp.float32), jnp.zeros((B, H, d), jnp.float32),
            jnp.zeros((B, H), jnp.float32))
    xs = (jnp.moveaxis(c_loc, 2, 0), jnp.moveaxis(n_loc, 2, 0), jnp.moveaxis(m_loc, 2, 0),
          jnp.moveaxis(b_last, 2, 0))
    _, (c_prev, n_prev, m_prev) = lax.scan(step, init, xs)
    c_prev = jnp.moveaxis(c_prev, 0, 2)
    n_prev = jnp.moveaxis(n_prev, 0, 2)
    m_prev = jnp.moveaxis(m_prev, 0, 2)
    inter = b + m_prev[..., None]
    m_t = jnp.maximum(inter, jnp.max(dmat, axis=-1))
    p = jnp.exp(dmat - m_t[..., None]) * jnp.einsum('bhcld,bhcsd->bhcls', qc, kc)
    sc = jnp.exp(inter - m_t)
    num = sc[..., None] * jnp.einsum('bhcde,bhcle->bhcld', c_prev, qc) + jnp.einsum('bhcls,bhcsd->bhcld', p, vc)
    den = sc * jnp.einsum('bhce,bhcle->bhcl', n_prev, qc) + jnp.sum(p, axis=-1)
    h = num / jnp.maximum(jnp.abs(den), jnp.exp(-m_t))[..., None]
    return h.reshape(B, H, S, d)


def mixer(h, pos, w_in, q_norm_g, k_norm_g, attn_sink, mlstm_gate_b, mlstm_head_g,
          sgu_norm_g, sgu_w_s, sgu_b, w_out):
    B, S, _ = h.shape
    sizes = [ATTN_Q_W, ATTN_KV_W, ATTN_KV_W, MLSTM_W, MLSTM_W, MLSTM_W, MLSTM_W,
             N_GATE_TYPES * MLSTM_HEADS, SGU_W, SGU_W]
    cuts = [int(c) for c in np.cumsum(sizes)[:-1]]
    z = h @ w_in
    aq, ak, av, mq, mk, mv, mo, mg, su, sv = jnp.split(z, cuts, axis=-1)

    aq = rope(rms_norm(aq.reshape(B, S, N_Q_HEADS, HEAD_DIM), q_norm_g), pos)
    ak = rope(rms_norm(ak.reshape(B, S, N_KV_HEADS, HEAD_DIM), k_norm_g), pos)
    av = av.reshape(B, S, N_KV_HEADS, HEAD_DIM)
    y_attn = window_attention(aq, ak, av, attn_sink)

    def heads(t):
        return jnp.transpose(t.reshape(B, S, MLSTM_HEADS, MLSTM_HEAD_DIM), (0, 2, 1, 3)).astype(jnp.float32)
    q_m, k_m, v_m = heads(mq), heads(mk) * (MLSTM_HEAD_DIM ** -0.5), heads(mv)
    g = mg.astype(jnp.float32).reshape(B, S, N_GATE_TYPES, MLSTM_HEADS) + mlstm_gate_b.astype(jnp.float32)
    g = jnp.transpose(g, (2, 0, 3, 1))
    li_f, lf_f = g[0], jax.nn.log_sigmoid(g[1])
    li_b, lf_b = g[2], jax.nn.log_sigmoid(g[3])
    h_fwd = mlstm_direction(q_m, k_m, v_m, li_f, lf_f)
    flip = lambda t: jnp.flip(t, axis=2)
    h_bwd = flip(mlstm_direction(flip(q_m), flip(k_m), flip(v_m), flip(li_b), flip(lf_b)))
    hm = jnp.transpose(h_fwd + h_bwd, (0, 2, 1, 3))
    hm = rms_norm(hm, mlstm_head_g.reshape(MLSTM_HEADS, MLSTM_HEAD_DIM)).reshape(B, S, MLSTM_W)
    y_mlstm = (jax.nn.sigmoid(mo.astype(jnp.float32)) * hm).astype(h.dtype)

    u = jax.nn.gelu(su, approximate=False)
    vg = jax.nn.gelu(sv, approximate=False).reshape(B, S, SGU_GROUPS, SGU_GROUP_DIM)
    vg = rms_norm(vg, sgu_norm_g.reshape(SGU_GROUPS, SGU_GROUP_DIM))
    vg = vg.reshape(B, S // SGU_CHUNK, SGU_CHUNK, SGU_GROUPS, SGU_GROUP_DIM)
    vg = jnp.einsum('gts,bcsgd->bctgd', sgu_w_s, vg) + jnp.transpose(sgu_b)[None, None, :, :, None]
    y_sgu = u * vg.reshape(B, S, SGU_W)

    y = jnp.concatenate([y_attn, y_mlstm, y_sgu.astype(h.dtype)], axis=-1)
    return y @ w_out


def setup_inputs(seed: int = 0) -> dict:
    key = jax.random.key(seed)
    ks = jax.random.split(key, 24)
    f32 = jnp.float32

    def nrm(k, shape, scale):
        return jax.random.normal(k, shape, f32) * scale

    def gain(k, shape):
        return 1.0 + 0.02 * jax.random.normal(k, shape, f32)

    x = jax.random.normal(ks[0], (BATCH, SEQ, D_MODEL), f32)
    offsets = jax.random.randint(ks[1], (BATCH, 1), 0, 4096, dtype=jnp.int32)
    positions = (jnp.arange(SEQ, dtype=jnp.int32)[None, :] + offsets).astype(jnp.int32)
    i_bias = 0.1 * jax.random.normal(ks[2], (DEPTH, 1, MLSTM_HEADS), f32)
    f_bias = jnp.linspace(3.0, 6.0, MLSTM_HEADS, dtype=f32)[None, None, :] + 0.1 * jax.random.normal(ks[3], (DEPTH, 1, MLSTM_HEADS), f32)
    i_bias_b = 0.1 * jax.random.normal(ks[4], (DEPTH, 1, MLSTM_HEADS), f32)
    f_bias_b = jnp.linspace(3.0, 6.0, MLSTM_HEADS, dtype=f32)[None, None, :] + 0.1 * jax.random.normal(ks[5], (DEPTH, 1, MLSTM_HEADS), f32)
    mlstm_gate_b = jnp.concatenate([i_bias, f_bias, i_bias_b, f_bias_b], axis=1)
    return {
        "x": x,
        "positions": positions,
        "norm_ffn1_g": gain(ks[6], (DEPTH, D_MODEL)),
        "ffn1_w_gate": nrm(ks[7], (DEPTH, D_MODEL, D_FF), D_MODEL ** -0.5),
        "ffn1_w_up": nrm(ks[8], (DEPTH, D_MODEL, D_FF), D_MODEL ** -0.5),
        "ffn1_w_down": nrm(ks[9], (DEPTH, D_FF, D_MODEL), D_FF ** -0.5),
        "norm_mix_g": gain(ks[10], (DEPTH, D_MODEL)),
        "w_in": nrm(ks[11], (DEPTH, D_MODEL, D_IN), D_MODEL ** -0.5),
        "q_norm_g": gain(ks[12], (DEPTH, HEAD_DIM)),
        "k_norm_g": gain(ks[13], (DEPTH, HEAD_DIM)),
        "attn_sink": nrm(ks[14], (DEPTH, N_Q_HEADS), 0.5),
        "mlstm_gate_b": mlstm_gate_b,
        "mlstm_head_g": gain(ks[15], (DEPTH, MLSTM_W)),
        "sgu_norm_g": gain(ks[16], (DEPTH, SGU_W)),
        "sgu_w_s": nrm(ks[17], (DEPTH, SGU_GROUPS, SGU_CHUNK, SGU_CHUNK), SGU_CHUNK ** -0.5),
        "sgu_b": 1.0 + 0.1 * jax.random.normal(ks[18], (DEPTH, SGU_GROUPS, SGU_CHUNK), f32),
        "w_out": nrm(ks[19], (DEPTH, D_MIX, D_MODEL), D_MIX ** -0.5),
        "norm_ffn2_g": gain(ks[20], (DEPTH, D_MODEL)),
        "ffn2_w_gate": nrm(ks[21], (DEPTH, D_MODEL, D_FF), D_MODEL ** -0.5),
        "ffn2_w_up": nrm(ks[22], (DEPTH, D_MODEL, D_FF), D_MODEL ** -0.5),
        "ffn2_w_down": nrm(ks[23], (DEPTH, D_FF, D_MODEL), D_FF ** -0.5),
        "norm_out_g": gain(jax.random.fold_in(key, 99), (DEPTH, D_MODEL)),
    }


def reference(x, positions, norm_ffn1_g, ffn1_w_gate, ffn1_w_up, ffn1_w_down, norm_mix_g, w_in,
              q_norm_g, k_norm_g, attn_sink, mlstm_gate_b, mlstm_head_g, sgu_norm_g, sgu_w_s, sgu_b,
              w_out, norm_ffn2_g, ffn2_w_gate, ffn2_w_up, ffn2_w_down, norm_out_g):
    for l in range(DEPTH):
        x = x + 0.5 * swiglu(rms_norm(x, norm_ffn1_g[l]), ffn1_w_gate[l], ffn1_w_up[l], ffn1_w_down[l])
        x = x + mixer(rms_norm(x, norm_mix_g[l]), positions, w_in[l], q_norm_g[l], k_norm_g[l],
                      attn_sink[l], mlstm_gate_b[l], mlstm_head_g[l], sgu_norm_g[l], sgu_w_s[l],
                      sgu_b[l], w_out[l])
        x = x + 0.5 * swiglu(rms_norm(x, norm_ffn2_g[l]), ffn2_w_gate[l], ffn2_w_up[l], ffn2_w_down[l])
        x = rms_norm(x, norm_out_g[l])
    return x
```

```python
import functools

import numpy as np
import jax
import jax.numpy as jnp
from jax import lax
from jax.experimental import pallas as pl
from jax.experimental.pallas import tpu as pltpu

F32 = jnp.float32
BF16 = jnp.bfloat16

D_MODEL = 1024
HEAD_DIM = 64
N_Q_HEADS = 8
N_KV_HEADS = 2
Q_PER_KV = N_Q_HEADS // N_KV_HEADS
ATTN_Q_W = N_Q_HEADS * HEAD_DIM
ATTN_KV_W = N_KV_HEADS * HEAD_DIM
WINDOW = 128
BLK = 128
ROPE_THETA = 10000.0
M_HEADS = 4
M_W = M_HEADS * HEAD_DIM
N_GATES = 4 * M_HEADS
SGU_GROUPS = 4
SGU_W = SGU_GROUPS * HEAD_DIM
D_FF = 2816
EPS = 1e-6
LANES = 128
GATE_PAD = LANES
D_IN_PAD = ATTN_Q_W + 2 * ATTN_KV_W + 4 * M_W + 2 * SGU_W + GATE_PAD
NEG = -1e30

VMEM_LIMIT = 56 * 1024 * 1024


def _dot(a, b):
    return jnp.dot(a, b, preferred_element_type=F32)


def _dot_nt(a, b):
    return lax.dot_general(a, b, (((1,), (1,)), ((), ())), preferred_element_type=F32)


def _dot_tn(a, b):
    return lax.dot_general(a, b, (((0,), (0,)), ((), ())), preferred_element_type=F32)


def _rms(x, g):
    ms = jnp.mean(x * x, axis=-1, keepdims=True)
    return x * lax.rsqrt(ms + EPS) * g


def _group_mean_sq(t, hmat):
    sq = t * t
    hi = sq.astype(BF16)
    lo = (sq - hi.astype(F32)).astype(BF16)
    return _dot(hi, hmat) + _dot(lo, hmat)


def _const_spec(shape):
    nd = len(shape)
    return pl.BlockSpec(shape, lambda *_: (0,) * nd, pipeline_mode=pl.Buffered(1))


def _params(n_grid):
    return pltpu.CompilerParams(dimension_semantics=("parallel",) * n_grid,
                                vmem_limit_bytes=VMEM_LIMIT)


def _rope_kernel(pos_ref, freq_ref, cos_ref, sin_ref):
    ang = pos_ref[...] * freq_ref[...]
    cos_ref[...] = jnp.cos(ang)
    sin_ref[...] = jnp.sin(ang)


def _rope_tables(positions):
    T = positions.size
    half = HEAD_DIM // 2
    per_row = LANES // half
    freqs = ROPE_THETA ** (-jnp.arange(0, HEAD_DIM, 2, dtype=F32) / HEAD_DIM)
    pos = jnp.repeat(positions.reshape(T // per_row, per_row).astype(F32), half, axis=1)
    freq_row = jnp.tile(freqs, per_row).reshape(1, LANES)
    rows = T // per_row
    cos, sin = pl.pallas_call(
        _rope_kernel,
        grid=(1,),
        in_specs=[pl.BlockSpec((rows, LANES), lambda i: (0, 0)),
                  pl.BlockSpec((1, LANES), lambda i: (0, 0))],
        out_specs=[pl.BlockSpec((rows, LANES), lambda i: (0, 0))] * 2,
        out_shape=[jax.ShapeDtypeStruct((rows, LANES), F32)] * 2,
        compiler_params=_params(1),
        name="rope_tables",
    )(pos, freq_row)
    cos = jnp.tile(cos.reshape(T, half), (1, per_row))
    sin = jnp.tile(sin.reshape(T, half), (1, per_row))
    return cos, sin


def _ffn_kernel(*refs, has_proj, has_final):
    refs = list(refs)
    x_ref = refs.pop(0)
    if has_proj:
        ya_ref, ym_ref, ys_ref, wo_ref = refs[:4]
        refs = refs[4:]
    g_ref, wg_ref, wu_ref, wd_ref = refs[:4]
    refs = refs[4:]
    if has_final:
        gout_ref = refs.pop(0)
    out_ref = refs.pop(0)

    x = x_ref[...]
    if has_proj:
        x = (x + _dot(ya_ref[...], wo_ref[0:ATTN_Q_W, :])
             + _dot(ym_ref[...], wo_ref[ATTN_Q_W:ATTN_Q_W + M_W, :])
             + _dot(ys_ref[...], wo_ref[ATTN_Q_W + M_W:, :]))
    h = _rms(x, g_ref[...]).astype(BF16)
    gate = _dot(h, wg_ref[...])
    up = _dot(h, wu_ref[...])
    act = (gate * jax.nn.sigmoid(gate) * up).astype(BF16)
    y = x + 0.5 * _dot(act, wd_ref[...])
    if has_final:
        y = _rms(y, gout_ref[...])
    out_ref[...] = y


def _ffn(x, g, wg, wu, wd, proj=None, gout=None, tm=256):
    T = x.shape[0]
    row = lambda w: pl.BlockSpec((tm, w), lambda i: (i, 0))
    args, specs = [x], [row(D_MODEL)]
    if proj is not None:
        ya, ym, ys, wo = proj
        args += [ya, ym, ys, wo]
        specs += [row(ATTN_Q_W), row(M_W), row(SGU_W), _const_spec(wo.shape)]
    args += [g.reshape(1, D_MODEL), wg, wu, wd]
    specs += [_const_spec((1, D_MODEL)), _const_spec(wg.shape), _const_spec(wu.shape), _const_spec(wd.shape)]
    if gout is not None:
        args.append(gout.reshape(1, D_MODEL))
        specs.append(_const_spec((1, D_MODEL)))
    return pl.pallas_call(
        functools.partial(_ffn_kernel, has_proj=proj is not None, has_final=gout is not None),
        grid=(T // tm,),
        in_specs=specs,
        out_specs=row(D_MODEL),
        out_shape=jax.ShapeDtypeStruct((T, D_MODEL), F32),
        compiler_params=_params(1),
        name="ffn",
    )(*args)


def _first_half(shape):
    lane = lax.broadcasted_iota(jnp.int32, shape, 1)
    return (lane & (HEAD_DIM - 1)) < HEAD_DIM // 2


def _rope(t, cos, sin_signed):
    n = t.shape[-1]
    reps = n // LANES
    swapped = jnp.where(_first_half(t.shape),
                        pltpu.roll(t, n - HEAD_DIM // 2, 1), pltpu.roll(t, HEAD_DIM // 2, 1))
    return t * jnp.tile(cos, (1, reps)) + swapped * jnp.tile(sin_signed, (1, reps))


def _mixin_kernel(x_ref, g_ref, w_ref, qg_ref, kg_ref, cos_ref, sin_ref, hmat_ref,
                  qa_ref, ka_ref, va_ref, mqkv_ref, mo_ref, mg_ref, su_ref, sv_ref):
    h = _rms(x_ref[...], g_ref[...]).astype(BF16)
    z = _dot(h, w_ref[...])
    cos = cos_ref[...]
    sin_signed = jnp.where(_first_half(cos.shape), -sin_ref[...], sin_ref[...])
    scale = HEAD_DIM ** -0.5

    o = 0
    aq = z[:, o:o + ATTN_Q_W]; o += ATTN_Q_W
    ak = z[:, o:o + ATTN_KV_W]; o += ATTN_KV_W
    av = z[:, o:o + ATTN_KV_W]; o += ATTN_KV_W
    mq = z[:, o:o + M_W]; o += M_W
    mk = z[:, o:o + M_W]; o += M_W
    mv = z[:, o:o + M_W]; o += M_W
    mo = z[:, o:o + M_W]; o += M_W
    su = z[:, o:o + SGU_W]; o += SGU_W
    sv = z[:, o:o + SGU_W]; o += SGU_W
    mg = z[:, o:o + GATE_PAD]

    qn = aq * lax.rsqrt(_group_mean_sq(aq, hmat_ref[...]) + EPS) * qg_ref[...]
    qa_ref[...] = (_rope(qn, cos, sin_signed) * scale).astype(BF16)
    kn = ak * lax.rsqrt(_group_mean_sq(ak, hmat_ref[0:ATTN_KV_W, 0:ATTN_KV_W]) + EPS) * kg_ref[...]
    ka_ref[...] = _rope(kn, cos, sin_signed).astype(BF16)
    va_ref[...] = av.astype(BF16)
    mqkv_ref[:, 0:M_W] = mq.astype(BF16)
    mqkv_ref[:, M_W:2 * M_W] = (mk * scale).astype(BF16)
    mqkv_ref[:, 2 * M_W:3 * M_W] = mv.astype(BF16)
    mo_ref[...] = mo.astype(BF16)
    mg_ref[...] = mg
    su_ref[...] = su.astype(BF16)
    sv_ref[...] = sv.astype(BF16)


def _mix_in(x, g, w_in_p, q_g, k_g, cos, sin, hmat, tm=256):
    T = x.shape[0]
    row = lambda w: pl.BlockSpec((tm, w), lambda i: (i, 0))
    widths = [ATTN_Q_W, ATTN_KV_W, ATTN_KV_W, 3 * M_W, M_W, GATE_PAD, SGU_W, SGU_W]
    dtypes = [BF16, BF16, BF16, BF16, BF16, F32, BF16, BF16]
    return pl.pallas_call(
        _mixin_kernel,
        grid=(T // tm,),
        in_specs=[row(D_MODEL), _const_spec((1, D_MODEL)), _const_spec(w_in_p.shape),
                  _const_spec((1, ATTN_Q_W)), _const_spec((1, ATTN_KV_W)),
                  row(LANES), row(LANES), _const_spec(hmat.shape)],
        out_specs=[row(w) for w in widths],
        out_shape=[jax.ShapeDtypeStruct((T, w), d) for w, d in zip(widths, dtypes)],
        compiler_params=_params(1),
        name="mix_in",
    )(x, g.reshape(1, D_MODEL), w_in_p,
      jnp.tile(q_g, N_Q_HEADS).reshape(1, ATTN_Q_W), jnp.tile(k_g, N_KV_HEADS).reshape(1, ATTN_KV_W),
      cos, sin, hmat)


def _attn_kernel(sink_ref, q_ref, k_ref, v_ref, o_ref):
    S = q_ref.shape[0]
    nb = S // BLK
    kw_len = 3 * BLK
    rows = Q_PER_KV * BLK
    row = lax.broadcasted_iota(jnp.int32, (rows, kw_len), 0)
    col = lax.broadcasted_iota(jnp.int32, (rows, kw_len), 1)
    rel = col - (row & (BLK - 1))
    grp = lax.broadcasted_iota(jnp.int32, (rows, 1), 0) >> 7

    def body(n, carry):
        start = pl.multiple_of(jnp.clip((n - 1) * BLK, 0, S - kw_len), BLK)
        q0 = pl.multiple_of(n * BLK, BLK)
        valid = jnp.abs(rel - (q0 - start)) <= WINDOW
        kw = k_ref[pl.ds(start, kw_len), :]
        vw = v_ref[pl.ds(start, kw_len), :]
        qblk = q_ref[pl.ds(q0, BLK), :]
        outs = []
        for kvh in range(N_KV_HEADS):
            kh = kw[:, kvh * HEAD_DIM:(kvh + 1) * HEAD_DIM]
            vh = vw[:, kvh * HEAD_DIM:(kvh + 1) * HEAD_DIM]
            heads = [kvh * Q_PER_KV + g for g in range(Q_PER_KV)]
            q4 = jnp.concatenate([qblk[:, hd * HEAD_DIM:(hd + 1) * HEAD_DIM] for hd in heads], axis=0)
            s = jnp.where(valid, _dot_nt(q4, kh), NEG)
            sink = jnp.zeros((rows, 1), F32)
            for g, hd in enumerate(heads):
                sink = jnp.where(grp == g, sink_ref[hd], sink)
            m = jnp.maximum(jnp.max(s, axis=-1, keepdims=True), sink)
            p = jnp.exp(s - m)
            denom = jnp.sum(p, axis=-1, keepdims=True) + jnp.exp(sink - m)
            o = _dot(p.astype(BF16), vh) / denom
            outs += [o[g * BLK:(g + 1) * BLK, :] for g in range(Q_PER_KV)]
        o_ref[pl.ds(q0, BLK), :] = jnp.concatenate(outs, axis=1).astype(BF16)
        return carry

    lax.fori_loop(0, nb, body, 0)


def _attention(qa, ka, va, sink, B, S):
    seq = lambda w: pl.BlockSpec((S, w), lambda b: (b, 0))
    return pl.pallas_call(
        _attn_kernel,
        grid=(B,),
        in_specs=[pl.BlockSpec(memory_space=pltpu.SMEM), seq(ATTN_Q_W), seq(ATTN_KV_W), seq(ATTN_KV_W)],
        out_specs=seq(ATTN_Q_W),
        out_shape=jax.ShapeDtypeStruct((B * S, ATTN_Q_W), BF16),
        compiler_params=_params(1),
        name="window_attn",
    )(sink, qa, ka, va)


def _log_sigmoid(x):
    return jnp.minimum(x, 0.0) - jnp.log(1.0 + jnp.exp(-jnp.abs(x)))


def _mlstm_kernel(qkv_ref, mo_ref, mg_ref, gb_ref, hg_ref, hmat_ref, o_ref, hf_ref, hb_ref):
    S = qkv_ref.shape[0]
    L = BLK
    nc = S // L
    sub = lax.broadcasted_iota(jnp.int32, (L, L), 0)
    lane = lax.broadcasted_iota(jnp.int32, (L, L), 1)
    tri = jnp.where(lane <= sub, 1.0, 0.0).astype(BF16)
    causal = (lane <= sub, lane >= sub)
    is_forget = ((lane >> 2) & 1) == 1
    ones_cols = jnp.ones((L, HEAD_DIM), F32)

    def chunk(c, d, states):
        r0 = pl.multiple_of(c * L, L)
        gates = mg_ref[pl.ds(r0, L), :] + gb_ref[...]
        gates = jnp.where(is_forget, _log_sigmoid(gates), gates)
        g1 = gates.astype(BF16)
        r1 = gates - g1.astype(F32)
        g2 = r1.astype(BF16)
        g3 = (r1 - g2.astype(F32)).astype(BF16)
        csum = _dot(tri, g1) + _dot(tri, g2) + _dot(tri, g3)
        total = csum[L - 1:L, :]
        if d == 1:
            csum = total - csum + gates
        gates_t = gates.T
        csum_t = csum.T
        qkv = qkv_ref[pl.ds(r0, L), :]
        outs, new_states = [], []
        for hd in range(M_HEADS):
            ci = 2 * d * M_HEADS + hd
            cf = ci + M_HEADS
            c_prev, m_prev = states[hd]
            li_col, b_col = gates[:, ci:ci + 1], csum[:, cf:cf + 1]
            rowv = gates_t[ci:ci + 1, :] - csum_t[cf:cf + 1, :]
            b_last = total[:, cf:cf + 1]
            q = qkv[:, hd * HEAD_DIM:(hd + 1) * HEAD_DIM]
            k = qkv[:, M_W + hd * HEAD_DIM:M_W + (hd + 1) * HEAD_DIM]
            v = qkv[:, 2 * M_W + hd * HEAD_DIM:2 * M_W + (hd + 1) * HEAD_DIM]
            v_ext = jnp.concatenate([v.astype(F32), ones_cols], axis=1)
            dmat = jnp.where(causal[d], b_col + rowv, NEG)
            a_col = b_last - b_col + li_col
            m_loc = jnp.max(a_col, axis=0, keepdims=True)
            w_col = jnp.exp(a_col - m_loc)
            c_loc = _dot_tn(k, (v_ext * w_col).astype(BF16))
            inter = b_col + m_prev
            m_t = jnp.maximum(inter, jnp.max(dmat, axis=1, keepdims=True))
            p = jnp.exp(dmat - m_t) * _dot_nt(q, k)
            sc = jnp.exp(inter - m_t)
            tot = sc * _dot(q, c_prev.astype(BF16)) + _dot(p.astype(BF16), v_ext.astype(BF16))
            num, den = tot[:, :HEAD_DIM], tot[:, HEAD_DIM:HEAD_DIM + 1]
            outs.append(num / jnp.maximum(jnp.abs(den), jnp.exp(-m_t)))
            m_new = jnp.maximum(b_last + m_prev, m_loc)
            c_new = jnp.exp(b_last + m_prev - m_new) * c_prev + jnp.exp(m_loc - m_new) * c_loc
            new_states.append((c_new, m_new))
        dst = hf_ref if d == 0 else hb_ref
        dst[pl.ds(r0, L), :] = jnp.concatenate(outs, axis=1)
        return tuple(new_states)

    def body(i, carry):
        fwd, bwd = carry
        return chunk(i, 0, fwd), chunk(nc - 1 - i, 1, bwd)

    init = tuple((jnp.zeros((HEAD_DIM, 2 * HEAD_DIM), F32), jnp.zeros((1, 1), F32)) for _ in range(M_HEADS))
    lax.fori_loop(0, nc, body, (init, init))

    hm = hf_ref[...] + hb_ref[...]
    hn = hm * lax.rsqrt(_group_mean_sq(hm, hmat_ref[...]) + EPS) * hg_ref[...]
    o_ref[...] = (jax.nn.sigmoid(mo_ref[...].astype(F32)) * hn).astype(BF16)


def _mlstm(mqkv, mo, mg, gate_b, head_g, hmat, B, S):
    seq = lambda w: pl.BlockSpec((S, w), lambda b: (b, 0))
    gb = jnp.pad(gate_b.reshape(1, N_GATES), ((0, 0), (0, GATE_PAD - N_GATES)))
    return pl.pallas_call(
        _mlstm_kernel,
        grid=(B,),
        in_specs=[seq(3 * M_W), seq(M_W), seq(GATE_PAD), _const_spec((1, GATE_PAD)),
                  _const_spec((1, M_W)), _const_spec((M_W, M_W))],
        out_specs=seq(M_W),
        out_shape=jax.ShapeDtypeStruct((B * S, M_W), BF16),
        scratch_shapes=[pltpu.VMEM((S, M_W), F32), pltpu.VMEM((S, M_W), F32)],
        compiler_params=_params(1),
        name="mlstm",
    )(mqkv, mo, mg, gb, head_g.reshape(1, M_W), hmat[:M_W, :M_W])


def _gelu(x):
    return 0.5 * x * (1.0 + lax.erf(x * np.float32(np.sqrt(0.5))))


def _sgu_kernel(su_ref, sv_ref, ng_ref, w_ref, b_ref, hmat_ref, o_ref):
    S = su_ref.shape[0]

    def body(c, carry):
        r0 = pl.multiple_of(c * BLK, BLK)
        u = _gelu(su_ref[pl.ds(r0, BLK), :].astype(F32))
        v = _gelu(sv_ref[pl.ds(r0, BLK), :].astype(F32))
        vn = (v * lax.rsqrt(_group_mean_sq(v, hmat_ref[...]) + EPS) * ng_ref[...]).astype(BF16)
        mixed = jnp.concatenate(
            [_dot(w_ref[g], vn[:, g * HEAD_DIM:(g + 1) * HEAD_DIM]) for g in range(SGU_GROUPS)], axis=1)
        o_ref[pl.ds(r0, BLK), :] = (u * (mixed + b_ref[...])).astype(BF16)
        return carry

    lax.fori_loop(0, S // BLK, body, 0)


def _sgu(su, sv, norm_g, w_s, b, hmat, B, S):
    seq = lambda w: pl.BlockSpec((S, w), lambda i: (i, 0))
    b_full = jnp.repeat(jnp.transpose(b), HEAD_DIM, axis=1)
    return pl.pallas_call(
        _sgu_kernel,
        grid=(B,),
        in_specs=[seq(SGU_W), seq(SGU_W), _const_spec((1, SGU_W)), _const_spec(w_s.shape),
                  _const_spec((BLK, SGU_W)), _const_spec((SGU_W, SGU_W))],
        out_specs=seq(SGU_W),
        out_shape=jax.ShapeDtypeStruct((B * S, SGU_W), BF16),
        compiler_params=_params(1),
        name="sgu",
    )(su, sv, norm_g.reshape(1, SGU_W), w_s.astype(BF16), b_full, hmat[:SGU_W, :SGU_W])


def _permute_w_in(w_in):
    g0 = ATTN_Q_W + 2 * ATTN_KV_W + 4 * M_W
    pad = jnp.zeros((w_in.shape[0], GATE_PAD - N_GATES), w_in.dtype)
    return jnp.concatenate([w_in[:, :g0], w_in[:, g0 + N_GATES:], w_in[:, g0:g0 + N_GATES], pad], axis=1)


def kernel(x, positions, norm_ffn1_g, ffn1_w_gate, ffn1_w_up, ffn1_w_down, norm_mix_g, w_in, q_norm_g, k_norm_g, attn_sink, mlstm_gate_b, mlstm_head_g, sgu_norm_g, sgu_w_s, sgu_b, w_out, norm_ffn2_g, ffn2_w_gate, ffn2_w_up, ffn2_w_down, norm_out_g):
    B, S, D = x.shape
    depth = w_in.shape[0]
    T = B * S
    xt = x.reshape(T, D)
    cos, sin = _rope_tables(positions)
    group = np.arange(ATTN_Q_W) // HEAD_DIM
    hmat = jnp.asarray((group[:, None] == group[None, :]) / HEAD_DIM, BF16)
    bf = lambda w: w.astype(BF16)
    for l in range(depth):
        xt = _ffn(xt, norm_ffn1_g[l], bf(ffn1_w_gate[l]), bf(ffn1_w_up[l]), bf(ffn1_w_down[l]))
        qa, ka, va, mqkv, mo, mg, su, sv = _mix_in(
            xt, norm_mix_g[l], bf(_permute_w_in(w_in[l])), q_norm_g[l], k_norm_g[l], cos, sin, hmat)
        ya = _attention(qa, ka, va, attn_sink[l], B, S)
        ym = _mlstm(mqkv, mo, mg, mlstm_gate_b[l], mlstm_head_g[l], hmat, B, S)
        ys = _sgu(su, sv, sgu_norm_g[l], sgu_w_s[l], sgu_b[l], hmat, B, S)
        xt = _ffn(xt, norm_ffn2_g[l], bf(ffn2_w_gate[l]), bf(ffn2_w_up[l]), bf(ffn2_w_down[l]),
                  proj=(ya, ym, ys, bf(w_out[l])), gout=norm_out_g[l])
    return xt.reshape(B, S, D)
```

```python
import functools

import numpy as np
import jax
import jax.numpy as jnp
from jax import lax
from jax.experimental import pallas as pl
from jax.experimental.pallas import tpu as pltpu

F32 = jnp.float32
BF16 = jnp.bfloat16

D_MODEL = 1024
HEAD_DIM = 64
N_Q_HEADS = 8
N_KV_HEADS = 2
Q_PER_KV = N_Q_HEADS // N_KV_HEADS
ATTN_Q_W = N_Q_HEADS * HEAD_DIM
ATTN_KV_W = N_KV_HEADS * HEAD_DIM
WINDOW = 128
BLK = 128
ROPE_THETA = 10000.0
M_HEADS = 4
M_W = M_HEADS * HEAD_DIM
N_GATES = 4 * M_HEADS
SGU_GROUPS = 4
SGU_W = SGU_GROUPS * HEAD_DIM
D_FF = 2816
EPS = 1e-6
LANES = 128
GATE_PAD = LANES
D_IN_PAD = ATTN_Q_W + 2 * ATTN_KV_W + 4 * M_W + 2 * SGU_W + GATE_PAD
NEG = -1e30

VMEM_LIMIT = 56 * 1024 * 1024
FFN_TM = 512
MIX_TM = 512


def _dot(a, b):
    return jnp.dot(a, b, preferred_element_type=F32)


def _dot_nt(a, b):
    return lax.dot_general(a, b, (((1,), (1,)), ((), ())), preferred_element_type=F32)


def _dot_tn(a, b):
    return lax.dot_general(a, b, (((0,), (0,)), ((), ())), preferred_element_type=F32)


def _rms(x, g):
    ms = jnp.mean(x * x, axis=-1, keepdims=True)
    return x * lax.rsqrt(ms + EPS) * g


def _group_mean_sq(t, hmat):
    return _dot((t * t).astype(BF16), hmat)


def _const_spec(shape):
    nd = len(shape)
    return pl.BlockSpec(shape, lambda *_: (0,) * nd, pipeline_mode=pl.Buffered(1))


def _layer_spec(stacked_shape, layer):
    nd = len(stacked_shape) - 1
    return pl.BlockSpec((None,) + tuple(stacked_shape[1:]), lambda *_: (layer,) + (0,) * nd,
                        pipeline_mode=pl.Buffered(1))


def _params(n_grid):
    return pltpu.CompilerParams(dimension_semantics=("parallel",) * n_grid,
                                vmem_limit_bytes=VMEM_LIMIT)


def _rope_kernel(pos_ref, freq_ref, cos_ref, sin_ref):
    ang = pos_ref[...] * freq_ref[...]
    cos_ref[...] = jnp.cos(ang)
    sin_ref[...] = jnp.sin(ang)


def _rope_tables(positions):
    T = positions.size
    half = HEAD_DIM // 2
    per_row = LANES // half
    freqs = ROPE_THETA ** (-jnp.arange(0, HEAD_DIM, 2, dtype=F32) / HEAD_DIM)
    pos = jnp.repeat(positions.reshape(T // per_row, per_row).astype(F32), half, axis=1)
    freq_row = jnp.tile(freqs, per_row).reshape(1, LANES)
    rows = T // per_row
    cos, sin = pl.pallas_call(
        _rope_kernel,
        grid=(1,),
        in_specs=[pl.BlockSpec((rows, LANES), lambda i: (0, 0)),
                  pl.BlockSpec((1, LANES), lambda i: (0, 0))],
        out_specs=[pl.BlockSpec((rows, LANES), lambda i: (0, 0))] * 2,
        out_shape=[jax.ShapeDtypeStruct((rows, LANES), F32)] * 2,
        compiler_params=_params(1),
        name="rope_tables",
    )(pos, freq_row)
    cos = jnp.tile(cos.reshape(T, half), (1, per_row))
    sin = jnp.tile(sin.reshape(T, half), (1, per_row))
    return cos, sin


def _ffn_kernel(*refs, has_proj, has_final):
    refs = list(refs)
    x_ref = refs.pop(0)
    if has_proj:
        ya_ref, ym_ref, ys_ref, wo_ref = refs[:4]
        refs = refs[4:]
    g_ref, wg_ref, wu_ref, wd_ref = refs[:4]
    refs = refs[4:]
    if has_final:
        gout_ref = refs.pop(0)
    out_ref = refs.pop(0)

    x = x_ref[...]
    if has_proj:
        x = (x + _dot(ya_ref[...], wo_ref[0:ATTN_Q_W, :])
             + _dot(ym_ref[...], wo_ref[ATTN_Q_W:ATTN_Q_W + M_W, :])
             + _dot(ys_ref[...], wo_ref[ATTN_Q_W + M_W:, :]))
    h = _rms(x, g_ref[...]).astype(BF16)
    gate = _dot(h, wg_ref[...])
    up = _dot(h, wu_ref[...])
    act = (gate * jax.nn.sigmoid(gate) * up).astype(BF16)
    y = x + 0.5 * _dot(act, wd_ref[...])
    if has_final:
        y = _rms(y, gout_ref[...])
    out_ref[...] = y


def _ffn(x, layer, g, wg, wu, wd, proj=None, gout=None, tm=FFN_TM):
    T = x.shape[0]
    row = lambda w: pl.BlockSpec((tm, w), lambda i: (i, 0))
    args, specs = [x], [row(D_MODEL)]
    if proj is not None:
        ya, ym, ys, wo = proj
        args += [ya, ym, ys, wo]
        specs += [row(ATTN_Q_W), row(M_W), row(SGU_W), _layer_spec(wo.shape, layer)]
    args += [g.reshape(1, D_MODEL), wg, wu, wd]
    specs += [_const_spec((1, D_MODEL)), _layer_spec(wg.shape, layer), _layer_spec(wu.shape, layer),
              _layer_spec(wd.shape, layer)]
    if gout is not None:
        args.append(gout.reshape(1, D_MODEL))
        specs.append(_const_spec((1, D_MODEL)))
    return pl.pallas_call(
        functools.partial(_ffn_kernel, has_proj=proj is not None, has_final=gout is not None),
        grid=(T // tm,),
        in_specs=specs,
        out_specs=row(D_MODEL),
        out_shape=jax.ShapeDtypeStruct((T, D_MODEL), F32),
        compiler_params=_params(1),
        name="ffn",
    )(*args)


def _first_half(shape):
    lane = lax.broadcasted_iota(jnp.int32, shape, 1)
    return (lane & (HEAD_DIM - 1)) < HEAD_DIM // 2


def _rope(t, cos, sin_signed):
    n = t.shape[-1]
    reps = n // LANES
    swapped = jnp.where(_first_half(t.shape),
                        pltpu.roll(t, n - HEAD_DIM // 2, 1), pltpu.roll(t, HEAD_DIM // 2, 1))
    return t * jnp.tile(cos, (1, reps)) + swapped * jnp.tile(sin_signed, (1, reps))


def _gelu(x):
    return 0.5 * x * (1.0 + lax.erf(x * np.float32(np.sqrt(0.5))))


def _mixin_kernel(x_ref, g_ref, w_ref, qg_ref, kg_ref, cos_ref, sin_ref, hmat_ref, ng_ref, ws_ref, bs_ref,
                  qa_ref, ka_ref, va_ref, mqkv_ref, mo_ref, mg_ref, ys_ref):
    h = _rms(x_ref[...], g_ref[...]).astype(BF16)
    z = _dot(h, w_ref[...])
    cos = cos_ref[...]
    sin_signed = jnp.where(_first_half(cos.shape), -sin_ref[...], sin_ref[...])
    scale = HEAD_DIM ** -0.5

    o = 0
    aq = z[:, o:o + ATTN_Q_W]; o += ATTN_Q_W
    ak = z[:, o:o + ATTN_KV_W]; o += ATTN_KV_W
    av = z[:, o:o + ATTN_KV_W]; o += ATTN_KV_W
    mq = z[:, o:o + M_W]; o += M_W
    mk = z[:, o:o + M_W]; o += M_W
    mv = z[:, o:o + M_W]; o += M_W
    mo = z[:, o:o + M_W]; o += M_W
    su = z[:, o:o + SGU_W]; o += SGU_W
    sv = z[:, o:o + SGU_W]; o += SGU_W
    mg = z[:, o:o + GATE_PAD]

    hmat = hmat_ref[...]
    gw = hmat.shape[0]
    q_ms = jnp.concatenate([_group_mean_sq(aq[:, i:i + gw], hmat) for i in range(0, ATTN_Q_W, gw)], axis=1)
    qn = aq * lax.rsqrt(q_ms + EPS) * qg_ref[...]
    qa_ref[...] = (_rope(qn, cos, sin_signed) * scale).astype(BF16)
    kn = ak * lax.rsqrt(_group_mean_sq(ak, hmat[0:ATTN_KV_W, 0:ATTN_KV_W]) + EPS) * kg_ref[...]
    ka_ref[...] = _rope(kn, cos, sin_signed).astype(BF16)
    va_ref[...] = av.astype(BF16)
    mqkv_ref[:, 0:M_W] = mq.astype(BF16)
    mqkv_ref[:, M_W:2 * M_W] = (mk * scale).astype(BF16)
    mqkv_ref[:, 2 * M_W:3 * M_W] = mv.astype(BF16)
    mo_ref[...] = mo.astype(BF16)
    mg_ref[...] = mg

    u = _gelu(su)
    v = _gelu(sv)
    vn = (v * lax.rsqrt(_group_mean_sq(v, hmat) + EPS) * ng_ref[...]).astype(BF16)
    lane_group = lax.broadcasted_iota(jnp.int32, (BLK, SGU_W), 1) >> 6
    for c in range(su.shape[0] // BLK):
        vc = vn[c * BLK:(c + 1) * BLK]
        stack = jnp.concatenate([jnp.where(lane_group == g, vc, jnp.zeros_like(vc)) for g in range(SGU_GROUPS)],
                                axis=0)
        mixed = _dot(ws_ref[...], stack)
        ys_ref[c * BLK:(c + 1) * BLK, :] = (u[c * BLK:(c + 1) * BLK] * (mixed + bs_ref[...])).astype(BF16)


def _mix_in(x, layer, g, w_in_p, q_g, k_g, cos, sin, hmat, sgu_g, sgu_w, sgu_b, tm=MIX_TM):
    T = x.shape[0]
    row = lambda w: pl.BlockSpec((tm, w), lambda i: (i, 0))
    widths = [ATTN_Q_W, ATTN_KV_W, ATTN_KV_W, 3 * M_W, M_W, GATE_PAD, SGU_W]
    dtypes = [BF16, BF16, BF16, BF16, BF16, F32, BF16]
    ws_cat = jnp.transpose(sgu_w, (1, 0, 2)).reshape(BLK, SGU_GROUPS * BLK).astype(BF16)
    bs_full = jnp.repeat(jnp.transpose(sgu_b), HEAD_DIM, axis=1)
    return pl.pallas_call(
        _mixin_kernel,
        grid=(T // tm,),
        in_specs=[row(D_MODEL), _const_spec((1, D_MODEL)), _layer_spec(w_in_p.shape, layer),
                  _const_spec((1, ATTN_Q_W)), _const_spec((1, ATTN_KV_W)),
                  row(LANES), row(LANES), _const_spec(hmat.shape),
                  _const_spec((1, SGU_W)), _const_spec(ws_cat.shape), _const_spec(bs_full.shape)],
        out_specs=[row(w) for w in widths],
        out_shape=[jax.ShapeDtypeStruct((T, w), d) for w, d in zip(widths, dtypes)],
        compiler_params=_params(1),
        name="mix_in",
    )(x, g.reshape(1, D_MODEL), w_in_p,
      jnp.tile(q_g, N_Q_HEADS).reshape(1, ATTN_Q_W), jnp.tile(k_g, N_KV_HEADS).reshape(1, ATTN_KV_W),
      cos, sin, hmat, sgu_g.reshape(1, SGU_W), ws_cat, bs_full)


def _attn_kernel(sink_ref, q_ref, k_ref, v_ref, o_ref):
    S = q_ref.shape[0]
    nb = S // BLK
    qi = lax.broadcasted_iota(jnp.int32, (BLK, BLK), 0)
    kc = lax.broadcasted_iota(jnp.int32, (BLK, BLK), 1)
    prev_bias = jnp.where(kc >= qi, 0.0, NEG)
    next_bias = jnp.where(kc <= qi, 0.0, NEG)

    def block(q0, start, biases):
        nk = len(biases)
        kw = k_ref[pl.ds(start, nk * BLK), :]
        vw = v_ref[pl.ds(start, nk * BLK), :]
        qblk = q_ref[pl.ds(q0, BLK), :]
        ones = jnp.ones((nk * BLK, HEAD_DIM), BF16)
        outs = []
        for kvh in range(N_KV_HEADS):
            kh = kw[:, kvh * HEAD_DIM:(kvh + 1) * HEAD_DIM]
            v_ext = jnp.concatenate([vw[:, kvh * HEAD_DIM:(kvh + 1) * HEAD_DIM], ones], axis=1)
            heads = [kvh * Q_PER_KV + g for g in range(Q_PER_KV)]
            q4 = jnp.concatenate([qblk[:, hd * HEAD_DIM:(hd + 1) * HEAD_DIM] for hd in heads], axis=0)
            s = _dot_nt(q4, kh)
            ps, ms = [], []
            for g, hd in enumerate(heads):
                parts = [s[g * BLK:(g + 1) * BLK, j * BLK:(j + 1) * BLK] for j in range(nk)]
                parts = [p if b is None else p + b for p, b in zip(parts, biases)]
                mx = functools.reduce(jnp.maximum, parts)
                m = jnp.maximum(jnp.max(mx, axis=-1, keepdims=True), sink_ref[hd])
                ps.append(jnp.concatenate([jnp.exp(p - m) for p in parts], axis=1).astype(BF16))
                ms.append(m)
            oe = _dot(jnp.concatenate(ps, axis=0), v_ext)
            for g, hd in enumerate(heads):
                og = oe[g * BLK:(g + 1) * BLK]
                denom = og[:, HEAD_DIM:HEAD_DIM + 1] + jnp.exp(sink_ref[hd] - ms[g])
                outs.append(og[:, :HEAD_DIM] / denom)
        o_ref[pl.ds(q0, BLK), :] = jnp.concatenate(outs, axis=1).astype(BF16)

    def middle(n, carry):
        block(pl.multiple_of(n * BLK, BLK), pl.multiple_of((n - 1) * BLK, BLK), [prev_bias, None, next_bias])
        return carry

    block(0, 0, [None, next_bias])
    lax.fori_loop(1, nb - 1, middle, 0)
    block((nb - 1) * BLK, (nb - 2) * BLK, [prev_bias, None])


def _attention(qa, ka, va, sink, B, S):
    seq = lambda w: pl.BlockSpec((S, w), lambda b: (b, 0))
    return pl.pallas_call(
        _attn_kernel,
        grid=(B,),
        in_specs=[pl.BlockSpec(memory_space=pltpu.SMEM), seq(ATTN_Q_W), seq(ATTN_KV_W), seq(ATTN_KV_W)],
        out_specs=seq(ATTN_Q_W),
        out_shape=jax.ShapeDtypeStruct((B * S, ATTN_Q_W), BF16),
        compiler_params=_params(1),
        name="window_attn",
    )(sink, qa, ka, va)


def _log_sigmoid(x):
    return jnp.minimum(x, 0.0) - jnp.log(1.0 + jnp.exp(-jnp.abs(x)))


def _mlstm_kernel(qkv_ref, mo_ref, mg_ref, gb_ref, hg_ref, o_ref,
                  gt_s, b_s, w_s, mloc_s, tot_s, rcol_s, vt_s, cloc_s, cprev_s, mprev_s):
    S = qkv_ref.shape[0]
    L = BLK
    nc = S // L
    ng = N_GATES
    sub = lax.broadcasted_iota(jnp.int32, (L, L), 0)
    lane = lax.broadcasted_iota(jnp.int32, (L, L), 1)
    ones_rows = jnp.ones((HEAD_DIM, L), F32)

    gates = mg_ref[...] + gb_ref[...]
    glane = lax.broadcasted_iota(jnp.int32, gates.shape, 1)
    gates = jnp.where(((glane >> 2) & 1) == 1, _log_sigmoid(gates), gates)
    for c in range(nc):
        gt_s[c * ng:(c + 1) * ng, :] = gates[c * L:(c + 1) * L, :].T[0:ng, :]
    gt = gt_s[...]
    upper = jnp.where(sub <= lane, 1.0, 0.0).astype(BF16)
    g1 = gt.astype(BF16)
    r1 = gt - g1.astype(F32)
    g2 = r1.astype(BF16)
    g3 = (r1 - g2.astype(F32)).astype(BF16)
    pre = _dot(g1, upper) + _dot(g2, upper) + _dot(g3, upper)
    tot = jnp.broadcast_to(pre[:, L - 1:L], pre.shape)
    row = lax.broadcasted_iota(jnp.int32, pre.shape, 0)
    bsum = jnp.where((row & (ng - 1)) < ng // 2, pre, tot - pre + gt)
    li = pltpu.roll(gt, M_HEADS, 0)
    a = tot - bsum + li
    mloc = jnp.broadcast_to(jnp.max(a, axis=1, keepdims=True), a.shape)
    b_s[...] = bsum
    w_s[...] = jnp.exp(a - mloc)
    mloc_s[...] = mloc
    tot_s[...] = tot
    rdiff = li - bsum
    zpad = jnp.zeros((L - ng, L), F32)
    for c in range(nc):
        rcol_s[c * L:(c + 1) * L, :] = jnp.concatenate([rdiff[c * ng:(c + 1) * ng, :], zpad], axis=0).T
        vt_s[:, c * L:(c + 1) * L] = qkv_ref[c * L:(c + 1) * L, 2 * M_W:3 * M_W].astype(F32).T

    def value_rows(hd, r0):
        return jnp.concatenate([vt_s[hd * HEAD_DIM:(hd + 1) * HEAD_DIM, pl.ds(r0, L)], ones_rows], axis=0)

    def local_state(c, carry):
        r0 = pl.multiple_of(c * L, L)
        g0 = pl.multiple_of(c * ng, ng)
        k = qkv_ref[pl.ds(r0, L), M_W:2 * M_W]
        w = w_s[pl.ds(g0, ng), :]
        for hd in range(M_HEADS):
            vte = value_rows(hd, r0)
            lhs = jnp.concatenate([vte * w[4 + hd:5 + hd, :], vte * w[12 + hd:13 + hd, :]], axis=0)
            cl = _dot(lhs.astype(BF16), k[:, (hd // 2) * LANES:(hd // 2 + 1) * LANES])
            cloc_s[0, hd, c] = cl[0:L]
            cloc_s[1, hd, c] = cl[L:2 * L]
        return carry

    lax.fori_loop(0, nc, local_state, 0)

    jrow = lax.broadcasted_iota(jnp.int32, (ng, L), 0)

    def scan(i, carry):
        states, m = carry
        cf = i
        cb = nc - 1 - i
        gf = pl.multiple_of(cf * ng, ng)
        gb = pl.multiple_of(cb * ng, ng)
        is_fwd = jrow < ng // 2
        blast = jnp.where(is_fwd, tot_s[pl.ds(gf, ng), :], tot_s[pl.ds(gb, ng), :])
        mloc_i = jnp.where(is_fwd, mloc_s[pl.ds(gf, ng), :], mloc_s[pl.ds(gb, ng), :])
        mprev_s[0, pl.ds(gf, ng), :] = m
        mprev_s[1, pl.ds(gb, ng), :] = m
        m_new = jnp.maximum(blast + m, mloc_i)
        sa = jnp.exp(blast + m - m_new)
        sb = jnp.exp(mloc_i - m_new)
        new_states = []
        for d in range(2):
            ci = cf if d == 0 else cb
            for hd in range(M_HEADS):
                r = 4 + hd + 8 * d
                st = states[d * M_HEADS + hd]
                cprev_s[d, hd, ci] = st.astype(BF16)
                new_states.append(sa[r:r + 1, :] * st + sb[r:r + 1, :] * cloc_s[d, hd, ci])
        return tuple(new_states), m_new

    init = (tuple(jnp.zeros((L, LANES), F32) for _ in range(2 * M_HEADS)), jnp.zeros((ng, L), F32))
    lax.fori_loop(0, nc, scan, init)

    causal = (sub <= lane, sub >= lane)

    def outputs(c, carry):
        r0 = pl.multiple_of(c * L, L)
        g0 = pl.multiple_of(c * ng, ng)
        q = qkv_ref[pl.ds(r0, L), 0:M_W]
        k = qkv_ref[pl.ds(r0, L), M_W:2 * M_W]
        bc = b_s[pl.ds(g0, ng), :]
        mprev = (mprev_s[0, pl.ds(g0, ng), :], mprev_s[1, pl.ds(g0, ng), :])
        rc = rcol_s[pl.ds(r0, L), :]
        houts = []
        for hd in range(M_HEADS):
            t = hd // 2
            qp = q[:, t * LANES:(t + 1) * LANES]
            qm = jnp.where((lane >> 6) == (hd & 1), qp, jnp.zeros_like(qp))
            x = jnp.concatenate([k[:, t * LANES:(t + 1) * LANES], cprev_s[0, hd, c], cprev_s[1, hd, c]], axis=0)
            y = _dot_nt(x, qm)
            st = y[0:L]
            pts, scs, mts = [], [], []
            for d in range(2):
                r = 4 + hd + 8 * d
                b_row = bc[r:r + 1, :]
                dm = jnp.where(causal[d], rc[:, r:r + 1] + b_row, NEG)
                inter = b_row + mprev[d][r:r + 1, :]
                m_t = jnp.maximum(inter, jnp.max(dm, axis=0, keepdims=True))
                pts.append((jnp.exp(dm - m_t) * st).astype(BF16))
                scs.append(jnp.exp(inter - m_t))
                mts.append(m_t)
            n2 = _dot(value_rows(hd, r0).astype(BF16), jnp.concatenate(pts, axis=1))
            ht = None
            for d in range(2):
                tt = n2[:, d * L:(d + 1) * L] + scs[d] * y[(d + 1) * L:(d + 2) * L]
                den = tt[HEAD_DIM:HEAD_DIM + 1, :]
                hd_out = tt[0:HEAD_DIM] / jnp.maximum(jnp.abs(den), jnp.exp(-mts[d]))
                ht = hd_out if ht is None else ht + hd_out
            ms = jnp.mean(ht * ht, axis=0, keepdims=True)
            houts.append(ht * lax.rsqrt(ms + EPS))
        hn = (jnp.concatenate(houts, axis=0) * hg_ref[...]).T
        o_ref[pl.ds(r0, L), :] = (jax.nn.sigmoid(mo_ref[pl.ds(r0, L), :].astype(F32)) * hn).astype(BF16)
        return carry

    lax.fori_loop(0, nc, outputs, 0, unroll=2)


def _mlstm(mqkv, mo, mg, gate_b, head_g, B, S):
    seq = lambda w: pl.BlockSpec((S, w), lambda b: (b, 0))
    nc = S // BLK
    gb = jnp.pad(gate_b.reshape(1, N_GATES), ((0, 0), (0, GATE_PAD - N_GATES)))
    hg_t = jnp.broadcast_to(head_g.reshape(M_W, 1), (M_W, BLK))
    rows = nc * N_GATES
    return pl.pallas_call(
        _mlstm_kernel,
        grid=(B,),
        in_specs=[seq(3 * M_W), seq(M_W), seq(GATE_PAD), _const_spec((1, GATE_PAD)), _const_spec((M_W, BLK))],
        out_specs=seq(M_W),
        out_shape=jax.ShapeDtypeStruct((B * S, M_W), BF16),
        scratch_shapes=[pltpu.VMEM((rows, BLK), F32),
                        pltpu.VMEM((rows, BLK), F32),
                        pltpu.VMEM((rows, BLK), F32),
                        pltpu.VMEM((rows, BLK), F32),
                        pltpu.VMEM((rows, BLK), F32),
                        pltpu.VMEM((S, LANES), F32),
                        pltpu.VMEM((M_W, S), F32),
                        pltpu.VMEM((2, M_HEADS, nc, BLK, LANES), F32),
                        pltpu.VMEM((2, M_HEADS, nc, BLK, LANES), BF16),
                        pltpu.VMEM((2, rows, BLK), F32)],
        compiler_params=_params(1),
        name="mlstm",
    )(mqkv, mo, mg, gb, hg_t)


def _permute_w_in(w_in):
    g0 = ATTN_Q_W + 2 * ATTN_KV_W + 4 * M_W
    pad = jnp.zeros(w_in.shape[:-1] + (GATE_PAD - N_GATES,), w_in.dtype)
    return jnp.concatenate([w_in[..., :g0], w_in[..., g0 + N_GATES:], w_in[..., g0:g0 + N_GATES], pad], axis=-1)


def kernel(x, positions, norm_ffn1_g, ffn1_w_gate, ffn1_w_up, ffn1_w_down, norm_mix_g, w_in, q_norm_g, k_norm_g, attn_sink, mlstm_gate_b, mlstm_head_g, sgu_norm_g, sgu_w_s, sgu_b, w_out, norm_ffn2_g, ffn2_w_gate, ffn2_w_up, ffn2_w_down, norm_out_g):
    B, S, D = x.shape
    depth = w_in.shape[0]
    T = B * S
    xt = x.reshape(T, D)
    cos, sin = _rope_tables(positions)
    group = np.arange(M_W) // HEAD_DIM
    hmat = jnp.asarray((group[:, None] == group[None, :]) / HEAD_DIM, BF16)
    w1g, w1u, w1d = ffn1_w_gate.astype(BF16), ffn1_w_up.astype(BF16), ffn1_w_down.astype(BF16)
    w2g, w2u, w2d = ffn2_w_gate.astype(BF16), ffn2_w_up.astype(BF16), ffn2_w_down.astype(BF16)
    w_in_p = _permute_w_in(w_in.astype(BF16))
    w_o = w_out.astype(BF16)
    for l in range(depth):
        xt = _ffn(xt, l, norm_ffn1_g[l], w1g, w1u, w1d)
        qa, ka, va, mqkv, mo, mg, ys = _mix_in(xt, l, norm_mix_g[l], w_in_p, q_norm_g[l], k_norm_g[l], cos, sin, hmat,
                                               sgu_norm_g[l], sgu_w_s[l], sgu_b[l])
        ya = _attention(qa, ka, va, attn_sink[l], B, S)
        ym = _mlstm(mqkv, mo, mg, mlstm_gate_b[l], mlstm_head_g[l], B, S)
        xt = _ffn(xt, l, norm_ffn2_g[l], w2g, w2u, w2d, proj=(ya, ym, ys, w_o), gout=norm_out_g[l])
    return xt.reshape(B, S, D)
```

```python
import functools

import numpy as np
import jax
import jax.numpy as jnp
from jax import lax
from jax.experimental import pallas as pl
from jax.experimental.pallas import tpu as pltpu

F32 = jnp.float32
BF16 = jnp.bfloat16

D_MODEL = 1024
HEAD_DIM = 64
N_Q_HEADS = 8
N_KV_HEADS = 2
Q_PER_KV = N_Q_HEADS // N_KV_HEADS
ATTN_Q_W = N_Q_HEADS * HEAD_DIM
ATTN_KV_W = N_KV_HEADS * HEAD_DIM
WINDOW = 128
BLK = 128
ROPE_THETA = 10000.0
M_HEADS = 4
M_W = M_HEADS * HEAD_DIM
N_GATES = 4 * M_HEADS
SGU_GROUPS = 4
SGU_W = SGU_GROUPS * HEAD_DIM
D_FF = 2816
EPS = 1e-6
LANES = 128
GATE_PAD = LANES
D_IN_PAD = ATTN_Q_W + 2 * ATTN_KV_W + 4 * M_W + 2 * SGU_W + GATE_PAD
NEG = -1e30

VMEM_LIMIT = 56 * 1024 * 1024
FFN_TM = 512
MIX_TM = 512


def _dot(a, b):
    return jnp.dot(a, b, preferred_element_type=F32)


def _dot_nt(a, b):
    return lax.dot_general(a, b, (((1,), (1,)), ((), ())), preferred_element_type=F32)


def _dot_tn(a, b):
    return lax.dot_general(a, b, (((0,), (0,)), ((), ())), preferred_element_type=F32)


def _rms(x, g):
    ms = jnp.mean(x * x, axis=-1, keepdims=True)
    return x * lax.rsqrt(ms + EPS) * g


def _group_mean_sq(t, hmat):
    return _dot((t * t).astype(BF16), hmat)


def _const_spec(shape):
    nd = len(shape)
    return pl.BlockSpec(shape, lambda *_: (0,) * nd, pipeline_mode=pl.Buffered(1))


def _layer_spec(stacked_shape, layer):
    nd = len(stacked_shape) - 1
    return pl.BlockSpec((None,) + tuple(stacked_shape[1:]), lambda *_: (layer,) + (0,) * nd,
                        pipeline_mode=pl.Buffered(1))


def _params(n_grid):
    return pltpu.CompilerParams(dimension_semantics=("parallel",) * n_grid,
                                vmem_limit_bytes=VMEM_LIMIT)


def _rope_kernel(pos_ref, freq_ref, cos_ref, sin_ref):
    ang = pos_ref[...] * freq_ref[...]
    cos_ref[...] = jnp.cos(ang)
    sin_ref[...] = jnp.sin(ang)


def _rope_tables(positions):
    T = positions.size
    half = HEAD_DIM // 2
    per_row = LANES // half
    freqs = ROPE_THETA ** (-jnp.arange(0, HEAD_DIM, 2, dtype=F32) / HEAD_DIM)
    pos = jnp.repeat(positions.reshape(T // per_row, per_row).astype(F32), half, axis=1)
    freq_row = jnp.tile(freqs, per_row).reshape(1, LANES)
    rows = T // per_row
    cos, sin = pl.pallas_call(
        _rope_kernel,
        grid=(1,),
        in_specs=[pl.BlockSpec((rows, LANES), lambda i: (0, 0)),
                  pl.BlockSpec((1, LANES), lambda i: (0, 0))],
        out_specs=[pl.BlockSpec((rows, LANES), lambda i: (0, 0))] * 2,
        out_shape=[jax.ShapeDtypeStruct((rows, LANES), F32)] * 2,
        compiler_params=_params(1),
        name="rope_tables",
    )(pos, freq_row)
    cos = jnp.tile(cos.reshape(T, half), (1, per_row))
    sin = jnp.tile(sin.reshape(T, half), (1, per_row))
    return cos, sin


def _ffn_kernel(*refs, has_proj, has_final):
    refs = list(refs)
    x_ref = refs.pop(0)
    if has_proj:
        ya_ref, ym_ref, ys_ref, wo_ref = refs[:4]
        refs = refs[4:]
    g_ref, wg_ref, wu_ref, wd_ref = refs[:4]
    refs = refs[4:]
    if has_final:
        gout_ref = refs.pop(0)
    out_ref = refs.pop(0)

    x = x_ref[...]
    if has_proj:
        x = (x + _dot(ya_ref[...], wo_ref[0:ATTN_Q_W, :])
             + _dot(ym_ref[...], wo_ref[ATTN_Q_W:ATTN_Q_W + M_W, :])
             + _dot(ys_ref[...], wo_ref[ATTN_Q_W + M_W:, :]))
    h = _rms(x, g_ref[...]).astype(BF16)
    gate = _dot(h, wg_ref[...])
    up = _dot(h, wu_ref[...])
    act = (gate * jax.nn.sigmoid(gate) * up).astype(BF16)
    y = x + 0.5 * _dot(act, wd_ref[...])
    if has_final:
        y = _rms(y, gout_ref[...])
    out_ref[...] = y


def _ffn(x, layer, g, wg, wu, wd, proj=None, gout=None, tm=FFN_TM):
    T = x.shape[0]
    row = lambda w: pl.BlockSpec((tm, w), lambda i: (i, 0))
    args, specs = [x], [row(D_MODEL)]
    if proj is not None:
        ya, ym, ys, wo = proj
        args += [ya, ym, ys, wo]
        specs += [row(ATTN_Q_W), row(M_W), row(SGU_W), _layer_spec(wo.shape, layer)]
    args += [g.reshape(1, D_MODEL), wg, wu, wd]
    specs += [_const_spec((1, D_MODEL)), _layer_spec(wg.shape, layer), _layer_spec(wu.shape, layer),
              _layer_spec(wd.shape, layer)]
    if gout is not None:
        args.append(gout.reshape(1, D_MODEL))
        specs.append(_const_spec((1, D_MODEL)))
    return pl.pallas_call(
        functools.partial(_ffn_kernel, has_proj=proj is not None, has_final=gout is not None),
        grid=(T // tm,),
        in_specs=specs,
        out_specs=row(D_MODEL),
        out_shape=jax.ShapeDtypeStruct((T, D_MODEL), F32),
        compiler_params=_params(1),
        name="ffn",
    )(*args)


def _first_half(shape):
    lane = lax.broadcasted_iota(jnp.int32, shape, 1)
    return (lane & (HEAD_DIM - 1)) < HEAD_DIM // 2


def _rope(t, cos, sin_signed):
    n = t.shape[-1]
    reps = n // LANES
    swapped = jnp.where(_first_half(t.shape),
                        pltpu.roll(t, n - HEAD_DIM // 2, 1), pltpu.roll(t, HEAD_DIM // 2, 1))
    return t * jnp.tile(cos, (1, reps)) + swapped * jnp.tile(sin_signed, (1, reps))


def _gelu(x):
    return 0.5 * x * (1.0 + lax.erf(x * np.float32(np.sqrt(0.5))))


def _mixin_kernel(x_ref, g_ref, w_ref, qg_ref, kg_ref, cos_ref, sin_ref, hmat_ref, ng_ref, ws_ref, bs_ref,
                  qa_ref, kd_ref, vat_ref, mqk_ref, mvt_ref, mo_ref, mgt_ref, ys_ref):
    h = _rms(x_ref[...], g_ref[...]).astype(BF16)
    cos = cos_ref[...]
    sin_signed = jnp.where(_first_half(cos.shape), -sin_ref[...], sin_ref[...])
    scale = HEAD_DIM ** -0.5
    hmat = hmat_ref[...]
    gw = hmat.shape[0]

    col = [0]

    def proj(width):
        z = _dot(h, w_ref[:, col[0]:col[0] + width])
        col[0] += width
        return z

    aq = proj(ATTN_Q_W)
    q_ms = jnp.concatenate([_group_mean_sq(aq[:, i:i + gw], hmat) for i in range(0, ATTN_Q_W, gw)], axis=1)
    qn = aq * lax.rsqrt(q_ms + EPS) * qg_ref[...]
    qa_ref[...] = (_rope(qn, cos, sin_signed) * scale).astype(BF16)

    akv = proj(2 * ATTN_KV_W)
    ak, av = akv[:, :ATTN_KV_W], akv[:, ATTN_KV_W:]
    kn = ak * lax.rsqrt(_group_mean_sq(ak, hmat[0:ATTN_KV_W, 0:ATTN_KV_W]) + EPS) * kg_ref[...]
    kr = _rope(kn, cos, sin_signed)
    ks = pltpu.roll(kr, HEAD_DIM, 1)
    low = lax.broadcasted_iota(jnp.int32, kr.shape, 1) < HEAD_DIM
    kd_ref[:, 0:LANES] = jnp.where(low, kr, ks).astype(BF16)
    kd_ref[:, LANES:2 * LANES] = jnp.where(low, ks, kr).astype(BF16)

    vat_ref[...] = av.T.astype(BF16)
    mqk_ref[:, 0:M_W] = proj(M_W).astype(BF16)
    mqk_ref[:, M_W:2 * M_W] = (proj(M_W) * scale).astype(BF16)
    mvt_ref[...] = proj(M_W).T.astype(BF16)
    mo_ref[...] = proj(M_W).astype(BF16)
    su = proj(SGU_W)
    sv = proj(SGU_W)
    mgt_ref[...] = proj(GATE_PAD).T[0:N_GATES, :]

    u = _gelu(su)
    v = _gelu(sv)
    vn = (v * lax.rsqrt(_group_mean_sq(v, hmat) + EPS) * ng_ref[...]).astype(BF16)
    lane_group = lax.broadcasted_iota(jnp.int32, (BLK, SGU_W), 1) >> 6
    for c in range(su.shape[0] // BLK):
        vc = vn[c * BLK:(c + 1) * BLK]
        stack = jnp.concatenate([jnp.where(lane_group == g, vc, jnp.zeros_like(vc)) for g in range(SGU_GROUPS)],
                                axis=0)
        mixed = _dot(ws_ref[...], stack)
        ys_ref[c * BLK:(c + 1) * BLK, :] = (u[c * BLK:(c + 1) * BLK] * (mixed + bs_ref[...])).astype(BF16)


def _mix_in(x, layer, g, w_in_p, q_g, k_g, cos, sin, hmat, sgu_g, sgu_w, sgu_b, tm=MIX_TM):
    T = x.shape[0]
    row = lambda w: pl.BlockSpec((tm, w), lambda i: (i, 0))
    tcol = lambda r: pl.BlockSpec((r, tm), lambda i: (0, i))
    out_specs = [row(ATTN_Q_W), row(2 * ATTN_KV_W), tcol(ATTN_KV_W), row(2 * M_W), tcol(M_W), row(M_W),
                 tcol(N_GATES), row(SGU_W)]
    out_shape = [jax.ShapeDtypeStruct(s, d) for s, d in [
        ((T, ATTN_Q_W), BF16), ((T, 2 * ATTN_KV_W), BF16), ((ATTN_KV_W, T), BF16), ((T, 2 * M_W), BF16),
        ((M_W, T), BF16), ((T, M_W), BF16), ((N_GATES, T), F32), ((T, SGU_W), BF16)]]
    ws_cat = jnp.transpose(sgu_w, (1, 0, 2)).reshape(BLK, SGU_GROUPS * BLK).astype(BF16)
    bs_full = jnp.repeat(jnp.transpose(sgu_b), HEAD_DIM, axis=1)
    return pl.pallas_call(
        _mixin_kernel,
        grid=(T // tm,),
        in_specs=[row(D_MODEL), _const_spec((1, D_MODEL)), _layer_spec(w_in_p.shape, layer),
                  _const_spec((1, ATTN_Q_W)), _const_spec((1, ATTN_KV_W)),
                  row(LANES), row(LANES), _const_spec(hmat.shape),
                  _const_spec((1, SGU_W)), _const_spec(ws_cat.shape), _const_spec(bs_full.shape)],
        out_specs=out_specs,
        out_shape=out_shape,
        compiler_params=_params(1),
        name="mix_in",
    )(x, g.reshape(1, D_MODEL), w_in_p,
      jnp.tile(q_g, N_Q_HEADS).reshape(1, ATTN_Q_W), jnp.tile(k_g, N_KV_HEADS).reshape(1, ATTN_KV_W),
      cos, sin, hmat, sgu_g.reshape(1, SGU_W), ws_cat, bs_full)


def _attn_kernel(sink_ref, q_ref, kd_ref, vt_ref, o_ref):
    S = q_ref.shape[0]
    nb = S // BLK
    kc = lax.broadcasted_iota(jnp.int32, (BLK, BLK), 0)
    qi = lax.broadcasted_iota(jnp.int32, (BLK, BLK), 1)
    prev_bias = jnp.tile(jnp.where(kc >= qi, 0.0, NEG), (1, Q_PER_KV))
    next_bias = jnp.tile(jnp.where(kc <= qi, 0.0, NEG), (1, Q_PER_KV))
    half_mask = (qi < HEAD_DIM, qi >= HEAD_DIM)
    head_lane = lax.broadcasted_iota(jnp.int32, (1, Q_PER_KV * BLK), 1) >> 7
    sink_rows = []
    for kvh in range(N_KV_HEADS):
        row = jnp.zeros((1, Q_PER_KV * BLK), F32)
        for g in range(Q_PER_KV):
            row = jnp.where(head_lane == g, sink_ref[kvh * Q_PER_KV + g], row)
        sink_rows.append(row)

    def block(q0, start, biases):
        nk = len(biases)
        ones = jnp.ones((HEAD_DIM, nk * BLK), BF16)
        outs = []
        for kvh in range(N_KV_HEADS):
            kd = kd_ref[pl.ds(start, nk * BLK), kvh * LANES:(kvh + 1) * LANES]
            v_ext = jnp.concatenate([vt_ref[kvh * HEAD_DIM:(kvh + 1) * HEAD_DIM, pl.ds(start, nk * BLK)], ones],
                                    axis=0)
            qs = []
            for g in range(Q_PER_KV):
                t = kvh * Q_PER_KV + g
                qt = q_ref[pl.ds(q0, BLK), (t // 2) * LANES:(t // 2 + 1) * LANES]
                qs.append(jnp.where(half_mask[t & 1], qt, jnp.zeros_like(qt)))
            s = _dot_nt(kd, jnp.concatenate(qs, axis=0))
            parts = [s[j * BLK:(j + 1) * BLK] for j in range(nk)]
            parts = [p if b is None else p + b for p, b in zip(parts, biases)]
            mx = jnp.max(functools.reduce(jnp.maximum, parts), axis=0, keepdims=True)
            m = jnp.maximum(mx, sink_rows[kvh])
            p = jnp.concatenate([jnp.exp(p - m) for p in parts], axis=0).astype(BF16)
            oe = _dot(v_ext, p)
            denom = oe[HEAD_DIM:HEAD_DIM + 1] + jnp.exp(sink_rows[kvh] - m)
            on = oe[0:HEAD_DIM] / denom
            for pair in range(Q_PER_KV // 2):
                two = jnp.concatenate([on[:, (2 * pair) * BLK:(2 * pair + 1) * BLK],
                                       on[:, (2 * pair + 1) * BLK:(2 * pair + 2) * BLK]], axis=0)
                outs.append(two.T)
        o_ref[pl.ds(q0, BLK), :] = jnp.concatenate(outs, axis=1).astype(BF16)

    def middle(n, carry):
        block(pl.multiple_of(n * BLK, BLK), pl.multiple_of((n - 1) * BLK, BLK), [prev_bias, None, next_bias])
        return carry

    block(0, 0, [None, next_bias])
    lax.fori_loop(1, nb - 1, middle, 0, unroll=2)
    block((nb - 1) * BLK, (nb - 2) * BLK, [prev_bias, None])


def _attention(qa, kd, vat, sink, B, S):
    seq = lambda w: pl.BlockSpec((S, w), lambda b: (b, 0))
    return pl.pallas_call(
        _attn_kernel,
        grid=(B,),
        in_specs=[pl.BlockSpec(memory_space=pltpu.SMEM), seq(ATTN_Q_W), seq(2 * ATTN_KV_W),
                  pl.BlockSpec((ATTN_KV_W, S), lambda b: (0, b))],
        out_specs=seq(ATTN_Q_W),
        out_shape=jax.ShapeDtypeStruct((B * S, ATTN_Q_W), BF16),
        compiler_params=_params(1),
        name="window_attn",
    )(sink, qa, kd, vat)


def _log_sigmoid(x):
    return jnp.minimum(x, 0.0) - jnp.log(1.0 + jnp.exp(-jnp.abs(x)))


def _mlstm_kernel(qk_ref, vt_ref, mo_ref, mgt_ref, gb_ref, hg_ref, o_ref,
                  gt_s, b_s, w_s, mloc_s, tot_s, rcol_s, cloc_s, cprev_s, mprev_s):
    S = qk_ref.shape[0]
    L = BLK
    nc = S // L
    ng = N_GATES
    sub = lax.broadcasted_iota(jnp.int32, (L, L), 0)
    lane = lax.broadcasted_iota(jnp.int32, (L, L), 1)

    for c in range(nc):
        gt_s[c * ng:(c + 1) * ng, :] = mgt_ref[:, c * L:(c + 1) * L] + gb_ref[...]
    gt = gt_s[...]
    row = lax.broadcasted_iota(jnp.int32, gt.shape, 0)
    gt = jnp.where(((row >> 2) & 1) == 1, _log_sigmoid(gt), gt)
    upper = jnp.where(sub <= lane, 1.0, 0.0).astype(BF16)
    g1 = gt.astype(BF16)
    r1 = gt - g1.astype(F32)
    g2 = r1.astype(BF16)
    g3 = (r1 - g2.astype(F32)).astype(BF16)
    pre = _dot(g1, upper) + _dot(g2, upper) + _dot(g3, upper)
    tot = jnp.broadcast_to(pre[:, L - 1:L], pre.shape)
    bsum =jnp.where((row & (ng - 1)) < ng // 2, pre, tot - pre + gt)
    li = pltpu.roll(gt, M_HEADS, 0)
    a = tot - bsum + li
    mloc = jnp.broadcast_to(jnp.max(a, axis=1, keepdims=True), a.shape)
    b_s[...] = bsum
    w_s[...] = jnp.exp(a - mloc)
    mloc_s[...] = mloc
    tot_s[...] = tot
    rdiff = li - bsum
    zpad = jnp.zeros((L - ng, L), F32)
    for c in range(nc):
        rcol_s[c * L:(c + 1) * L, :] = jnp.concatenate([rdiff[c * ng:(c + 1) * ng, :], zpad], axis=0).T

    def value_rows(hd, r0, dtype):
        vt = vt_ref[hd * HEAD_DIM:(hd + 1) * HEAD_DIM, pl.ds(r0, L)].astype(dtype)
        return jnp.concatenate([vt, jnp.ones((HEAD_DIM, L), dtype)], axis=0)

    def local_state(c, carry):
        r0 = pl.multiple_of(c * L, L)
        g0 = pl.multiple_of(c * ng, ng)
        k = qk_ref[pl.ds(r0, L), M_W:2 * M_W]
        w = w_s[pl.ds(g0, ng), :]
        for hd in range(M_HEADS):
            vte = value_rows(hd, r0, F32)
            lhs = jnp.concatenate([vte * w[4 + hd:5 + hd, :], vte * w[12 + hd:13 + hd, :]], axis=0)
            cl = _dot(lhs.astype(BF16), k[:, (hd // 2) * LANES:(hd // 2 + 1) * LANES])
            cloc_s[0, hd, c] = cl[0:L]
            cloc_s[1, hd, c] = cl[L:2 * L]
        return carry

    lax.fori_loop(0, nc, local_state, 0)

    jrow = lax.broadcasted_iota(jnp.int32, (ng, L), 0)

    def scan(i, carry):
        states, m = carry
        cf = i
        cb = nc - 1 - i
        gf = pl.multiple_of(cf * ng, ng)
        gb = pl.multiple_of(cb * ng, ng)
        is_fwd = jrow < ng // 2
        blast = jnp.where(is_fwd, tot_s[pl.ds(gf, ng), :], tot_s[pl.ds(gb, ng), :])
        mloc_i = jnp.where(is_fwd, mloc_s[pl.ds(gf, ng), :], mloc_s[pl.ds(gb, ng), :])
        mprev_s[0, pl.ds(gf, ng), :] = m
        mprev_s[1, pl.ds(gb, ng), :] = m
        m_new = jnp.maximum(blast + m, mloc_i)
        sa = jnp.exp(blast + m - m_new)
        sb = jnp.exp(mloc_i - m_new)
        new_states = []
        for d in range(2):
            ci = cf if d == 0 else cb
            for hd in range(M_HEADS):
                r = 4 + hd + 8 * d
                st = states[d * M_HEADS + hd]
                cprev_s[d, hd, ci] = st.astype(BF16)
                new_states.append(sa[r:r + 1, :] * st + sb[r:r + 1, :] * cloc_s[d, hd, ci])
        return tuple(new_states), m_new

    init = (tuple(jnp.zeros((L, LANES), F32) for _ in range(2 * M_HEADS)), jnp.zeros((ng, L), F32))
    lax.fori_loop(0, nc, scan, init)

    causal = (sub <= lane, sub >= lane)

    def outputs(c, carry):
        r0 = pl.multiple_of(c * L, L)
        g0 = pl.multiple_of(c * ng, ng)
        q = qk_ref[pl.ds(r0, L), 0:M_W]
        k = qk_ref[pl.ds(r0, L), M_W:2 * M_W]
        bc = b_s[pl.ds(g0, ng), :]
        mprev = (mprev_s[0, pl.ds(g0, ng), :], mprev_s[1, pl.ds(g0, ng), :])
        rc = rcol_s[pl.ds(r0, L), :]
        houts = []
        for hd in range(M_HEADS):
            t = hd // 2
            qp = q[:, t * LANES:(t + 1) * LANES]
            qm = jnp.where((lane >> 6) == (hd & 1), qp, jnp.zeros_like(qp))
            x = jnp.concatenate([k[:, t * LANES:(t + 1) * LANES], cprev_s[0, hd, c], cprev_s[1, hd, c]], axis=0)
            y = _dot_nt(x, qm)
            st = y[0:L]
            pts, scs, mts = [], [], []
            for d in range(2):
                r = 4 + hd + 8 * d
                b_row = bc[r:r + 1, :]
                dm = jnp.where(causal[d], rc[:, r:r + 1] + b_row, NEG)
                inter = b_row + mprev[d][r:r + 1, :]
                m_t = jnp.maximum(inter, jnp.max(dm, axis=0, keepdims=True))
                pts.append((jnp.exp(dm - m_t) * st).astype(BF16))
                scs.append(jnp.exp(inter - m_t))
                mts.append(m_t)
            n2 = _dot(value_rows(hd, r0, BF16), jnp.concatenate(pts, axis=1))
            ht = None
            for d in range(2):
                tt = n2[:, d * L:(d + 1) * L] + scs[d] * y[(d + 1) * L:(d + 2) * L]
                den = tt[HEAD_DIM:HEAD_DIM + 1, :]
                hd_out = tt[0:HEAD_DIM] / jnp.maximum(jnp.abs(den), jnp.exp(-mts[d]))
                ht = hd_out if ht is None else ht + hd_out
            ms = jnp.mean(ht * ht, axis=0, keepdims=True)
            houts.append(ht * lax.rsqrt(ms + EPS))
        hn = (jnp.concatenate(houts, axis=0) * hg_ref[...]).T
        o_ref[pl.ds(r0, L), :] = (jax.nn.sigmoid(mo_ref[pl.ds(r0, L), :].astype(F32)) * hn).astype(BF16)
        return carry

    lax.fori_loop(0, nc, outputs, 0, unroll=2)


def _mlstm(mqk, mvt, mo, mgt, gate_b, head_g, B, S):
    seq = lambda w: pl.BlockSpec((S, w), lambda b: (b, 0))
    tseq = lambda r: pl.BlockSpec((r, S), lambda b: (0, b))
    nc = S // BLK
    gb = jnp.broadcast_to(gate_b.reshape(N_GATES, 1), (N_GATES, BLK))
    hg_t = jnp.broadcast_to(head_g.reshape(M_W, 1), (M_W, BLK))
    rows = nc * N_GATES
    return pl.pallas_call(
        _mlstm_kernel,
        grid=(B,),
        in_specs=[seq(2 * M_W), tseq(M_W), seq(M_W), tseq(N_GATES), _const_spec((N_GATES, BLK)),
                  _const_spec((M_W, BLK))],
        out_specs=seq(M_W),
        out_shape=jax.ShapeDtypeStruct((B * S, M_W), BF16),
        scratch_shapes=[pltpu.VMEM((rows, BLK), F32),
                        pltpu.VMEM((rows, BLK), F32),
                        pltpu.VMEM((rows, BLK), F32),
                        pltpu.VMEM((rows, BLK), F32),
                        pltpu.VMEM((rows, BLK), F32),
                        pltpu.VMEM((S, LANES), F32),
                        pltpu.VMEM((2, M_HEADS, nc, BLK, LANES), F32),
                        pltpu.VMEM((2, M_HEADS, nc, BLK, LANES), BF16),
                        pltpu.VMEM((2, rows, BLK), F32)],
        compiler_params=_params(1),
        name="mlstm",
    )(mqk, mvt, mo, mgt, gb, hg_t)


def _permute_w_in(w_in):
    g0 = ATTN_Q_W + 2 * ATTN_KV_W + 4 * M_W
    pad = jnp.zeros(w_in.shape[:-1] + (GATE_PAD - N_GATES,), w_in.dtype)
    return jnp.concatenate([w_in[..., :g0], w_in[..., g0 + N_GATES:], w_in[..., g0:g0 + N_GATES], pad], axis=-1)


def kernel(x, positions, norm_ffn1_g, ffn1_w_gate, ffn1_w_up, ffn1_w_down, norm_mix_g, w_in, q_norm_g, k_norm_g, attn_sink, mlstm_gate_b, mlstm_head_g, sgu_norm_g, sgu_w_s, sgu_b, w_out, norm_ffn2_g, ffn2_w_gate, ffn2_w_up, ffn2_w_down, norm_out_g):
    B, S, D = x.shape
    depth = w_in.shape[0]
    T = B * S
    xt = x.reshape(T, D)
    cos, sin = _rope_tables(positions)
    group = np.arange(M_W) // HEAD_DIM
    hmat = jnp.asarray((group[:, None] == group[None, :]) / HEAD_DIM, BF16)
    w1g, w1u, w1d = ffn1_w_gate.astype(BF16), ffn1_w_up.astype(BF16), ffn1_w_down.astype(BF16)
    w2g, w2u, w2d = ffn2_w_gate.astype(BF16), ffn2_w_up.astype(BF16), ffn2_w_down.astype(BF16)
    w_in_p = _permute_w_in(w_in.astype(BF16))
    w_o = w_out.astype(BF16)
    for l in range(depth):
        xt = _ffn(xt, l, norm_ffn1_g[l], w1g, w1u, w1d)
        qa, kd, vat, mqk, mvt, mo, mgt, ys = _mix_in(xt, l, norm_mix_g[l], w_in_p, q_norm_g[l], k_norm_g[l], cos, sin,
                                                     hmat, sgu_norm_g[l], sgu_w_s[l], sgu_b[l])
        ya = _attention(qa, kd, vat, attn_sink[l], B, S)
        ym = _mlstm(mqk, mvt, mo, mgt, mlstm_gate_b[l], mlstm_head_g[l], B, S)
        xt = _ffn(xt, l, norm_ffn2_g[l], w2g, w2u, w2d, proj=(ya, ym, ys, w_o), gout=norm_out_g[l])
    return xt.reshape(B, S, D)
```

```python
import functools

import numpy as np
import jax
import jax.numpy as jnp
from jax import lax
from jax.experimental import pallas as pl
from jax.experimental.pallas import tpu as pltpu

F32 = jnp.float32
BF16 = jnp.bfloat16

D_MODEL = 1024
HEAD_DIM = 64
N_Q_HEADS = 8
N_KV_HEADS = 2
Q_PER_KV = N_Q_HEADS // N_KV_HEADS
ATTN_Q_W = N_Q_HEADS * HEAD_DIM
ATTN_KV_W = N_KV_HEADS * HEAD_DIM
WINDOW = 128
BLK = 128
ROPE_THETA = 10000.0
M_HEADS = 4
M_W = M_HEADS * HEAD_DIM
N_GATES = 4 * M_HEADS
SGU_GROUPS = 4
SGU_W = SGU_GROUPS * HEAD_DIM
D_FF = 2816
EPS = 1e-6
LANES = 128
GATE_PAD = LANES
D_IN_PAD = ATTN_Q_W + 2 * ATTN_KV_W + 4 * M_W + 2 * SGU_W + GATE_PAD
NEG = -1e30

VMEM_LIMIT = 56 * 1024 * 1024
FFN_TM = 512
MIX_TM = 512


def _dot(a, b):
    return jnp.dot(a, b, preferred_element_type=F32)


def _dot_nt(a, b):
    return lax.dot_general(a, b, (((1,), (1,)), ((), ())), preferred_element_type=F32)


def _dot_tn(a, b):
    return lax.dot_general(a, b, (((0,), (0,)), ((), ())), preferred_element_type=F32)


def _rms(x, g):
    ms = jnp.mean(x * x, axis=-1, keepdims=True)
    return x * lax.rsqrt(ms + EPS) * g


def _group_mean_sq(t, hmat):
    return _dot((t * t).astype(BF16), hmat)


def _const_spec(shape):
    nd = len(shape)
    return pl.BlockSpec(shape, lambda *_: (0,) * nd, pipeline_mode=pl.Buffered(1))


def _layer_spec(stacked_shape, layer):
    nd = len(stacked_shape) - 1
    return pl.BlockSpec((None,) + tuple(stacked_shape[1:]), lambda *_: (layer,) + (0,) * nd,
                        pipeline_mode=pl.Buffered(1))


def _params(n_grid):
    return pltpu.CompilerParams(dimension_semantics=("parallel",) * n_grid,
                                vmem_limit_bytes=VMEM_LIMIT)


def _rope_kernel(pos_ref, freq_ref, cos_ref, sin_ref):
    ang = pos_ref[...] * freq_ref[...]
    cos_ref[...] = jnp.cos(ang)
    sin_ref[...] = jnp.sin(ang)


def _rope_tables(positions):
    T = positions.size
    half = HEAD_DIM // 2
    per_row = LANES // half
    freqs = ROPE_THETA ** (-jnp.arange(0, HEAD_DIM, 2, dtype=F32) / HEAD_DIM)
    pos = jnp.repeat(positions.reshape(T // per_row, per_row).astype(F32), half, axis=1)
    freq_row = jnp.tile(freqs, per_row).reshape(1, LANES)
    rows = T // per_row
    cos, sin = pl.pallas_call(
        _rope_kernel,
        grid=(1,),
        in_specs=[pl.BlockSpec((rows, LANES), lambda i: (0, 0)),
                  pl.BlockSpec((1, LANES), lambda i: (0, 0))],
        out_specs=[pl.BlockSpec((rows, LANES), lambda i: (0, 0))] * 2,
        out_shape=[jax.ShapeDtypeStruct((rows, LANES), F32)] * 2,
        compiler_params=_params(1),
        name="rope_tables",
    )(pos, freq_row)
    cos = jnp.tile(cos.reshape(T, half), (1, per_row))
    sin = jnp.tile(sin.reshape(T, half), (1, per_row))
    return cos, sin


def _ffn_kernel(*refs, has_proj, has_final):
    refs = list(refs)
    x_ref = refs.pop(0)
    if has_proj:
        ya_ref, ym_ref, ys_ref, wo_ref = refs[:4]
        refs = refs[4:]
    g_ref, wg_ref, wu_ref, wd_ref = refs[:4]
    refs = refs[4:]
    if has_final:
        gout_ref = refs.pop(0)
    out_ref = refs.pop(0)

    x = x_ref[...]
    if has_proj:
        x = (x + _dot(ya_ref[...], wo_ref[0:ATTN_Q_W, :])
             + _dot(ym_ref[...], wo_ref[ATTN_Q_W:ATTN_Q_W + M_W, :])
             + _dot(ys_ref[...], wo_ref[ATTN_Q_W + M_W:, :]))
    h = _rms(x, g_ref[...]).astype(BF16)
    gate = _dot(h, wg_ref[...])
    up = _dot(h, wu_ref[...])
    act = (gate * jax.nn.sigmoid(gate) * up).astype(BF16)
    y = x + 0.5 * _dot(act, wd_ref[...])
    if has_final:
        y = _rms(y, gout_ref[...])
    out_ref[...] = y


def _ffn(x, layer, g, wg, wu, wd, proj=None, gout=None, tm=FFN_TM):
    T = x.shape[0]
    row = lambda w: pl.BlockSpec((tm, w), lambda i: (i, 0))
    args, specs = [x], [row(D_MODEL)]
    if proj is not None:
        ya, ym, ys, wo = proj
        args += [ya, ym, ys, wo]
        specs += [row(ATTN_Q_W), row(M_W), row(SGU_W), _layer_spec(wo.shape, layer)]
    args += [g.reshape(1, D_MODEL), wg, wu, wd]
    specs += [_const_spec((1, D_MODEL)), _layer_spec(wg.shape, layer), _layer_spec(wu.shape, layer),
              _layer_spec(wd.shape, layer)]
    if gout is not None:
        args.append(gout.reshape(1, D_MODEL))
        specs.append(_const_spec((1, D_MODEL)))
    return pl.pallas_call(
        functools.partial(_ffn_kernel, has_proj=proj is not None, has_final=gout is not None),
        grid=(T // tm,),
        in_specs=specs,
        out_specs=row(D_MODEL),
        out_shape=jax.ShapeDtypeStruct((T, D_MODEL), F32),
        compiler_params=_params(1),
        name="ffn",
    )(*args)


def _first_half(shape):
    lane = lax.broadcasted_iota(jnp.int32, shape, 1)
    return (lane & (HEAD_DIM - 1)) < HEAD_DIM // 2


def _rope(t, cos, sin_signed):
    n = t.shape[-1]
    reps = n // LANES
    swapped = jnp.where(_first_half(t.shape),
                        pltpu.roll(t, n - HEAD_DIM // 2, 1), pltpu.roll(t, HEAD_DIM // 2, 1))
    return t * jnp.tile(cos, (1, reps)) + swapped * jnp.tile(sin_signed, (1, reps))


def _gelu(x):
    return 0.5 * x * (1.0 + lax.erf(x * np.float32(np.sqrt(0.5))))


def _mixin_kernel(x_ref, g_ref, w_ref, qg_ref, kg_ref, cos_ref, sin_ref, hmat_ref, ng_ref, ws_ref, bs_ref,
                  qa_ref, kd_ref, vat_ref, mqk_ref, mvt_ref, mo_ref, mgt_ref, ys_ref):
    h = _rms(x_ref[...], g_ref[...]).astype(BF16)
    cos = cos_ref[...]
    sin_signed = jnp.where(_first_half(cos.shape), -sin_ref[...], sin_ref[...])
    scale = HEAD_DIM ** -0.5
    hmat = hmat_ref[...]
    gw = hmat.shape[0]

    col = [0]

    def proj(width):
        z = _dot(h, w_ref[:, col[0]:col[0] + width])
        col[0] += width
        return z

    aq = proj(ATTN_Q_W)
    q_ms = jnp.concatenate([_group_mean_sq(aq[:, i:i + gw], hmat) for i in range(0, ATTN_Q_W, gw)], axis=1)
    qn = aq * lax.rsqrt(q_ms + EPS) * qg_ref[...]
    qa_ref[...] = (_rope(qn, cos, sin_signed) * scale).astype(BF16)

    akv = proj(2 * ATTN_KV_W)
    ak, av = akv[:, :ATTN_KV_W], akv[:, ATTN_KV_W:]
    kn = ak * lax.rsqrt(_group_mean_sq(ak, hmat[0:ATTN_KV_W, 0:ATTN_KV_W]) + EPS) * kg_ref[...]
    kr = _rope(kn, cos, sin_signed)
    ks = pltpu.roll(kr, HEAD_DIM, 1)
    low = lax.broadcasted_iota(jnp.int32, kr.shape, 1) < HEAD_DIM
    kd_ref[:, 0:LANES] = jnp.where(low, kr, ks).astype(BF16)
    kd_ref[:, LANES:2 * LANES] = jnp.where(low, ks, kr).astype(BF16)

    vat_ref[...] = av.T.astype(BF16)
    mqk_ref[:, 0:M_W] = proj(M_W).astype(BF16)
    mqk_ref[:, M_W:2 * M_W] = (proj(M_W) * scale).astype(BF16)
    mvt_ref[...] = proj(M_W).T.astype(BF16)
    mo_ref[...] = proj(M_W).astype(BF16)
    su = proj(SGU_W)
    sv = proj(SGU_W)
    mgt_ref[...] = proj(GATE_PAD).T[0:N_GATES, :]

    u = _gelu(su)
    v = _gelu(sv)
    vn = (v * lax.rsqrt(_group_mean_sq(v, hmat) + EPS) * ng_ref[...]).astype(BF16)
    lane_group = lax.broadcasted_iota(jnp.int32, (BLK, SGU_W), 1) >> 6
    for c in range(su.shape[0] // BLK):
        vc = vn[c * BLK:(c + 1) * BLK]
        stack = jnp.concatenate([jnp.where(lane_group == g, vc, jnp.zeros_like(vc)) for g in range(SGU_GROUPS)],
                                axis=0)
        mixed = _dot(ws_ref[...], stack)
        ys_ref[c * BLK:(c + 1) * BLK, :] = (u[c * BLK:(c + 1) * BLK] * (mixed + bs_ref[...])).astype(BF16)


def _mix_in(x, layer, g, w_in_p, q_g, k_g, cos, sin, hmat, sgu_g, sgu_w, sgu_b, tm=MIX_TM):
    T = x.shape[0]
    row = lambda w: pl.BlockSpec((tm, w), lambda i: (i, 0))
    tcol = lambda r: pl.BlockSpec((r, tm), lambda i: (0, i))
    out_specs = [row(ATTN_Q_W), row(2 * ATTN_KV_W), tcol(ATTN_KV_W), row(2 * M_W), tcol(M_W), row(M_W),
                 tcol(N_GATES), row(SGU_W)]
    out_shape = [jax.ShapeDtypeStruct(s, d) for s, d in [
        ((T, ATTN_Q_W), BF16), ((T, 2 * ATTN_KV_W), BF16), ((ATTN_KV_W, T), BF16), ((T, 2 * M_W), BF16),
        ((M_W, T), BF16), ((T, M_W), BF16), ((N_GATES, T), F32), ((T, SGU_W), BF16)]]
    ws_cat = jnp.transpose(sgu_w, (1, 0, 2)).reshape(BLK, SGU_GROUPS * BLK).astype(BF16)
    bs_full = jnp.repeat(jnp.transpose(sgu_b), HEAD_DIM, axis=1)
    return pl.pallas_call(
        _mixin_kernel,
        grid=(T // tm,),
        in_specs=[row(D_MODEL), _const_spec((1, D_MODEL)), _layer_spec(w_in_p.shape, layer),
                  _const_spec((1, ATTN_Q_W)), _const_spec((1, ATTN_KV_W)),
                  row(LANES), row(LANES), _const_spec(hmat.shape),
                  _const_spec((1, SGU_W)), _const_spec(ws_cat.shape), _const_spec(bs_full.shape)],
        out_specs=out_specs,
        out_shape=out_shape,
        compiler_params=_params(1),
        name="mix_in",
    )(x, g.reshape(1, D_MODEL), w_in_p,
      jnp.tile(q_g, N_Q_HEADS).reshape(1, ATTN_Q_W), jnp.tile(k_g, N_KV_HEADS).reshape(1, ATTN_KV_W),
      cos, sin, hmat, sgu_g.reshape(1, SGU_W), ws_cat, bs_full)


def _attn_blocks(sink_ref, q_ref, kd_ref, vt_ref, o_ref):
    S = q_ref.shape[0]
    nb = S // BLK
    kc = lax.broadcasted_iota(jnp.int32, (BLK, BLK), 0)
    qi = lax.broadcasted_iota(jnp.int32, (BLK, BLK), 1)
    prev_bias = jnp.tile(jnp.where(kc >= qi, 0.0, NEG), (1, Q_PER_KV))
    next_bias = jnp.tile(jnp.where(kc <= qi, 0.0, NEG), (1, Q_PER_KV))
    half_mask = (qi < HEAD_DIM, qi >= HEAD_DIM)
    head_lane = lax.broadcasted_iota(jnp.int32, (1, Q_PER_KV * BLK), 1) >> 7
    sink_rows = []
    for kvh in range(N_KV_HEADS):
        row = jnp.zeros((1, Q_PER_KV * BLK), F32)
        for g in range(Q_PER_KV):
            row = jnp.where(head_lane == g, sink_ref[kvh * Q_PER_KV + g], row)
        sink_rows.append(row)

    def block(q0, start, biases):
        nk = len(biases)
        scores = []
        for kvh in range(N_KV_HEADS):
            kd = kd_ref[pl.ds(start, nk * BLK), kvh * LANES:(kvh + 1) * LANES]
            qs = []
            for g in range(Q_PER_KV):
                t = kvh * Q_PER_KV + g
                qt = q_ref[pl.ds(q0, BLK), (t // 2) * LANES:(t // 2 + 1) * LANES]
                qs.append(jnp.where(half_mask[t & 1], qt, jnp.zeros_like(qt)))
            scores.append(_dot_nt(kd, jnp.concatenate(qs, axis=0)))

        def weighted_values():
            ones = jnp.ones((HEAD_DIM, nk * BLK), BF16)
            outs = []
            for kvh in range(N_KV_HEADS):
                v_ext = jnp.concatenate([vt_ref[kvh * HEAD_DIM:(kvh + 1) * HEAD_DIM, pl.ds(start, nk * BLK)], ones],
                                        axis=0)
                parts = [scores[kvh][j * BLK:(j + 1) * BLK] for j in range(nk)]
                parts = [p if b is None else p + b for p, b in zip(parts, biases)]
                mx = jnp.max(functools.reduce(jnp.maximum, parts), axis=0, keepdims=True)
                m = jnp.maximum(mx, sink_rows[kvh])
                p = jnp.concatenate([jnp.exp(p - m) for p in parts], axis=0).astype(BF16)
                oe = _dot(v_ext, p)
                denom = oe[HEAD_DIM:HEAD_DIM + 1] + jnp.exp(sink_rows[kvh] - m)
                on = oe[0:HEAD_DIM] / denom
                for pair in range(Q_PER_KV // 2):
                    two = jnp.concatenate([on[:, (2 * pair) * BLK:(2 * pair + 1) * BLK],
                                           on[:, (2 * pair + 1) * BLK:(2 * pair + 2) * BLK]], axis=0)
                    outs.append(two.T)
            result = jnp.concatenate(outs, axis=1).astype(BF16)

            def commit():
                o_ref[pl.ds(q0, BLK), :] = result
            return commit

        return weighted_values

    def attend(n):
        if isinstance(n, int) and n == 0:
            return block(0, 0, [None, next_bias])
        if isinstance(n, int) and n == nb - 1:
            return block(n * BLK, (n - 1) * BLK, [prev_bias, None])
        return block(pl.multiple_of(n * BLK, BLK), pl.multiple_of((n - 1) * BLK, BLK), [prev_bias, None, next_bias])

    return attend


def _log_sigmoid(x):
    return jnp.minimum(x, 0.0) - jnp.log(1.0 + jnp.exp(-jnp.abs(x)))


def _mlstm_chunks(qk_ref, vt_ref, mo_ref, mgt_ref, gb_ref, hg_ref, o_ref,
                  gt_s, b_s, w_s, mloc_s, tot_s, rcol_s, cloc_s, cprev_s, mprev_s):
    S = qk_ref.shape[0]
    L = BLK
    nc = S // L
    ng = N_GATES
    sub = lax.broadcasted_iota(jnp.int32, (L, L), 0)
    lane = lax.broadcasted_iota(jnp.int32, (L, L), 1)

    for c in range(nc):
        gt_s[c * ng:(c + 1) * ng, :] = mgt_ref[:, c * L:(c + 1) * L] + gb_ref[...]
    gt = gt_s[...]
    row = lax.broadcasted_iota(jnp.int32, gt.shape, 0)
    gt = jnp.where(((row >> 2) & 1) == 1, _log_sigmoid(gt), gt)
    upper = jnp.where(sub <= lane, 1.0, 0.0).astype(BF16)
    g1 = gt.astype(BF16)
    r1 = gt - g1.astype(F32)
    g2 = r1.astype(BF16)
    g3 = (r1 - g2.astype(F32)).astype(BF16)
    pre = _dot(g1, upper) + _dot(g2, upper) + _dot(g3, upper)
    tot = jnp.broadcast_to(pre[:, L - 1:L], pre.shape)
    bsum =jnp.where((row & (ng - 1)) < ng // 2, pre, tot - pre + gt)
    li = pltpu.roll(gt, M_HEADS, 0)
    a = tot - bsum + li
    mloc = jnp.broadcast_to(jnp.max(a, axis=1, keepdims=True), a.shape)
    b_s[...] = bsum
    w_s[...] = jnp.exp(a - mloc)
    mloc_s[...] = mloc
    tot_s[...] = tot
    rdiff = li - bsum
    zpad = jnp.zeros((L - ng, L), F32)
    for c in range(nc):
        rcol_s[c * L:(c + 1) * L, :] = jnp.concatenate([rdiff[c * ng:(c + 1) * ng, :], zpad], axis=0).T

    def value_rows(hd, r0, dtype):
        vt = vt_ref[hd * HEAD_DIM:(hd + 1) * HEAD_DIM, pl.ds(r0, L)].astype(dtype)
        return jnp.concatenate([vt, jnp.ones((HEAD_DIM, L), dtype)], axis=0)

    def local_state(c, carry):
        r0 = pl.multiple_of(c * L, L)
        g0 = pl.multiple_of(c * ng, ng)
        k = qk_ref[pl.ds(r0, L), M_W:2 * M_W]
        w = w_s[pl.ds(g0, ng), :]
        for hd in range(M_HEADS):
            vte = value_rows(hd, r0, F32)
            lhs = jnp.concatenate([vte * w[4 + hd:5 + hd, :], vte * w[12 + hd:13 + hd, :]], axis=0)
            cl = _dot(lhs.astype(BF16), k[:, (hd // 2) * LANES:(hd // 2 + 1) * LANES])
            cloc_s[0, hd, c] = cl[0:L]
            cloc_s[1, hd, c] = cl[L:2 * L]
        return carry

    lax.fori_loop(0, nc, local_state, 0)

    jrow = lax.broadcasted_iota(jnp.int32, (ng, L), 0)

    def scan(i, carry):
        states, m = carry
        cf = i
        cb = nc - 1 - i
        gf = pl.multiple_of(cf * ng, ng)
        gb = pl.multiple_of(cb * ng, ng)
        is_fwd = jrow < ng // 2
        blast = jnp.where(is_fwd, tot_s[pl.ds(gf, ng), :], tot_s[pl.ds(gb, ng), :])
        mloc_i = jnp.where(is_fwd, mloc_s[pl.ds(gf, ng), :], mloc_s[pl.ds(gb, ng), :])
        mprev_s[0, pl.ds(gf, ng), :] = m
        mprev_s[1, pl.ds(gb, ng), :] = m
        m_new = jnp.maximum(blast + m, mloc_i)
        sa = jnp.exp(blast + m - m_new)
        sb = jnp.exp(mloc_i - m_new)
        new_states = []
        for d in range(2):
            ci = cf if d == 0 else cb
            for hd in range(M_HEADS):
                r = 4 + hd + 8 * d
                st = states[d * M_HEADS + hd]
                cprev_s[d, hd, ci] = st.astype(BF16)
                new_states.append(sa[r:r + 1, :] * st + sb[r:r + 1, :] * cloc_s[d, hd, ci])
        return tuple(new_states), m_new

    init = (tuple(jnp.zeros((L, LANES), F32) for _ in range(2 * M_HEADS)), jnp.zeros((ng, L), F32))
    lax.fori_loop(0, nc, scan, init)

    causal = (sub <= lane, sub >= lane)

    def outputs(c):
        r0 = pl.multiple_of(c * L, L)
        g0 = pl.multiple_of(c * ng, ng)
        q = qk_ref[pl.ds(r0, L), 0:M_W]
        k = qk_ref[pl.ds(r0, L), M_W:2 * M_W]
        bc = b_s[pl.ds(g0, ng), :]
        mprev = (mprev_s[0, pl.ds(g0, ng), :], mprev_s[1, pl.ds(g0, ng), :])
        rc = rcol_s[pl.ds(r0, L), :]
        ys = []
        for hd in range(M_HEADS):
            t = hd // 2
            qp = q[:, t * LANES:(t + 1) * LANES]
            qm = jnp.where((lane >> 6) == (hd & 1), qp, jnp.zeros_like(qp))
            x = jnp.concatenate([k[:, t * LANES:(t + 1) * LANES], cprev_s[0, hd, c], cprev_s[1, hd, c]], axis=0)
            ys.append(_dot_nt(x, qm))

        def combine():
            houts = []
            for hd in range(M_HEADS):
                y = ys[hd]
                st = y[0:L]
                pts, scs, mts = [], [], []
                for d in range(2):
                    r = 4 + hd + 8 * d
                    b_row = bc[r:r + 1, :]
                    dm = jnp.where(causal[d], rc[:, r:r + 1] + b_row, NEG)
                    inter = b_row + mprev[d][r:r + 1, :]
                    m_t = jnp.maximum(inter, jnp.max(dm, axis=0, keepdims=True))
                    pts.append((jnp.exp(dm - m_t) * st).astype(BF16))
                    scs.append(jnp.exp(inter - m_t))
                    mts.append(m_t)
                n2 = _dot(value_rows(hd, r0, BF16), jnp.concatenate(pts, axis=1))
                ht = None
                for d in range(2):
                    tt = n2[:, d * L:(d + 1) * L] + scs[d] * y[(d + 1) * L:(d + 2) * L]
                    den = tt[HEAD_DIM:HEAD_DIM + 1, :]
                    hd_out = tt[0:HEAD_DIM] / jnp.maximum(jnp.abs(den), jnp.exp(-mts[d]))
                    ht = hd_out if ht is None else ht + hd_out
                ms = jnp.mean(ht * ht, axis=0, keepdims=True)
                houts.append(ht * lax.rsqrt(ms + EPS))
            hn = (jnp.concatenate(houts, axis=0) * hg_ref[...]).T
            result = (jax.nn.sigmoid(mo_ref[pl.ds(r0, L), :].astype(F32)) * hn).astype(BF16)

            def commit():
                o_ref[pl.ds(r0, L), :] = result
            return commit

        return combine

    return outputs


def _mixers_kernel(sink_ref, q_ref, kd_ref, vat_ref, qk_ref, mvt_ref, mo_ref, mgt_ref, gb_ref, hg_ref,
                   ya_ref, ym_ref, *scratch):
    nb = q_ref.shape[0] // BLK
    attend = _attn_blocks(sink_ref, q_ref, kd_ref, vat_ref, ya_ref)
    outputs = _mlstm_chunks(qk_ref, mvt_ref, mo_ref, mgt_ref, gb_ref, hg_ref, ym_ref, *scratch)

    def run(blocks):
        second_stages = [f(n) for n in blocks for f in (attend, outputs)]
        commits = [stage() for stage in second_stages]
        for commit in commits:
            commit()

    def pair(i, carry):
        run((1 + 2 * i, 2 + 2 * i))
        return carry

    run((0,))
    lax.fori_loop(0, (nb - 2) // 2, pair, 0)
    run((nb - 1,))


def _mixers(qa, kd, vat, sink, mqk, mvt, mo, mgt, gate_b, head_g, B, S):
    seq = lambda w: pl.BlockSpec((S, w), lambda b: (b, 0))
    tseq = lambda r: pl.BlockSpec((r, S), lambda b: (0, b))
    nc = S // BLK
    assert nc % 2 == 0 and nc >= 4
    gb = jnp.broadcast_to(gate_b.reshape(N_GATES, 1), (N_GATES, BLK))
    hg_t = jnp.broadcast_to(head_g.reshape(M_W, 1), (M_W, BLK))
    rows = nc * N_GATES
    return pl.pallas_call(
        _mixers_kernel,
        grid=(B,),
        in_specs=[pl.BlockSpec(memory_space=pltpu.SMEM), seq(ATTN_Q_W), seq(2 * ATTN_KV_W), tseq(ATTN_KV_W),
                  seq(2 * M_W), tseq(M_W), seq(M_W), tseq(N_GATES), _const_spec((N_GATES, BLK)),
                  _const_spec((M_W, BLK))],
        out_specs=[seq(ATTN_Q_W), seq(M_W)],
        out_shape=[jax.ShapeDtypeStruct((B * S, ATTN_Q_W), BF16), jax.ShapeDtypeStruct((B * S, M_W), BF16)],
        scratch_shapes=[pltpu.VMEM((rows, BLK), F32),
                        pltpu.VMEM((rows, BLK), F32),
                        pltpu.VMEM((rows, BLK), F32),
                        pltpu.VMEM((rows, BLK), F32),
                        pltpu.VMEM((rows, BLK), F32),
                        pltpu.VMEM((S, LANES), F32),
                        pltpu.VMEM((2, M_HEADS, nc, BLK, LANES), F32),
                        pltpu.VMEM((2, M_HEADS, nc, BLK, LANES), BF16),
                        pltpu.VMEM((2, rows, BLK), F32)],
        compiler_params=_params(1),
        name="mixers",
    )(sink, qa, kd, vat, mqk, mvt, mo, mgt, gb, hg_t)


def _permute_w_in(w_in):
    g0 = ATTN_Q_W + 2 * ATTN_KV_W + 4 * M_W
    pad = jnp.zeros(w_in.shape[:-1] + (GATE_PAD - N_GATES,), w_in.dtype)
    return jnp.concatenate([w_in[..., :g0], w_in[..., g0 + N_GATES:], w_in[..., g0:g0 + N_GATES], pad], axis=-1)


def kernel(x, positions, norm_ffn1_g, ffn1_w_gate, ffn1_w_up, ffn1_w_down, norm_mix_g, w_in, q_norm_g, k_norm_g, attn_sink, mlstm_gate_b, mlstm_head_g, sgu_norm_g, sgu_w_s, sgu_b, w_out, norm_ffn2_g, ffn2_w_gate, ffn2_w_up, ffn2_w_down, norm_out_g):
    B, S, D = x.shape
    depth = w_in.shape[0]
    T = B * S
    xt = x.reshape(T, D)
    cos, sin = _rope_tables(positions)
    group = np.arange(M_W) // HEAD_DIM
    hmat = jnp.asarray((group[:, None] == group[None, :]) / HEAD_DIM, BF16)
    w1g, w1u, w1d = ffn1_w_gate.astype(BF16), ffn1_w_up.astype(BF16), ffn1_w_down.astype(BF16)
    w2g, w2u, w2d = ffn2_w_gate.astype(BF16), ffn2_w_up.astype(BF16), ffn2_w_down.astype(BF16)
    w_in_p = _permute_w_in(w_in.astype(BF16))
    w_o = w_out.astype(BF16)
    for l in range(depth):
        xt = _ffn(xt, l, norm_ffn1_g[l], w1g, w1u, w1d)
        qa, kd, vat, mqk, mvt, mo, mgt, ys = _mix_in(xt, l, norm_mix_g[l], w_in_p, q_norm_g[l], k_norm_g[l], cos, sin,
                                                     hmat, sgu_norm_g[l], sgu_w_s[l], sgu_b[l])
        ya, ym = _mixers(qa, kd, vat, attn_sink[l], mqk, mvt, mo, mgt, mlstm_gate_b[l], mlstm_head_g[l], B, S)
        xt = _ffn(xt, l, norm_ffn2_g[l], w2g, w2u, w2d, proj=(ya, ym, ys, w_o), gout=norm_out_g[l])
    return xt.reshape(B, S, D)
```

```python
import functools

import numpy as np
import jax
import jax.numpy as jnp
from jax import lax
from jax.experimental import pallas as pl
from jax.experimental.pallas import tpu as pltpu

F32 = jnp.float32
BF16 = jnp.bfloat16

D_MODEL = 1024
HEAD_DIM = 64
N_Q_HEADS = 8
N_KV_HEADS = 2
Q_PER_KV = N_Q_HEADS // N_KV_HEADS
ATTN_Q_W = N_Q_HEADS * HEAD_DIM
ATTN_KV_W = N_KV_HEADS * HEAD_DIM
WINDOW = 128
BLK = 128
ROPE_THETA = 10000.0
M_HEADS = 4
M_W = M_HEADS * HEAD_DIM
N_GATES = 4 * M_HEADS
SGU_GROUPS = 4
SGU_W = SGU_GROUPS * HEAD_DIM
D_FF = 2816
EPS = 1e-6
LANES = 128
GATE_PAD = LANES
MAIN_W = ATTN_Q_W + 2 * ATTN_KV_W + 4 * M_W
NEG = -1e30

VMEM_LIMIT = 56 * 1024 * 1024
FFN_TM = 512
FFN_SPLIT = 2
MIX_TM = 512
MIX_SPLIT = 1


def _dot(a, b):
    return jnp.dot(a, b, preferred_element_type=F32)


def _dot_nt(a, b):
    return lax.dot_general(a, b, (((1,), (1,)), ((), ())), preferred_element_type=F32)


def _dot_tn(a, b):
    return lax.dot_general(a, b, (((0,), (0,)), ((), ())), preferred_element_type=F32)


def _rms(x, g):
    ms = jnp.mean(x * x, axis=-1, keepdims=True)
    return x * lax.rsqrt(ms + EPS) * g


def _group_mean_sq(t, hmat):
    return _dot((t * t).astype(BF16), hmat)


def _const_spec(shape):
    nd = len(shape)
    return pl.BlockSpec(shape, lambda *_: (0,) * nd, pipeline_mode=pl.Buffered(1))


def _layer_spec(stacked_shape, layer):
    nd = len(stacked_shape) - 1
    return pl.BlockSpec((None,) + tuple(stacked_shape[1:]), lambda *_: (layer,) + (0,) * nd,
                        pipeline_mode=pl.Buffered(1))


def _params(n_grid):
    return pltpu.CompilerParams(dimension_semantics=("parallel",) * n_grid,
                                vmem_limit_bytes=VMEM_LIMIT)


def _rope_kernel(pos_ref, freq_ref, cos_ref, sin_ref):
    ang = pos_ref[...] * freq_ref[...]
    cos_ref[...] = jnp.cos(ang)
    sin_ref[...] = jnp.sin(ang)


def _rope_tables(positions):
    T = positions.size
    half = HEAD_DIM // 2
    per_row = LANES // half
    freqs = ROPE_THETA ** (-jnp.arange(0, HEAD_DIM, 2, dtype=F32) / HEAD_DIM)
    pos = jnp.repeat(positions.reshape(T // per_row, per_row).astype(F32), half, axis=1)
    freq_row = jnp.tile(freqs, per_row).reshape(1, LANES)
    rows = T // per_row
    cos, sin = pl.pallas_call(
        _rope_kernel,
        grid=(1,),
        in_specs=[pl.BlockSpec((rows, LANES), lambda i: (0, 0)),
                  pl.BlockSpec((1, LANES), lambda i: (0, 0))],
        out_specs=[pl.BlockSpec((rows, LANES), lambda i: (0, 0))] * 2,
        out_shape=[jax.ShapeDtypeStruct((rows, LANES), F32)] * 2,
        compiler_params=_params(1),
        name="rope_tables",
    )(pos, freq_row)
    cos = jnp.tile(cos.reshape(T, half), (1, per_row))
    sin = jnp.tile(sin.reshape(T, half), (1, per_row))
    return cos, sin


def _ffn_kernel(*refs, has_proj, has_final):
    refs = list(refs)
    x_ref = refs.pop(0)
    if has_proj:
        ya_ref, ym_ref, ys_ref, wo_ref = refs[:4]
        refs = refs[4:]
    g_ref, wg_ref, wu_ref, wd_ref = refs[:4]
    refs = refs[4:]
    if has_final:
        gout_ref = refs.pop(0)
    out_ref = refs.pop(0)

    hs = x_ref.shape[0] // FFN_SPLIT
    rows = [slice(i * hs, (i + 1) * hs) for i in range(FFN_SPLIT)]
    xs = []
    for r in rows:
        x = x_ref[r, :]
        if has_proj:
            x = (x + _dot(ya_ref[r, :], wo_ref[0:ATTN_Q_W, :])
                 + _dot(ym_ref[r, :], wo_ref[ATTN_Q_W:ATTN_Q_W + M_W, :])
                 + _dot(ys_ref[r, :], wo_ref[ATTN_Q_W + M_W:, :]))
        xs.append(x)
    hidden = [_rms(x, g_ref[...]).astype(BF16) for x in xs]
    gate_up = [(_dot(h, wg_ref[...]), _dot(h, wu_ref[...])) for h in hidden]
    for r, x, (gate, up) in zip(rows, xs, gate_up):
        act = (gate * jax.nn.sigmoid(gate) * up).astype(BF16)
        y = x + 0.5 * _dot(act, wd_ref[...])
        if has_final:
            y = _rms(y, gout_ref[...])
        out_ref[r, :] = y


def _ffn(x, layer, g, wg, wu, wd, proj=None, gout=None, tm=FFN_TM):
    T = x.shape[0]
    row = lambda w: pl.BlockSpec((tm, w), lambda i: (i, 0))
    args, specs = [x], [row(D_MODEL)]
    if proj is not None:
        ya, ym, ys, wo = proj
        args += [ya, ym, ys, wo]
        specs += [row(ATTN_Q_W), row(M_W), row(SGU_W), _layer_spec(wo.shape, layer)]
    args += [g.reshape(1, D_MODEL), wg, wu, wd]
    specs += [_const_spec((1, D_MODEL)), _layer_spec(wg.shape, layer), _layer_spec(wu.shape, layer),
              _layer_spec(wd.shape, layer)]
    if gout is not None:
        args.append(gout.reshape(1, D_MODEL))
        specs.append(_const_spec((1, D_MODEL)))
    return pl.pallas_call(
        functools.partial(_ffn_kernel, has_proj=proj is not None, has_final=gout is not None),
        grid=(T // tm,),
        in_specs=specs,
        out_specs=row(D_MODEL),
        out_shape=jax.ShapeDtypeStruct((T, D_MODEL), F32),
        compiler_params=_params(1),
        name="ffn",
    )(*args)


def _first_half(shape):
    lane = lax.broadcasted_iota(jnp.int32, shape, 1)
    return (lane & (HEAD_DIM - 1)) < HEAD_DIM // 2


def _rope(t, cos, sin_signed):
    n = t.shape[-1]
    reps = n // LANES
    swapped = jnp.where(_first_half(t.shape),
                        pltpu.roll(t, n - HEAD_DIM // 2, 1), pltpu.roll(t, HEAD_DIM // 2, 1))
    return t * jnp.tile(cos, (1, reps)) + swapped * jnp.tile(sin_signed, (1, reps))


def _gelu(x):
    return 0.5 * x * (1.0 + lax.erf(x * np.float32(np.sqrt(0.5))))


def _mixin_kernel(x_ref, g_ref, w_ref, wsg_ref, wgt_ref, qg_ref, kg_ref, cos_ref, sin_ref, hmat_ref, ng_ref, ws_ref,
                  bs_ref, qa_ref, kd_ref, vat_ref, mqk_ref, mvt_ref, mo_ref, mgt_ref, ys_ref):
    scale = HEAD_DIM ** -0.5
    hmat = hmat_ref[...]
    gw = hmat.shape[0]
    c_q, c_kv, c_mq, c_mk, c_mv, c_mo = (int(c) for c in np.cumsum([0, ATTN_Q_W, 2 * ATTN_KV_W, M_W, M_W, M_W]))
    lane_group = lax.broadcasted_iota(jnp.int32, (BLK, SGU_W), 1) >> 6

    def projections(r0, nrows):
        r = slice(r0, r0 + nrows)
        h = _rms(x_ref[r, :], g_ref[...]).astype(BF16)
        proj = lambda ref, start, width: _dot(h, ref[:, start:start + width])
        su = proj(wsg_ref, 0, SGU_W)
        sv = proj(wsg_ref, SGU_W, SGU_W)
        aq = proj(w_ref, c_q, ATTN_Q_W)
        akv = proj(w_ref, c_kv, 2 * ATTN_KV_W)
        ak, av = akv[:, :ATTN_KV_W], akv[:, ATTN_KV_W:]
        v = _gelu(sv)
        v_ms = _group_mean_sq(v, hmat)
        mqk_ref[r, 0:M_W] = proj(w_ref, c_mq, M_W).astype(BF16)
        mqk_ref[r, M_W:2 * M_W] = (proj(w_ref, c_mk, M_W) * scale).astype(BF16)
        q_ms = jnp.concatenate([_group_mean_sq(aq[:, i:i + gw], hmat) for i in range(0, ATTN_Q_W, gw)], axis=1)
        k_ms = _group_mean_sq(ak, hmat[0:ATTN_KV_W, 0:ATTN_KV_W])
        mvt_ref[:, r] = proj(w_ref, c_mv, M_W).T.astype(BF16)
        mo_ref[r, :] = proj(w_ref, c_mo, M_W).astype(BF16)
        mgt_ref[:, r] = proj(wgt_ref, 0, GATE_PAD).T[0:N_GATES, :]

        def tail():
            cos = cos_ref[r, :]
            sin_signed = jnp.where(_first_half(cos.shape), -sin_ref[r, :], sin_ref[r, :])
            qn = aq * lax.rsqrt(q_ms + EPS) * qg_ref[...]
            qa_ref[r, :] = (_rope(qn, cos, sin_signed) * scale).astype(BF16)
            kn = ak * lax.rsqrt(k_ms + EPS) * kg_ref[...]
            kr = _rope(kn, cos, sin_signed)
            ks = pltpu.roll(kr, HEAD_DIM, 1)
            low = lax.broadcasted_iota(jnp.int32, kr.shape, 1) < HEAD_DIM
            kd_ref[r, 0:LANES] = jnp.where(low, kr, ks).astype(BF16)
            kd_ref[r, LANES:2 * LANES] = jnp.where(low, ks, kr).astype(BF16)
            vat_ref[:, r] = av.T.astype(BF16)
            u = _gelu(su)
            vn = (v * lax.rsqrt(v_ms + EPS) * ng_ref[...]).astype(BF16)
            for c in range(nrows // BLK):
                vc = vn[c * BLK:(c + 1) * BLK]
                stack = jnp.concatenate(
                    [jnp.where(lane_group == g, vc, jnp.zeros_like(vc)) for g in range(SGU_GROUPS)], axis=0)
                mixed = _dot(ws_ref[...], stack)
                ys_ref[r0 + c * BLK:r0 + (c + 1) * BLK, :] = (
                    u[c * BLK:(c + 1) * BLK] * (mixed + bs_ref[...])).astype(BF16)

        return tail

    nrows = x_ref.shape[0] // MIX_SPLIT
    tails = [projections(i * nrows, nrows) for i in range(MIX_SPLIT)]
    for tail in tails:
        tail()


def _mix_in(x, layer, g, w_in, w_sgu, w_gate, q_g, k_g, cos, sin, hmat, sgu_g, sgu_w, sgu_b, tm=MIX_TM):
    T = x.shape[0]
    w_main_spec = pl.BlockSpec((None, D_MODEL, MAIN_W), lambda i: (layer, 0, 0), pipeline_mode=pl.Buffered(1))
    row = lambda w: pl.BlockSpec((tm, w), lambda i: (i, 0))
    tcol = lambda r: pl.BlockSpec((r, tm), lambda i: (0, i))
    out_specs = [row(ATTN_Q_W), row(2 * ATTN_KV_W), tcol(ATTN_KV_W), row(2 * M_W), tcol(M_W), row(M_W),
                 tcol(N_GATES), row(SGU_W)]
    out_shape = [jax.ShapeDtypeStruct(s, d) for s, d in [
        ((T, ATTN_Q_W), BF16), ((T, 2 * ATTN_KV_W), BF16), ((ATTN_KV_W, T), BF16), ((T, 2 * M_W), BF16),
        ((M_W, T), BF16), ((T, M_W), BF16), ((N_GATES, T), F32), ((T, SGU_W), BF16)]]
    ws_cat = jnp.transpose(sgu_w, (1, 0, 2)).reshape(BLK, SGU_GROUPS * BLK).astype(BF16)
    bs_full = jnp.repeat(jnp.transpose(sgu_b), HEAD_DIM, axis=1)
    return pl.pallas_call(
        _mixin_kernel,
        grid=(T // tm,),
        in_specs=[row(D_MODEL), _const_spec((1, D_MODEL)), w_main_spec, _layer_spec(w_sgu.shape, layer),
                  _layer_spec(w_gate.shape, layer), _const_spec((1, ATTN_Q_W)), _const_spec((1, ATTN_KV_W)),
                  row(LANES), row(LANES), _const_spec(hmat.shape),
                  _const_spec((1, SGU_W)), _const_spec(ws_cat.shape), _const_spec(bs_full.shape)],
        out_specs=out_specs,
        out_shape=out_shape,
        compiler_params=_params(1),
        name="mix_in",
    )(x, g.reshape(1, D_MODEL), w_in, w_sgu, w_gate,
      jnp.tile(q_g, N_Q_HEADS).reshape(1, ATTN_Q_W), jnp.tile(k_g, N_KV_HEADS).reshape(1, ATTN_KV_W),
      cos, sin, hmat, sgu_g.reshape(1, SGU_W), ws_cat, bs_full)


def _attn_blocks(sink_ref, q_ref, kd_ref, vt_ref, o_ref):
    S = q_ref.shape[0]
    nb = S // BLK
    kc = lax.broadcasted_iota(jnp.int32, (BLK, BLK), 0)
    qi = lax.broadcasted_iota(jnp.int32, (BLK, BLK), 1)
    prev_bias = jnp.tile(jnp.where(kc >= qi, 0.0, NEG), (1, Q_PER_KV))
    next_bias = jnp.tile(jnp.where(kc <= qi, 0.0, NEG), (1, Q_PER_KV))
    half_mask = (qi < HEAD_DIM, qi >= HEAD_DIM)
    head_lane = lax.broadcasted_iota(jnp.int32, (1, Q_PER_KV * BLK), 1) >> 7
    sink_rows = []
    for kvh in range(N_KV_HEADS):
        row = jnp.zeros((1, Q_PER_KV * BLK), F32)
        for g in range(Q_PER_KV):
            row = jnp.where(head_lane == g, sink_ref[kvh * Q_PER_KV + g], row)
        sink_rows.append(row)

    def block(q0, start, biases):
        nk = len(biases)
        scores = []
        for kvh in range(N_KV_HEADS):
            kd = kd_ref[pl.ds(start, nk * BLK), kvh * LANES:(kvh + 1) * LANES]
            qs = []
            for g in range(Q_PER_KV):
                t = kvh * Q_PER_KV + g
                qt = q_ref[pl.ds(q0, BLK), (t // 2) * LANES:(t // 2 + 1) * LANES]
                qs.append(jnp.where(half_mask[t & 1], qt, jnp.zeros_like(qt)))
            scores.append(_dot_nt(kd, jnp.concatenate(qs, axis=0)))

        def weighted_values():
            ones = jnp.ones((HEAD_DIM, nk * BLK), BF16)
            outs = []
            for kvh in range(N_KV_HEADS):
                v_ext = jnp.concatenate([vt_ref[kvh * HEAD_DIM:(kvh + 1) * HEAD_DIM, pl.ds(start, nk * BLK)], ones],
                                        axis=0)
                parts = [scores[kvh][j * BLK:(j + 1) * BLK] for j in range(nk)]
                parts = [p if b is None else p + b for p, b in zip(parts, biases)]
                mx = jnp.max(functools.reduce(jnp.maximum, parts), axis=0, keepdims=True)
                m = jnp.maximum(mx, sink_rows[kvh])
                p = jnp.concatenate([jnp.exp(p - m) for p in parts], axis=0).astype(BF16)
                oe = _dot(v_ext, p)
                denom = oe[HEAD_DIM:HEAD_DIM + 1] + jnp.exp(sink_rows[kvh] - m)
                on = oe[0:HEAD_DIM] / denom
                for pair in range(Q_PER_KV // 2):
                    two = jnp.concatenate([on[:, (2 * pair) * BLK:(2 * pair + 1) * BLK],
                                           on[:, (2 * pair + 1) * BLK:(2 * pair + 2) * BLK]], axis=0)
                    outs.append(two.T)
            result = jnp.concatenate(outs, axis=1).astype(BF16)

            def commit():
                o_ref[pl.ds(q0, BLK), :] = result
            return commit

        return weighted_values

    def attend(n):
        if isinstance(n, int) and n == 0:
            return block(0, 0, [None, next_bias])
        if isinstance(n, int) and n == nb - 1:
            return block(n * BLK, (n - 1) * BLK, [prev_bias, None])
        return block(pl.multiple_of(n * BLK, BLK), pl.multiple_of((n - 1) * BLK, BLK), [prev_bias, None, next_bias])

    return attend


def _log_sigmoid(x):
    return jnp.minimum(x, 0.0) - jnp.log(1.0 + jnp.exp(-jnp.abs(x)))


def _mlstm_chunks(qk_ref, vt_ref, mo_ref, mgt_ref, gb_ref, hg_ref, o_ref,
                  gt_s, b_s, w_s, mloc_s, tot_s, rcol_s, cloc_s, cprev_s, mprev_s, keep_s, add_s):
    S = qk_ref.shape[0]
    L = BLK
    nc = S // L
    ng = N_GATES
    sub = lax.broadcasted_iota(jnp.int32, (L, L), 0)
    lane = lax.broadcasted_iota(jnp.int32, (L, L), 1)

    for c in range(nc):
        gt_s[c * ng:(c + 1) * ng, :] = mgt_ref[:, c * L:(c + 1) * L] + gb_ref[...]
    gt = gt_s[...]
    row = lax.broadcasted_iota(jnp.int32, gt.shape, 0)
    gt = jnp.where(((row >> 2) & 1) == 1, _log_sigmoid(gt), gt)
    upper = jnp.where(sub <= lane, 1.0, 0.0).astype(BF16)
    g1 = gt.astype(BF16)
    r1 = gt - g1.astype(F32)
    g2 = r1.astype(BF16)
    g3 = (r1 - g2.astype(F32)).astype(BF16)
    pre = _dot(g1, upper) + _dot(g2, upper) + _dot(g3, upper)
    tot = jnp.broadcast_to(pre[:, L - 1:L], pre.shape)
    bsum =jnp.where((row & (ng - 1)) < ng // 2, pre, tot - pre + gt)
    li = pltpu.roll(gt, M_HEADS, 0)
    a = tot - bsum + li
    mloc = jnp.broadcast_to(jnp.max(a, axis=1, keepdims=True), a.shape)
    b_s[...] = bsum
    w_s[...] = jnp.exp(a - mloc)
    mloc_s[...] = mloc
    tot_s[...] = tot
    rdiff = li - bsum
    zpad = jnp.zeros((L - ng, L), F32)
    for c in range(nc):
        rcol_s[c * L:(c + 1) * L, :] = jnp.concatenate([rdiff[c * ng:(c + 1) * ng, :], zpad], axis=0).T

    def value_rows(hd, r0, dtype):
        vt = vt_ref[hd * HEAD_DIM:(hd + 1) * HEAD_DIM, pl.ds(r0, L)].astype(dtype)
        return jnp.concatenate([vt, jnp.ones((HEAD_DIM, L), dtype)], axis=0)

    def local_state(c, carry):
        r0 = pl.multiple_of(c * L, L)
        g0 = pl.multiple_of(c * ng, ng)
        k = qk_ref[pl.ds(r0, L), M_W:2 * M_W]
        w = w_s[pl.ds(g0, ng), :]
        for hd in range(M_HEADS):
            vte = value_rows(hd, r0, F32)
            lhs = jnp.concatenate([vte * w[4 + hd:5 + hd, :], vte * w[12 + hd:13 + hd, :]], axis=0)
            cl = _dot(lhs.astype(BF16), k[:, (hd // 2) * LANES:(hd // 2 + 1) * LANES])
            cloc_s[0, hd, c] = cl[0:L]
            cloc_s[1, hd, c] = cl[L:2 * L]
        return carry

    lax.fori_loop(0, nc, local_state, 0, unroll=2)

    jrow = lax.broadcasted_iota(jnp.int32, (ng, L), 0)

    def scan_stabilisers(i, m):
        cf = i
        cb = nc - 1 - i
        gf = pl.multiple_of(cf * ng, ng)
        gb = pl.multiple_of(cb * ng, ng)
        is_fwd = jrow < ng // 2
        blast = jnp.where(is_fwd, tot_s[pl.ds(gf, ng), :], tot_s[pl.ds(gb, ng), :])
        mloc_i = jnp.where(is_fwd, mloc_s[pl.ds(gf, ng), :], mloc_s[pl.ds(gb, ng), :])
        mprev_s[0, pl.ds(gf, ng), :] = m
        mprev_s[1, pl.ds(gb, ng), :] = m
        m_new = jnp.maximum(blast + m, mloc_i)
        gi = pl.multiple_of(i * ng, ng)
        keep_s[pl.ds(gi, ng), :] = jnp.exp(blast + m - m_new)
        add_s[pl.ds(gi, ng), :] = jnp.exp(mloc_i - m_new)
        return m_new

    lax.fori_loop(0, nc, scan_stabilisers, jnp.zeros((ng, L), F32))

    for hd in range(M_HEADS):
        def scan_states(i, states, hd=hd):
            gi = pl.multiple_of(i * ng, ng)
            keep = keep_s[pl.ds(gi, ng), :]
            add = add_s[pl.ds(gi, ng), :]
            new_states = []
            for d, ci in enumerate((i, nc - 1 - i)):
                r = 4 + hd + 8 * d
                cprev_s[d, hd, ci] = states[d].astype(BF16)
                new_states.append(keep[r:r + 1, :] * states[d] + add[r:r + 1, :] * cloc_s[d, hd, ci])
            return tuple(new_states)

        lax.fori_loop(0, nc, scan_states, (jnp.zeros((L, LANES), F32), jnp.zeros((L, LANES), F32)))

    causal = (sub <= lane, sub >= lane)

    def outputs(c):
        r0 = pl.multiple_of(c * L, L)
        g0 = pl.multiple_of(c * ng, ng)
        q = qk_ref[pl.ds(r0, L), 0:M_W]
        k = qk_ref[pl.ds(r0, L), M_W:2 * M_W]
        bc = b_s[pl.ds(g0, ng), :]
        mprev = (mprev_s[0, pl.ds(g0, ng), :], mprev_s[1, pl.ds(g0, ng), :])
        rc = rcol_s[pl.ds(r0, L), :]
        ys = []
        for hd in range(M_HEADS):
            t = hd // 2
            qp = q[:, t * LANES:(t + 1) * LANES]
            qm = jnp.where((lane >> 6) == (hd & 1), qp, jnp.zeros_like(qp))
            x = jnp.concatenate([k[:, t * LANES:(t + 1) * LANES], cprev_s[0, hd, c], cprev_s[1, hd, c]], axis=0)
            ys.append(_dot_nt(x, qm))

        def combine():
            houts = []
            for hd in range(M_HEADS):
                y = ys[hd]
                st = y[0:L]
                pts, scs, mts = [], [], []
                for d in range(2):
                    r = 4 + hd + 8 * d
                    b_row = bc[r:r + 1, :]
                    dm = jnp.where(causal[d], rc[:, r:r + 1] + b_row, NEG)
                    inter = b_row + mprev[d][r:r + 1, :]
                    m_t = jnp.maximum(inter, jnp.max(dm, axis=0, keepdims=True))
                    pts.append((jnp.exp(dm - m_t) * st).astype(BF16))
                    scs.append(jnp.exp(inter - m_t))
                    mts.append(m_t)
                n2 = _dot(value_rows(hd, r0, BF16), jnp.concatenate(pts, axis=1))
                ht = None
                for d in range(2):
                    tt = n2[:, d * L:(d + 1) * L] + scs[d] * y[(d + 1) * L:(d + 2) * L]
                    den = tt[HEAD_DIM:HEAD_DIM + 1, :]
                    hd_out = tt[0:HEAD_DIM] / jnp.maximum(jnp.abs(den), jnp.exp(-mts[d]))
                    ht = hd_out if ht is None else ht + hd_out
                ms = jnp.mean(ht * ht, axis=0, keepdims=True)
                houts.append(ht * lax.rsqrt(ms + EPS))
            hn = (jnp.concatenate(houts, axis=0) * hg_ref[...]).T
            result = (jax.nn.sigmoid(mo_ref[pl.ds(r0, L), :].astype(F32)) * hn).astype(BF16)

            def commit():
                o_ref[pl.ds(r0, L), :] = result
            return commit

        return combine

    return outputs


def _mixers_kernel(sink_ref, q_ref, kd_ref, vat_ref, qk_ref, mvt_ref, mo_ref, mgt_ref, gb_ref, hg_ref,
                   ya_ref, ym_ref, *scratch):
    nb = q_ref.shape[0] // BLK
    attend = _attn_blocks(sink_ref, q_ref, kd_ref, vat_ref, ya_ref)
    outputs = _mlstm_chunks(qk_ref, mvt_ref, mo_ref, mgt_ref, gb_ref, hg_ref, ym_ref, *scratch)

    def run(blocks):
        second_stages = [f(n) for n in blocks for f in (attend, outputs)]
        commits = [stage() for stage in second_stages]
        for commit in commits:
            commit()

    def pair(i, carry):
        run((1 + 2 * i, 2 + 2 * i))
        return carry

    run((0, nb - 1))
    lax.fori_loop(0, (nb - 2) // 2, pair, 0)


def _mixers(qa, kd, vat, sink, mqk, mvt, mo, mgt, gate_b, head_g, B, S):
    seq = lambda w: pl.BlockSpec((S, w), lambda b: (b, 0))
    tseq = lambda r: pl.BlockSpec((r, S), lambda b: (0, b))
    nc = S // BLK
    assert nc % 2 == 0 and nc >= 4
    gb = jnp.broadcast_to(gate_b.reshape(N_GATES, 1), (N_GATES, BLK))
    hg_t = jnp.broadcast_to(head_g.reshape(M_W, 1), (M_W, BLK))
    rows = nc * N_GATES
    return pl.pallas_call(
        _mixers_kernel,
        grid=(B,),
        in_specs=[pl.BlockSpec(memory_space=pltpu.SMEM), seq(ATTN_Q_W), seq(2 * ATTN_KV_W), tseq(ATTN_KV_W),
                  seq(2 * M_W), tseq(M_W), seq(M_W), tseq(N_GATES), _const_spec((N_GATES, BLK)),
                  _const_spec((M_W, BLK))],
        out_specs=[seq(ATTN_Q_W), seq(M_W)],
        out_shape=[jax.ShapeDtypeStruct((B * S, ATTN_Q_W), BF16), jax.ShapeDtypeStruct((B * S, M_W), BF16)],
        scratch_shapes=[pltpu.VMEM((rows, BLK), F32),
                        pltpu.VMEM((rows, BLK), F32),
                        pltpu.VMEM((rows, BLK), F32),
                        pltpu.VMEM((rows, BLK), F32),
                        pltpu.VMEM((rows, BLK), F32),
                        pltpu.VMEM((S, LANES), F32),
                        pltpu.VMEM((2, M_HEADS, nc, BLK, LANES), F32),
                        pltpu.VMEM((2, M_HEADS, nc, BLK, LANES), BF16),
                        pltpu.VMEM((2, rows, BLK), F32),
                        pltpu.VMEM((rows, BLK), F32),
                        pltpu.VMEM((rows, BLK), F32)],
        compiler_params=_params(1),
        name="mixers",
    )(sink, qa, kd, vat, mqk, mvt, mo, mgt, gb, hg_t)


def kernel(x, positions, norm_ffn1_g, ffn1_w_gate, ffn1_w_up, ffn1_w_down, norm_mix_g, w_in, q_norm_g, k_norm_g, attn_sink, mlstm_gate_b, mlstm_head_g, sgu_norm_g, sgu_w_s, sgu_b, w_out, norm_ffn2_g, ffn2_w_gate, ffn2_w_up, ffn2_w_down, norm_out_g):
    B, S, D = x.shape
    depth = w_in.shape[0]
    T = B * S
    xt = x.reshape(T, D)
    cos, sin = _rope_tables(positions)
    group = np.arange(M_W) // HEAD_DIM
    hmat = jnp.asarray((group[:, None] == group[None, :]) / HEAD_DIM, BF16)
    w1g, w1u, w1d = ffn1_w_gate.astype(BF16), ffn1_w_up.astype(BF16), ffn1_w_down.astype(BF16)
    w2g, w2u, w2d = ffn2_w_gate.astype(BF16), ffn2_w_up.astype(BF16), ffn2_w_down.astype(BF16)
    w_in_b = w_in.astype(BF16)
    w_gate = jnp.pad(w_in_b[..., MAIN_W:MAIN_W + N_GATES], ((0, 0), (0, 0), (0, GATE_PAD - N_GATES)))
    w_sgu = w_in_b[..., MAIN_W + N_GATES:]
    w_o = w_out.astype(BF16)
    for l in range(depth):
        xt = _ffn(xt, l, norm_ffn1_g[l], w1g, w1u, w1d)
        qa, kd, vat, mqk, mvt, mo, mgt, ys = _mix_in(xt, l, norm_mix_g[l], w_in_b, w_sgu, w_gate, q_norm_g[l],
                                                     k_norm_g[l], cos, sin, hmat, sgu_norm_g[l], sgu_w_s[l], sgu_b[l])
        ya, ym = _mixers(qa, kd, vat, attn_sink[l], mqk, mvt, mo, mgt, mlstm_gate_b[l], mlstm_head_g[l], B, S)
        xt = _ffn(xt, l, norm_ffn2_g[l], w2g, w2u, w2d, proj=(ya, ym, ys, w_o), gout=norm_out_g[l])
    return xt.reshape(B, S, D)
```

```python
import functools

import numpy as np
import jax
import jax.numpy as jnp
from jax import lax
from jax.experimental import pallas as pl
from jax.experimental.pallas import tpu as pltpu

F32 = jnp.float32
BF16 = jnp.bfloat16

D_MODEL = 1024
HEAD_DIM = 64
N_Q_HEADS = 8
N_KV_HEADS = 2
Q_PER_KV = N_Q_HEADS // N_KV_HEADS
ATTN_Q_W = N_Q_HEADS * HEAD_DIM
ATTN_KV_W = N_KV_HEADS * HEAD_DIM
WINDOW = 128
BLK = 128
ROPE_THETA = 10000.0
M_HEADS = 4
M_W = M_HEADS * HEAD_DIM
N_GATES = 4 * M_HEADS
SGU_GROUPS = 4
SGU_W = SGU_GROUPS * HEAD_DIM
D_FF = 2816
EPS = 1e-6
LANES = 128
GATE_PAD = LANES
MAIN_W = ATTN_Q_W + 2 * ATTN_KV_W + 4 * M_W
NEG = -1e30

VMEM_LIMIT = 56 * 1024 * 1024
FFN_TM = 512
FFN_SPLIT = 2
W_CHUNKS = 8
MIX_TM = 512
MIX_SPLIT = 1


def _dot(a, b):
    return jnp.dot(a, b, preferred_element_type=F32)


def _dot_nt(a, b):
    return lax.dot_general(a, b, (((1,), (1,)), ((), ())), preferred_element_type=F32)


def _dot_tn(a, b):
    return lax.dot_general(a, b, (((0,), (0,)), ((), ())), preferred_element_type=F32)


def _rms(x, g):
    ms = jnp.mean(x * x, axis=-1, keepdims=True)
    return x * lax.rsqrt(ms + EPS) * g


def _group_mean_sq(t, hmat):
    return _dot((t * t).astype(BF16), hmat)


def _const_spec(shape):
    nd = len(shape)
    return pl.BlockSpec(shape, lambda *_: (0,) * nd, pipeline_mode=pl.Buffered(1))


def _layer_spec(stacked_shape, layer):
    nd = len(stacked_shape) - 1
    return pl.BlockSpec((None,) + tuple(stacked_shape[1:]), lambda *_: (layer,) + (0,) * nd,
                        pipeline_mode=pl.Buffered(1))


def _params(n_grid):
    return pltpu.CompilerParams(dimension_semantics=("parallel",) * n_grid,
                                vmem_limit_bytes=VMEM_LIMIT)


ROPE_HALF = HEAD_DIM // 2
ROPE_PACK = LANES // ROPE_HALF
ROPE_STEPS = 8


def _rope_kernel(pos_ref, freq_ref, cos_ref, sin_ref):
    ang = pos_ref[...] * freq_ref[...]
    rows = ang.shape[0]
    token = lax.broadcasted_iota(jnp.int32, ang.shape, 1) >> 5
    for table, out_ref in ((jnp.cos(ang), cos_ref), (jnp.sin(ang), sin_ref)):
        for i in range(ROPE_PACK):
            x = jnp.where(token == i, table, 0.0)
            y = x + pltpu.roll(x, 2 * ROPE_HALF, 1)
            out_ref[pl.ds(i, rows, stride=ROPE_PACK), :] = y + pltpu.roll(y, ROPE_HALF, 1)


def _rope_tables(positions):
    T = positions.size
    freqs = ROPE_THETA ** (-jnp.arange(0, HEAD_DIM, 2, dtype=F32) / HEAD_DIM)
    pos = jnp.repeat(positions.reshape(T // ROPE_PACK, ROPE_PACK).astype(F32), ROPE_HALF, axis=1)
    freq_row = jnp.tile(freqs, ROPE_PACK).reshape(1, LANES)
    rows = T // ROPE_PACK // ROPE_STEPS
    return pl.pallas_call(
        _rope_kernel,
        grid=(ROPE_STEPS,),
        in_specs=[pl.BlockSpec((rows, LANES), lambda i: (i, 0)),
                  pl.BlockSpec((1, LANES), lambda i: (0, 0))],
        out_specs=[pl.BlockSpec((rows * ROPE_PACK, LANES), lambda i: (i, 0))] * 2,
        out_shape=[jax.ShapeDtypeStruct((T, LANES), F32)] * 2,
        compiler_params=_params(1),
        name="rope_tables",
    )(pos, freq_row)


def _load_weight_bf16(w_hbm, layer, dst, stage, sems):
    rows = stage.shape[1]
    n_chunks = dst.shape[0] // rows

    def copy(c):
        return pltpu.make_async_copy(w_hbm.at[layer, pl.ds(c * rows, rows), :], stage.at[c % 2], sems.at[c % 2])

    copy(0).start()
    for c in range(n_chunks):
        if c + 1 < n_chunks:
            copy(c + 1).start()
        copy(c).wait()
        dst[c * rows:(c + 1) * rows, :] = stage[c % 2].astype(BF16)


def _ffn_kernel(*refs, layer, has_proj, has_final):
    refs = list(refs)
    x_ref = refs.pop(0)
    if has_proj:
        ya_ref, ym_ref, ys_ref, wo_ref = refs[:4]
        refs = refs[4:]
    g_ref, wg_hbm, wu_hbm, wd_hbm = refs[:4]
    refs = refs[4:]
    if has_final:
        gout_ref = refs.pop(0)
    out_ref, wg_ref, wu_ref, wd_ref, stage_in, stage_out, sems = refs

    @pl.when(pl.program_id(0) == 0)
    def _():
        _load_weight_bf16(wg_hbm, layer, wg_ref, stage_in, sems)
        _load_weight_bf16(wu_hbm, layer, wu_ref, stage_in, sems)
        _load_weight_bf16(wd_hbm, layer, wd_ref, stage_out, sems)

    hs = x_ref.shape[0] // FFN_SPLIT
    rows = [slice(i * hs, (i + 1) * hs) for i in range(FFN_SPLIT)]
    xs = []
    for r in rows:
        x = x_ref[r, :]
        if has_proj:
            x = (x + _dot(ya_ref[r, :], wo_ref[0:ATTN_Q_W, :])
                 + _dot(ym_ref[r, :], wo_ref[ATTN_Q_W:ATTN_Q_W + M_W, :])
                 + _dot(ys_ref[r, :], wo_ref[ATTN_Q_W + M_W:, :]))
        xs.append(x)
    hidden = [_rms(x, g_ref[...]).astype(BF16) for x in xs]
    gate_up = [(_dot(h, wg_ref[...]), _dot(h, wu_ref[...])) for h in hidden]
    for r, x, (gate, up) in zip(rows, xs, gate_up):
        act = (gate * jax.nn.sigmoid(gate) * up).astype(BF16)
        y = x + 0.5 * _dot(act, wd_ref[...])
        if has_final:
            y = _rms(y, gout_ref[...])
        out_ref[r, :] = y


def _ffn(x, layer, g, wg, wu, wd, proj=None, gout=None, tm=FFN_TM):
    T = x.shape[0]
    d_model, d_ff = wg.shape[1:]
    row = lambda w: pl.BlockSpec((tm, w), lambda i: (i, 0))
    hbm = pl.BlockSpec(memory_space=pl.ANY)
    args, specs = [x], [row(D_MODEL)]
    if proj is not None:
        ya, ym, ys, wo = proj
        args += [ya, ym, ys, wo]
        specs += [row(ATTN_Q_W), row(M_W), row(SGU_W), _layer_spec(wo.shape, layer)]
    args += [g.reshape(1, D_MODEL), wg, wu, wd]
    specs += [_const_spec((1, D_MODEL)), hbm, hbm, hbm]
    if gout is not None:
        args.append(gout.reshape(1, D_MODEL))
        specs.append(_const_spec((1, D_MODEL)))
    return pl.pallas_call(
        functools.partial(_ffn_kernel, layer=layer, has_proj=proj is not None, has_final=gout is not None),
        grid=(T // tm,),
        in_specs=specs,
        out_specs=row(D_MODEL),
        out_shape=jax.ShapeDtypeStruct((T, D_MODEL), F32),
        scratch_shapes=[pltpu.VMEM((d_model, d_ff), BF16), pltpu.VMEM((d_model, d_ff), BF16),
                        pltpu.VMEM((d_ff, d_model), BF16),
                        pltpu.VMEM((2, d_model // W_CHUNKS, d_ff), F32),
                        pltpu.VMEM((2, d_ff // W_CHUNKS, d_model), F32),
                        pltpu.SemaphoreType.DMA((2,))],
        compiler_params=pltpu.CompilerParams(dimension_semantics=("arbitrary",), vmem_limit_bytes=VMEM_LIMIT),
        name="ffn",
    )(*args)


def _first_half(shape):
    lane = lax.broadcasted_iota(jnp.int32, shape, 1)
    return (lane & (HEAD_DIM - 1)) < HEAD_DIM // 2


def _rope(t, cos, sin_signed):
    n = t.shape[-1]
    reps = n // LANES
    swapped = jnp.where(_first_half(t.shape),
                        pltpu.roll(t, n - HEAD_DIM // 2, 1), pltpu.roll(t, HEAD_DIM // 2, 1))
    return t * jnp.tile(cos, (1, reps)) + swapped * jnp.tile(sin_signed, (1, reps))


def _gelu(x):
    return 0.5 * x * (1.0 + lax.erf(x * np.float32(np.sqrt(0.5))))


def _mixin_kernel(x_ref, g_ref, w_ref, wsg_ref, wgt_ref, qg_ref, kg_ref, cos_ref, sin_ref, hmat_ref, ng_ref, ws_ref,
                  bs_ref, qa_ref, kd_ref, vat_ref, mqk_ref, mvt_ref, mo_ref, mgt_ref, ys_ref):
    scale = HEAD_DIM ** -0.5
    hmat = hmat_ref[...]
    gw = hmat.shape[0]
    c_q, c_kv, c_mq, c_mk, c_mv, c_mo = (int(c) for c in np.cumsum([0, ATTN_Q_W, 2 * ATTN_KV_W, M_W, M_W, M_W]))
    lane_group = lax.broadcasted_iota(jnp.int32, (BLK, SGU_W), 1) >> 6

    def projections(r0, nrows):
        r = slice(r0, r0 + nrows)
        h = _rms(x_ref[r, :], g_ref[...]).astype(BF16)
        proj = lambda ref, start, width: _dot(h, ref[:, start:start + width])
        su = proj(wsg_ref, 0, SGU_W)
        sv = proj(wsg_ref, SGU_W, SGU_W)
        aq = proj(w_ref, c_q, ATTN_Q_W)
        akv = proj(w_ref, c_kv, 2 * ATTN_KV_W)
        ak, av = akv[:, :ATTN_KV_W], akv[:, ATTN_KV_W:]
        v = _gelu(sv)
        v_ms = _group_mean_sq(v, hmat)
        mqk_ref[r, 0:M_W] = proj(w_ref, c_mq, M_W).astype(BF16)
        mqk_ref[r, M_W:2 * M_W] = (proj(w_ref, c_mk, M_W) * scale).astype(BF16)
        q_ms = jnp.concatenate([_group_mean_sq(aq[:, i:i + gw], hmat) for i in range(0, ATTN_Q_W, gw)], axis=1)
        k_ms = _group_mean_sq(ak, hmat[0:ATTN_KV_W, 0:ATTN_KV_W])
        mvt_ref[:, r] = proj(w_ref, c_mv, M_W).T.astype(BF16)
        mo_ref[r, :] = proj(w_ref, c_mo, M_W).astype(BF16)
        mgt_ref[:, r] = proj(wgt_ref, 0, GATE_PAD).T[0:N_GATES, :]

        def tail():
            cos = cos_ref[r, :]
            sin_signed = jnp.where(_first_half(cos.shape), -sin_ref[r, :], sin_ref[r, :])
            qn = aq * lax.rsqrt(q_ms + EPS) * qg_ref[...]
            qa_ref[r, :] = (_rope(qn, cos, sin_signed) * scale).astype(BF16)
            kn = ak * lax.rsqrt(k_ms + EPS) * kg_ref[...]
            kr = _rope(kn, cos, sin_signed)
            ks = pltpu.roll(kr, HEAD_DIM, 1)
            low = lax.broadcasted_iota(jnp.int32, kr.shape, 1) < HEAD_DIM
            kd_ref[r, 0:LANES] = jnp.where(low, kr, ks).astype(BF16)
            kd_ref[r, LANES:2 * LANES] = jnp.where(low, ks, kr).astype(BF16)
            vat_ref[:, r] = av.T.astype(BF16)
            u = _gelu(su)
            vn = (v * lax.rsqrt(v_ms + EPS) * ng_ref[...]).astype(BF16)
            for c in range(nrows // BLK):
                vc = vn[c * BLK:(c + 1) * BLK]
                stack = jnp.concatenate(
                    [jnp.where(lane_group == g, vc, jnp.zeros_like(vc)) for g in range(SGU_GROUPS)], axis=0)
                mixed = _dot(ws_ref[...], stack)
                ys_ref[r0 + c * BLK:r0 + (c + 1) * BLK, :] = (
                    u[c * BLK:(c + 1) * BLK] * (mixed + bs_ref[...])).astype(BF16)

        return tail

    nrows = x_ref.shape[0] // MIX_SPLIT
    tails = [projections(i * nrows, nrows) for i in range(MIX_SPLIT)]
    for tail in tails:
        tail()


def _mix_in(x, layer, g, w_main, w_sgu, w_gate, q_g, k_g, cos, sin, hmat, sgu_g, sgu_w, sgu_b, tm=MIX_TM):
    T = x.shape[0]
    row = lambda w: pl.BlockSpec((tm, w), lambda i: (i, 0))
    tcol = lambda r: pl.BlockSpec((r, tm), lambda i: (0, i))
    out_specs = [row(ATTN_Q_W), row(2 * ATTN_KV_W), tcol(ATTN_KV_W), row(2 * M_W), tcol(M_W), row(M_W),
                 tcol(N_GATES), row(SGU_W)]
    out_shape = [jax.ShapeDtypeStruct(s, d) for s, d in [
        ((T, ATTN_Q_W), BF16), ((T, 2 * ATTN_KV_W), BF16), ((ATTN_KV_W, T), BF16), ((T, 2 * M_W), BF16),
        ((M_W, T), BF16), ((T, M_W), BF16), ((N_GATES, T), F32), ((T, SGU_W), BF16)]]
    ws_cat = jnp.transpose(sgu_w, (1, 0, 2)).reshape(BLK, SGU_GROUPS * BLK).astype(BF16)
    bs_full = jnp.repeat(jnp.transpose(sgu_b), HEAD_DIM, axis=1)
    return pl.pallas_call(
        _mixin_kernel,
        grid=(T // tm,),
        in_specs=[row(D_MODEL), _const_spec((1, D_MODEL)), _layer_spec(w_main.shape, layer), _layer_spec(w_sgu.shape, layer),
                  _layer_spec(w_gate.shape, layer), _const_spec((1, ATTN_Q_W)), _const_spec((1, ATTN_KV_W)),
                  row(LANES), row(LANES), _const_spec(hmat.shape),
                  _const_spec((1, SGU_W)), _const_spec(ws_cat.shape), _const_spec(bs_full.shape)],
        out_specs=out_specs,
        out_shape=out_shape,
        compiler_params=_params(1),
        name="mix_in",
    )(x, g.reshape(1, D_MODEL), w_main, w_sgu, w_gate,
      jnp.tile(q_g, N_Q_HEADS).reshape(1, ATTN_Q_W), jnp.tile(k_g, N_KV_HEADS).reshape(1, ATTN_KV_W),
      cos, sin, hmat, sgu_g.reshape(1, SGU_W), ws_cat, bs_full)


def _attn_blocks(sink_ref, q_ref, kd_ref, vt_ref, o_ref):
    S = q_ref.shape[0]
    nb = S // BLK
    kc = lax.broadcasted_iota(jnp.int32, (BLK, BLK), 0)
    qi = lax.broadcasted_iota(jnp.int32, (BLK, BLK), 1)
    prev_bias = jnp.tile(jnp.where(kc >= qi, 0.0, NEG), (1, Q_PER_KV))
    next_bias = jnp.tile(jnp.where(kc <= qi, 0.0, NEG), (1, Q_PER_KV))
    half_mask = (qi < HEAD_DIM, qi >= HEAD_DIM)
    head_lane = lax.broadcasted_iota(jnp.int32, (1, Q_PER_KV * BLK), 1) >> 7
    sink_rows = []
    for kvh in range(N_KV_HEADS):
        row = jnp.zeros((1, Q_PER_KV * BLK), F32)
        for g in range(Q_PER_KV):
            row = jnp.where(head_lane == g, sink_ref[kvh * Q_PER_KV + g], row)
        sink_rows.append(row)

    def block(q0, start, biases):
        nk = len(biases)
        scores = []
        for kvh in range(N_KV_HEADS):
            kd = kd_ref[pl.ds(start, nk * BLK), kvh * LANES:(kvh + 1) * LANES]
            qs = []
            for g in range(Q_PER_KV):
                t = kvh * Q_PER_KV + g
                qt = q_ref[pl.ds(q0, BLK), (t // 2) * LANES:(t // 2 + 1) * LANES]
                qs.append(jnp.where(half_mask[t & 1], qt, jnp.zeros_like(qt)))
            scores.append(_dot_nt(kd, jnp.concatenate(qs, axis=0)))

        def weighted_values():
            ones = jnp.ones((HEAD_DIM, nk * BLK), BF16)
            outs = []
            for kvh in range(N_KV_HEADS):
                v_ext = jnp.concatenate([vt_ref[kvh * HEAD_DIM:(kvh + 1) * HEAD_DIM, pl.ds(start, nk * BLK)], ones],
                                        axis=0)
                parts = [scores[kvh][j * BLK:(j + 1) * BLK] for j in range(nk)]
                parts = [p if b is None else p + b for p, b in zip(parts, biases)]
                mx = jnp.max(functools.reduce(jnp.maximum, parts), axis=0, keepdims=True)
                m = jnp.maximum(mx, sink_rows[kvh])
                p = jnp.concatenate([jnp.exp(p - m) for p in parts], axis=0).astype(BF16)
                oe = _dot(v_ext, p)
                denom = oe[HEAD_DIM:HEAD_DIM + 1] + jnp.exp(sink_rows[kvh] - m)
                on = oe[0:HEAD_DIM] / denom
                for pair in range(Q_PER_KV // 2):
                    two = jnp.concatenate([on[:, (2 * pair) * BLK:(2 * pair + 1) * BLK],
                                           on[:, (2 * pair + 1) * BLK:(2 * pair + 2) * BLK]], axis=0)
                    outs.append(two.T)
            result = jnp.concatenate(outs, axis=1).astype(BF16)

            def commit():
                o_ref[pl.ds(q0, BLK), :] = result
            return commit

        return weighted_values

    def attend(n):
        if isinstance(n, int) and n == 0:
            return block(0, 0, [None, next_bias])
        if isinstance(n, int) and n == nb - 1:
            return block(n * BLK, (n - 1) * BLK, [prev_bias, None])
        return block(pl.multiple_of(n * BLK, BLK), pl.multiple_of((n - 1) * BLK, BLK), [prev_bias, None, next_bias])

    return attend


def _log_sigmoid(x):
    return jnp.minimum(x, 0.0) - jnp.log(1.0 + jnp.exp(-jnp.abs(x)))


def _mlstm_chunks(qk_ref, vt_ref, mo_ref, mgt_ref, gb_ref, hg_ref, o_ref,
                  gt_s, b_s, w_s, mloc_s, tot_s, rcol_s, cloc_s, cprev_s, mprev_s, keep_s, add_s):
    S = qk_ref.shape[0]
    L = BLK
    nc = S // L
    ng = N_GATES
    sub = lax.broadcasted_iota(jnp.int32, (L, L), 0)
    lane = lax.broadcasted_iota(jnp.int32, (L, L), 1)

    for c in range(nc):
        gt_s[c * ng:(c + 1) * ng, :] = mgt_ref[:, c * L:(c + 1) * L] + gb_ref[...]
    gt = gt_s[...]
    row = lax.broadcasted_iota(jnp.int32, gt.shape, 0)
    gt = jnp.where(((row >> 2) & 1) == 1, _log_sigmoid(gt), gt)
    upper = jnp.where(sub <= lane, 1.0, 0.0).astype(BF16)
    g1 = gt.astype(BF16)
    r1 = gt - g1.astype(F32)
    g2 = r1.astype(BF16)
    g3 = (r1 - g2.astype(F32)).astype(BF16)
    pre = _dot(g1, upper) + _dot(g2, upper) + _dot(g3, upper)
    tot = jnp.broadcast_to(pre[:, L - 1:L], pre.shape)
    bsum =jnp.where((row & (ng - 1)) < ng // 2, pre, tot - pre + gt)
    li = pltpu.roll(gt, M_HEADS, 0)
    a = tot - bsum + li
    mloc = jnp.broadcast_to(jnp.max(a, axis=1, keepdims=True), a.shape)
    b_s[...] = bsum
    w_s[...] = jnp.exp(a - mloc)
    mloc_s[...] = mloc
    tot_s[...] = tot
    rdiff = li - bsum
    zpad = jnp.zeros((L - ng, L), F32)
    for c in range(nc):
        rcol_s[c * L:(c + 1) * L, :] = jnp.concatenate([rdiff[c * ng:(c + 1) * ng, :], zpad], axis=0).T

    def value_rows(hd, r0, dtype):
        vt = vt_ref[hd * HEAD_DIM:(hd + 1) * HEAD_DIM, pl.ds(r0, L)].astype(dtype)
        return jnp.concatenate([vt, jnp.ones((HEAD_DIM, L), dtype)], axis=0)

    def local_state(c, carry):
        r0 = pl.multiple_of(c * L, L)
        g0 = pl.multiple_of(c * ng, ng)
        k = qk_ref[pl.ds(r0, L), M_W:2 * M_W]
        w = w_s[pl.ds(g0, ng), :]
        for hd in range(M_HEADS):
            vte = value_rows(hd, r0, F32)
            lhs = jnp.concatenate([vte * w[4 + hd:5 + hd, :], vte * w[12 + hd:13 + hd, :]], axis=0)
            cl = _dot(lhs.astype(BF16), k[:, (hd // 2) * LANES:(hd // 2 + 1) * LANES])
            cloc_s[0, hd, c] = cl[0:L]
            cloc_s[1, hd, c] = cl[L:2 * L]
        return carry

    lax.fori_loop(0, nc, local_state, 0, unroll=2)

    jrow = lax.broadcasted_iota(jnp.int32, (ng, L), 0)

    def scan_stabilisers(i, m):
        cf = i
        cb = nc - 1 - i
        gf = pl.multiple_of(cf * ng, ng)
        gb = pl.multiple_of(cb * ng, ng)
        is_fwd = jrow < ng // 2
        blast = jnp.where(is_fwd, tot_s[pl.ds(gf, ng), :], tot_s[pl.ds(gb, ng), :])
        mloc_i = jnp.where(is_fwd, mloc_s[pl.ds(gf, ng), :], mloc_s[pl.ds(gb, ng), :])
        mprev_s[0, pl.ds(gf, ng), :] = m
        mprev_s[1, pl.ds(gb, ng), :] = m
        m_new = jnp.maximum(blast + m, mloc_i)
        gi = pl.multiple_of(i * ng, ng)
        keep_s[pl.ds(gi, ng), :] = jnp.exp(blast + m - m_new)
        add_s[pl.ds(gi, ng), :] = jnp.exp(mloc_i - m_new)
        return m_new

    lax.fori_loop(0, nc, scan_stabilisers, jnp.zeros((ng, L), F32))

    for hd in range(M_HEADS):
        def scan_states(i, states, hd=hd):
            gi = pl.multiple_of(i * ng, ng)
            keep = keep_s[pl.ds(gi, ng), :]
            add = add_s[pl.ds(gi, ng), :]
            new_states = []
            for d, ci in enumerate((i, nc - 1 - i)):
                r = 4 + hd + 8 * d
                cprev_s[d, hd, ci] = states[d].astype(BF16)
                new_states.append(keep[r:r + 1, :] * states[d] + add[r:r + 1, :] * cloc_s[d, hd, ci])
            return tuple(new_states)

        lax.fori_loop(0, nc, scan_states, (jnp.zeros((L, LANES), F32), jnp.zeros((L, LANES), F32)))

    causal = (sub <= lane, sub >= lane)

    def outputs(c):
        r0 = pl.multiple_of(c * L, L)
        g0 = pl.multiple_of(c * ng, ng)
        q = qk_ref[pl.ds(r0, L), 0:M_W]
        k = qk_ref[pl.ds(r0, L), M_W:2 * M_W]
        bc = b_s[pl.ds(g0, ng), :]
        mprev = (mprev_s[0, pl.ds(g0, ng), :], mprev_s[1, pl.ds(g0, ng), :])
        rc = rcol_s[pl.ds(r0, L), :]
        ys = []
        for hd in range(M_HEADS):
            t = hd // 2
            qp = q[:, t * LANES:(t + 1) * LANES]
            qm = jnp.where((lane >> 6) == (hd & 1), qp, jnp.zeros_like(qp))
            x = jnp.concatenate([k[:, t * LANES:(t + 1) * LANES], cprev_s[0, hd, c], cprev_s[1, hd, c]], axis=0)
            ys.append(_dot_nt(x, qm))

        def combine():
            houts = []
            for hd in range(M_HEADS):
                y = ys[hd]
                st = y[0:L]
                pts, scs, mts = [], [], []
                for d in range(2):
                    r = 4 + hd + 8 * d
                    b_row = bc[r:r + 1, :]
                    dm = jnp.where(causal[d], rc[:, r:r + 1] + b_row, NEG)
                    inter = b_row + mprev[d][r:r + 1, :]
                    m_t = jnp.maximum(inter, jnp.max(dm, axis=0, keepdims=True))
                    pts.append((jnp.exp(dm - m_t) * st).astype(BF16))
                    scs.append(jnp.exp(inter - m_t))
                    mts.append(m_t)
                n2 = _dot(value_rows(hd, r0, BF16), jnp.concatenate(pts, axis=1))
                ht = None
                for d in range(2):
                    tt = n2[:, d * L:(d + 1) * L] + scs[d] * y[(d + 1) * L:(d + 2) * L]
                    den = tt[HEAD_DIM:HEAD_DIM + 1, :]
                    hd_out = tt[0:HEAD_DIM] / jnp.maximum(jnp.abs(den), jnp.exp(-mts[d]))
                    ht = hd_out if ht is None else ht + hd_out
                ms = jnp.mean(ht * ht, axis=0, keepdims=True)
                houts.append(ht * lax.rsqrt(ms + EPS))
            hn = (jnp.concatenate(houts, axis=0) * hg_ref[...]).T
            result = (jax.nn.sigmoid(mo_ref[pl.ds(r0, L), :].astype(F32)) * hn).astype(BF16)

            def commit():
                o_ref[pl.ds(r0, L), :] = result
            return commit

        return combine

    return outputs


def _mixers_kernel(sink_ref, q_ref, kd_ref, vat_ref, qk_ref, mvt_ref, mo_ref, mgt_ref, gb_ref, hg_ref,
                   ya_ref, ym_ref, *scratch):
    nb = q_ref.shape[0] // BLK
    attend = _attn_blocks(sink_ref, q_ref, kd_ref, vat_ref, ya_ref)
    outputs = _mlstm_chunks(qk_ref, mvt_ref, mo_ref, mgt_ref, gb_ref, hg_ref, ym_ref, *scratch)

    def run(blocks):
        second_stages = [f(n) for n in blocks for f in (attend, outputs)]
        commits = [stage() for stage in second_stages]
        for commit in commits:
            commit()

    def pair(i, carry):
        run((1 + 2 * i, 2 + 2 * i))
        return carry

    run((0, nb - 1))
    lax.fori_loop(0, (nb - 2) // 2, pair, 0)


def _mixers(qa, kd, vat, sink, mqk, mvt, mo, mgt, gate_b, head_g, B, S):
    seq = lambda w: pl.BlockSpec((S, w), lambda b: (b, 0))
    tseq = lambda r: pl.BlockSpec((r, S), lambda b: (0, b))
    nc = S // BLK
    assert nc % 2 == 0 and nc >= 4
    gb = jnp.broadcast_to(gate_b.reshape(N_GATES, 1), (N_GATES, BLK))
    hg_t = jnp.broadcast_to(head_g.reshape(M_W, 1), (M_W, BLK))
    rows = nc * N_GATES
    return pl.pallas_call(
        _mixers_kernel,
        grid=(B,),
        in_specs=[pl.BlockSpec(memory_space=pltpu.SMEM), seq(ATTN_Q_W), seq(2 * ATTN_KV_W), tseq(ATTN_KV_W),
                  seq(2 * M_W), tseq(M_W), seq(M_W), tseq(N_GATES), _const_spec((N_GATES, BLK)),
                  _const_spec((M_W, BLK))],
        out_specs=[seq(ATTN_Q_W), seq(M_W)],
        out_shape=[jax.ShapeDtypeStruct((B * S, ATTN_Q_W), BF16), jax.ShapeDtypeStruct((B * S, M_W), BF16)],
        scratch_shapes=[pltpu.VMEM((rows, BLK), F32),
                        pltpu.VMEM((rows, BLK), F32),
                        pltpu.VMEM((rows, BLK), F32),
                        pltpu.VMEM((rows, BLK), F32),
                        pltpu.VMEM((rows, BLK), F32),
                        pltpu.VMEM((S, LANES), F32),
                        pltpu.VMEM((2, M_HEADS, nc, BLK, LANES), F32),
                        pltpu.VMEM((2, M_HEADS, nc, BLK, LANES), BF16),
                        pltpu.VMEM((2, rows, BLK), F32),
                        pltpu.VMEM((rows, BLK), F32),
                        pltpu.VMEM((rows, BLK), F32)],
        compiler_params=_params(1),
        name="mixers",
    )(sink, qa, kd, vat, mqk, mvt, mo, mgt, gb, hg_t)


def kernel(x, positions, norm_ffn1_g, ffn1_w_gate, ffn1_w_up, ffn1_w_down, norm_mix_g, w_in, q_norm_g, k_norm_g, attn_sink, mlstm_gate_b, mlstm_head_g, sgu_norm_g, sgu_w_s, sgu_b, w_out, norm_ffn2_g, ffn2_w_gate, ffn2_w_up, ffn2_w_down, norm_out_g):
    B, S, D = x.shape
    depth = w_in.shape[0]
    T = B * S
    xt = x.reshape(T, D)
    cos, sin = _rope_tables(positions)
    group = np.arange(M_W) // HEAD_DIM
    hmat = jnp.asarray((group[:, None] == group[None, :]) / HEAD_DIM, BF16)
    w_main = w_in[..., :MAIN_W].astype(BF16)
    w_gate = jnp.pad(w_in[..., MAIN_W:MAIN_W + N_GATES].astype(BF16), ((0, 0), (0, 0), (0, GATE_PAD - N_GATES)))
    w_sgu = w_in[..., MAIN_W + N_GATES:].astype(BF16)
    w_o = w_out.astype(BF16)
    for l in range(depth):
        xt = _ffn(xt, l, norm_ffn1_g[l], ffn1_w_gate, ffn1_w_up, ffn1_w_down)
        qa, kd, vat, mqk, mvt, mo, mgt, ys = _mix_in(xt, l, norm_mix_g[l], w_main, w_sgu, w_gate, q_norm_g[l],
                                                     k_norm_g[l], cos, sin, hmat, sgu_norm_g[l], sgu_w_s[l], sgu_b[l])
        ya, ym = _mixers(qa, kd, vat, attn_sink[l], mqk, mvt, mo, mgt, mlstm_gate_b[l], mlstm_head_g[l], B, S)
        xt = _ffn(xt, l, norm_ffn2_g[l], ffn2_w_gate, ffn2_w_up, ffn2_w_down, proj=(ya, ym, ys, w_o),
                  gout=norm_out_g[l])
    return xt.reshape(B, S, D)
```

```python
import functools

import numpy as np
import jax
import jax.numpy as jnp
from jax import lax
from jax.experimental import pallas as pl
from jax.experimental.pallas import tpu as pltpu

F32 = jnp.float32
BF16 = jnp.bfloat16

D_MODEL = 1024
HEAD_DIM = 64
N_Q_HEADS = 8
N_KV_HEADS = 2
Q_PER_KV = N_Q_HEADS // N_KV_HEADS
ATTN_Q_W = N_Q_HEADS * HEAD_DIM
ATTN_KV_W = N_KV_HEADS * HEAD_DIM
WINDOW = 128
BLK = 128
ROPE_THETA = 10000.0
M_HEADS = 4
M_W = M_HEADS * HEAD_DIM
N_GATES = 4 * M_HEADS
V_ROWS = HEAD_DIM + 16
SGU_GROUPS = 4
SGU_W = SGU_GROUPS * HEAD_DIM
D_FF = 2816
EPS = 1e-6
LANES = 128
GATE_PAD = LANES
MAIN_W = ATTN_Q_W + 2 * ATTN_KV_W + 4 * M_W
NEG = -1e30
LOG2E = float(np.log2(np.e))

VMEM_LIMIT = 56 * 1024 * 1024
FFN_TM = 512
FFN_SPLIT = 2
W_CHUNKS = 8
MIX_TM = 512
MIX_SPLIT = 1


def _dot(a, b):
    return jnp.dot(a, b, preferred_element_type=F32)


def _dot_nt(a, b):
    return lax.dot_general(a, b, (((1,), (1,)), ((), ())), preferred_element_type=F32)


def _dot_tn(a, b):
    return lax.dot_general(a, b, (((0,), (0,)), ((), ())), preferred_element_type=F32)


def _rms(x, g):
    ms = jnp.mean(x * x, axis=-1, keepdims=True)
    return x * lax.rsqrt(ms + EPS) * g


def _group_mean_sq(t, hmat):
    return _dot((t * t).astype(BF16), hmat)


def _const_spec(shape):
    nd = len(shape)
    return pl.BlockSpec(shape, lambda *_: (0,) * nd, pipeline_mode=pl.Buffered(1))


def _layer_spec(stacked_shape, layer):
    nd = len(stacked_shape) - 1
    return pl.BlockSpec((None,) + tuple(stacked_shape[1:]), lambda *_: (layer,) + (0,) * nd,
                        pipeline_mode=pl.Buffered(1))


def _params(n_grid):
    return pltpu.CompilerParams(dimension_semantics=("parallel",) * n_grid,
                                vmem_limit_bytes=VMEM_LIMIT)


ROPE_HALF = HEAD_DIM // 2
ROPE_PACK = LANES // ROPE_HALF
ROPE_STEPS = 8


def _rope_kernel(pos_ref, freq_ref, cos_ref, sin_ref):
    ang = pos_ref[...] * freq_ref[...]
    rows = ang.shape[0]
    token = lax.broadcasted_iota(jnp.int32, ang.shape, 1) >> 5
    for table, out_ref in ((jnp.cos(ang), cos_ref), (jnp.sin(ang), sin_ref)):
        for i in range(ROPE_PACK):
            x = jnp.where(token == i, table, 0.0)
            y = x + pltpu.roll(x, 2 * ROPE_HALF, 1)
            out_ref[pl.ds(i, rows, stride=ROPE_PACK), :] = y + pltpu.roll(y, ROPE_HALF, 1)


def _rope_tables(positions):
    T = positions.size
    freqs = ROPE_THETA ** (-jnp.arange(0, HEAD_DIM, 2, dtype=F32) / HEAD_DIM)
    pos = jnp.repeat(positions.reshape(T // ROPE_PACK, ROPE_PACK).astype(F32), ROPE_HALF, axis=1)
    freq_row = jnp.tile(freqs, ROPE_PACK).reshape(1, LANES)
    rows = T // ROPE_PACK // ROPE_STEPS
    return pl.pallas_call(
        _rope_kernel,
        grid=(ROPE_STEPS,),
        in_specs=[pl.BlockSpec((rows, LANES), lambda i: (i, 0)),
                  pl.BlockSpec((1, LANES), lambda i: (0, 0))],
        out_specs=[pl.BlockSpec((rows * ROPE_PACK, LANES), lambda i: (i, 0))] * 2,
        out_shape=[jax.ShapeDtypeStruct((T, LANES), F32)] * 2,
        compiler_params=_params(1),
        name="rope_tables",
    )(pos, freq_row)


def _load_weight_bf16(w_hbm, layer, dst, stage, sems):
    rows = stage.shape[1]
    n_chunks = dst.shape[0] // rows

    def copy(c):
        return pltpu.make_async_copy(w_hbm.at[layer, pl.ds(c * rows, rows), :], stage.at[c % 2], sems.at[c % 2])

    copy(0).start()
    for c in range(n_chunks):
        if c + 1 < n_chunks:
            copy(c + 1).start()
        copy(c).wait()
        dst[c * rows:(c + 1) * rows, :] = stage[c % 2].astype(BF16)


def _ffn_kernel(*refs, layer, has_proj, has_final):
    refs = list(refs)
    x_ref = refs.pop(0)
    if has_proj:
        ya_ref, ym_ref, ys_ref, wo_ref = refs[:4]
        refs = refs[4:]
    g_ref, wg_hbm, wu_hbm, wd_hbm = refs[:4]
    refs = refs[4:]
    if has_final:
        gout_ref = refs.pop(0)
    out_ref, wg_ref, wu_ref, wd_ref, stage_in, stage_out, sems = refs

    @pl.when(pl.program_id(0) == 0)
    def _():
        _load_weight_bf16(wg_hbm, layer, wg_ref, stage_in, sems)
        _load_weight_bf16(wu_hbm, layer, wu_ref, stage_in, sems)
        _load_weight_bf16(wd_hbm, layer, wd_ref, stage_out, sems)

    hs = x_ref.shape[0] // FFN_SPLIT
    rows = [slice(i * hs, (i + 1) * hs) for i in range(FFN_SPLIT)]
    xs = []
    for r in rows:
        x = x_ref[r, :]
        if has_proj:
            x = (x + _dot(ya_ref[r, :], wo_ref[0:ATTN_Q_W, :])
                 + _dot(ym_ref[r, :], wo_ref[ATTN_Q_W:ATTN_Q_W + M_W, :])
                 + _dot(ys_ref[r, :], wo_ref[ATTN_Q_W + M_W:, :]))
        xs.append(x)
    hidden = [_rms(x, g_ref[...]).astype(BF16) for x in xs]
    gate_up = [(_dot(h, wg_ref[...]), _dot(h, wu_ref[...])) for h in hidden]
    for r, x, (gate, up) in zip(rows, xs, gate_up):
        act = (gate * jax.nn.sigmoid(gate) * up).astype(BF16)
        y = x + 0.5 * _dot(act, wd_ref[...])
        if has_final:
            y = _rms(y, gout_ref[...])
        out_ref[r, :] = y


def _ffn(x, layer, g, wg, wu, wd, proj=None, gout=None, tm=FFN_TM):
    T = x.shape[0]
    d_model, d_ff = wg.shape[1:]
    row = lambda w: pl.BlockSpec((tm, w), lambda i: (i, 0))
    hbm = pl.BlockSpec(memory_space=pl.ANY)
    args, specs = [x], [row(D_MODEL)]
    if proj is not None:
        ya, ym, ys, wo = proj
        args += [ya, ym, ys, wo]
        specs += [row(ATTN_Q_W), row(M_W), row(SGU_W), _layer_spec(wo.shape, layer)]
    args += [g.reshape(1, D_MODEL), wg, wu, wd]
    specs += [_const_spec((1, D_MODEL)), hbm, hbm, hbm]
    if gout is not None:
        args.append(gout.reshape(1, D_MODEL))
        specs.append(_const_spec((1, D_MODEL)))
    return pl.pallas_call(
        functools.partial(_ffn_kernel, layer=layer, has_proj=proj is not None, has_final=gout is not None),
        grid=(T // tm,),
        in_specs=specs,
        out_specs=row(D_MODEL),
        out_shape=jax.ShapeDtypeStruct((T, D_MODEL), F32),
        scratch_shapes=[pltpu.VMEM((d_model, d_ff), BF16), pltpu.VMEM((d_model, d_ff), BF16),
                        pltpu.VMEM((d_ff, d_model), BF16),
                        pltpu.VMEM((2, d_model // W_CHUNKS, d_ff), F32),
                        pltpu.VMEM((2, d_ff // W_CHUNKS, d_model), F32),
                        pltpu.SemaphoreType.DMA((2,))],
        compiler_params=pltpu.CompilerParams(dimension_semantics=("arbitrary",), vmem_limit_bytes=VMEM_LIMIT),
        name="ffn",
    )(*args)


def _first_half(shape):
    lane = lax.broadcasted_iota(jnp.int32, shape, 1)
    return (lane & (HEAD_DIM - 1)) < HEAD_DIM // 2


def _rope(t, cos, sin_signed):
    n = t.shape[-1]
    reps = n // LANES
    swapped = jnp.where(_first_half(t.shape),
                        pltpu.roll(t, n - HEAD_DIM // 2, 1), pltpu.roll(t, HEAD_DIM // 2, 1))
    return t * jnp.tile(cos, (1, reps)) + swapped * jnp.tile(sin_signed, (1, reps))


def _gelu(x):
    return 0.5 * x * (1.0 + lax.erf(x * np.float32(np.sqrt(0.5))))


def _mixin_kernel(x_ref, g_ref, w_ref, wsg_ref, wgt_ref, qg_ref, kg_ref, cos_ref, sin_ref, hmat_ref, ng_ref, ws_ref,
                  bs_ref, qa_ref, kd_ref, vat_ref, mqk_ref, mvt_ref, mo_ref, mgt_ref, ys_ref):
    scale = HEAD_DIM ** -0.5
    hmat = hmat_ref[...]
    gw = hmat.shape[0]
    c_q, c_kv, c_mq, c_mk, c_mv, c_mo = (int(c) for c in np.cumsum([0, ATTN_Q_W, 2 * ATTN_KV_W, M_W, M_W, M_W]))
    lane_group = lax.broadcasted_iota(jnp.int32, (BLK, SGU_W), 1) >> 6

    def projections(r0, nrows):
        r = slice(r0, r0 + nrows)
        h = _rms(x_ref[r, :], g_ref[...]).astype(BF16)
        proj = lambda ref, start, width: _dot(h, ref[:, start:start + width])
        su = proj(wsg_ref, 0, SGU_W)
        sv = proj(wsg_ref, SGU_W, SGU_W)
        aq = proj(w_ref, c_q, ATTN_Q_W)
        akv = proj(w_ref, c_kv, 2 * ATTN_KV_W)
        ak, av = akv[:, :ATTN_KV_W], akv[:, ATTN_KV_W:]
        v = _gelu(sv)
        v_ms = _group_mean_sq(v, hmat)
        mqk_ref[r, 0:M_W] = proj(w_ref, c_mq, M_W).astype(BF16)
        mqk_ref[r, M_W:2 * M_W] = (proj(w_ref, c_mk, M_W) * scale).astype(BF16)
        q_ms = jnp.concatenate([_group_mean_sq(aq[:, i:i + gw], hmat) for i in range(0, ATTN_Q_W, gw)], axis=1)
        k_ms = _group_mean_sq(ak, hmat[0:ATTN_KV_W, 0:ATTN_KV_W])
        mvt_ref[:, r] = proj(w_ref, c_mv, M_W).T.astype(BF16)
        mo_ref[r, :] = proj(w_ref, c_mo, M_W).astype(BF16)
        mgt_ref[:, r] = proj(wgt_ref, 0, GATE_PAD).T[0:N_GATES, :]

        def tail():
            cos = cos_ref[r, :]
            sin_signed = jnp.where(_first_half(cos.shape), -sin_ref[r, :], sin_ref[r, :])
            qn = aq * lax.rsqrt(q_ms + EPS) * qg_ref[...]
            qa_ref[r, :] = (_rope(qn, cos, sin_signed) * (scale * LOG2E)).astype(BF16)
            kn = ak * lax.rsqrt(k_ms + EPS) * kg_ref[...]
            kr = _rope(kn, cos, sin_signed)
            ks = pltpu.roll(kr, HEAD_DIM, 1)
            low = lax.broadcasted_iota(jnp.int32, kr.shape, 1) < HEAD_DIM
            kd_ref[r, 0:LANES] = jnp.where(low, kr, ks).astype(BF16)
            kd_ref[r, LANES:2 * LANES] = jnp.where(low, ks, kr).astype(BF16)
            vat_ref[:, r] = av.T.astype(BF16)
            u = _gelu(su)
            vn = (v * lax.rsqrt(v_ms + EPS) * ng_ref[...]).astype(BF16)
            for c in range(nrows // BLK):
                vc = vn[c * BLK:(c + 1) * BLK]
                stack = jnp.concatenate(
                    [jnp.where(lane_group == g, vc, jnp.zeros_like(vc)) for g in range(SGU_GROUPS)], axis=0)
                mixed = _dot(ws_ref[...], stack)
                ys_ref[r0 + c * BLK:r0 + (c + 1) * BLK, :] = (
                    u[c * BLK:(c + 1) * BLK] * (mixed + bs_ref[...])).astype(BF16)

        return tail

    nrows = x_ref.shape[0] // MIX_SPLIT
    tails = [projections(i * nrows, nrows) for i in range(MIX_SPLIT)]
    for tail in tails:
        tail()


def _mix_in(x, layer, g, w_main, w_sgu, w_gate, q_g, k_g, cos, sin, hmat, sgu_g, sgu_w, sgu_b, tm=MIX_TM):
    T = x.shape[0]
    row = lambda w: pl.BlockSpec((tm, w), lambda i: (i, 0))
    tcol = lambda r: pl.BlockSpec((r, tm), lambda i: (0, i))
    out_specs = [row(ATTN_Q_W), row(2 * ATTN_KV_W), tcol(ATTN_KV_W), row(2 * M_W), tcol(M_W), row(M_W),
                 tcol(N_GATES), row(SGU_W)]
    out_shape = [jax.ShapeDtypeStruct(s, d) for s, d in [
        ((T, ATTN_Q_W), BF16), ((T, 2 * ATTN_KV_W), BF16), ((ATTN_KV_W, T), BF16), ((T, 2 * M_W), BF16),
        ((M_W, T), BF16), ((T, M_W), BF16), ((N_GATES, T), F32), ((T, SGU_W), BF16)]]
    ws_cat = jnp.transpose(sgu_w, (1, 0, 2)).reshape(BLK, SGU_GROUPS * BLK).astype(BF16)
    bs_full = jnp.repeat(jnp.transpose(sgu_b), HEAD_DIM, axis=1)
    return pl.pallas_call(
        _mixin_kernel,
        grid=(T // tm,),
        in_specs=[row(D_MODEL), _const_spec((1, D_MODEL)), _layer_spec(w_main.shape, layer), _layer_spec(w_sgu.shape, layer),
                  _layer_spec(w_gate.shape, layer), _const_spec((1, ATTN_Q_W)), _const_spec((1, ATTN_KV_W)),
                  row(LANES), row(LANES), _const_spec(hmat.shape),
                  _const_spec((1, SGU_W)), _const_spec(ws_cat.shape), _const_spec(bs_full.shape)],
        out_specs=out_specs,
        out_shape=out_shape,
        compiler_params=_params(1),
        name="mix_in",
    )(x, g.reshape(1, D_MODEL), w_main, w_sgu, w_gate,
      jnp.tile(q_g, N_Q_HEADS).reshape(1, ATTN_Q_W), jnp.tile(k_g, N_KV_HEADS).reshape(1, ATTN_KV_W),
      cos, sin, hmat, sgu_g.reshape(1, SGU_W), ws_cat, bs_full)


def _attn_blocks(sink_ref, q_ref, kd_ref, vt_ref, o_ref):
    S = q_ref.shape[0]
    nb = S // BLK
    kc = lax.broadcasted_iota(jnp.int32, (BLK, BLK), 0)
    qi = lax.broadcasted_iota(jnp.int32, (BLK, BLK), 1)
    prev_bias = jnp.tile(jnp.where(kc >= qi, 0.0, NEG), (1, Q_PER_KV))
    next_bias = jnp.tile(jnp.where(kc <= qi, 0.0, NEG), (1, Q_PER_KV))
    half_mask = (qi < HEAD_DIM, qi >= HEAD_DIM)
    head_lane = lax.broadcasted_iota(jnp.int32, (1, Q_PER_KV * BLK), 1) >> 7
    sink_rows = []
    for kvh in range(N_KV_HEADS):
        row = jnp.zeros((1, Q_PER_KV * BLK), F32)
        for g in range(Q_PER_KV):
            row = jnp.where(head_lane == g, sink_ref[kvh * Q_PER_KV + g] * LOG2E, row)
        sink_rows.append(row)

    def block(q0, start, biases):
        nk = len(biases)
        scores = []
        for kvh in range(N_KV_HEADS):
            kd = kd_ref[pl.ds(start, nk * BLK), kvh * LANES:(kvh + 1) * LANES]
            qs = []
            for g in range(Q_PER_KV):
                t = kvh * Q_PER_KV + g
                qt = q_ref[pl.ds(q0, BLK), (t // 2) * LANES:(t // 2 + 1) * LANES]
                qs.append(jnp.where(half_mask[t & 1], qt, jnp.zeros_like(qt)))
            scores.append(_dot_nt(kd, jnp.concatenate(qs, axis=0)))

        def weighted_values():
            ones = jnp.ones((HEAD_DIM, nk * BLK), BF16)
            outs = []
            for kvh in range(N_KV_HEADS):
                v_ext = jnp.concatenate([vt_ref[kvh * HEAD_DIM:(kvh + 1) * HEAD_DIM, pl.ds(start, nk * BLK)], ones],
                                        axis=0)
                parts = [scores[kvh][j * BLK:(j + 1) * BLK] for j in range(nk)]
                parts = [p if b is None else p + b for p, b in zip(parts, biases)]
                mx = jnp.max(functools.reduce(jnp.maximum, parts), axis=0, keepdims=True)
                m = jnp.maximum(mx, sink_rows[kvh])
                p = jnp.concatenate([jnp.exp2(p - m) for p in parts], axis=0).astype(BF16)
                oe = _dot(v_ext, p)
                denom = oe[HEAD_DIM:HEAD_DIM + 1] + jnp.exp2(sink_rows[kvh] - m)
                on = oe[0:HEAD_DIM] / denom
                for pair in range(Q_PER_KV // 2):
                    two = jnp.concatenate([on[:, (2 * pair) * BLK:(2 * pair + 1) * BLK],
                                           on[:, (2 * pair + 1) * BLK:(2 * pair + 2) * BLK]], axis=0)
                    outs.append(two.T)
            result = jnp.concatenate(outs, axis=1).astype(BF16)

            def commit():
                o_ref[pl.ds(q0, BLK), :] = result
            return commit

        return weighted_values

    def attend(n):
        if isinstance(n, int) and n == 0:
            return block(0, 0, [None, next_bias])
        if isinstance(n, int) and n == nb - 1:
            return block(n * BLK, (n - 1) * BLK, [prev_bias, None])
        return block(pl.multiple_of(n * BLK, BLK), pl.multiple_of((n - 1) * BLK, BLK), [prev_bias, None, next_bias])

    return attend


def _log_sigmoid(x):
    return jnp.minimum(x, 0.0) - jnp.log(1.0 + jnp.exp(-jnp.abs(x)))


def _mlstm_chunks(qk_ref, vt_ref, mo_ref, mgt_ref, gb_ref, hg_ref, o_ref,
                  gt_s, b_s, w_s, mloc_s, tot_s, rcol_s, cloc_s, cprev_s, mprev_s, keep_s, add_s):
    S = qk_ref.shape[0]
    L = BLK
    nc = S // L
    ng = N_GATES
    sub = lax.broadcasted_iota(jnp.int32, (L, L), 0)
    lane = lax.broadcasted_iota(jnp.int32, (L, L), 1)

    for c in range(nc):
        gt_s[c * ng:(c + 1) * ng, :] = mgt_ref[:, c * L:(c + 1) * L] + gb_ref[...]
    gt = gt_s[...]
    row = lax.broadcasted_iota(jnp.int32, gt.shape, 0)
    gt = jnp.where(((row >> 2) & 1) == 1, _log_sigmoid(gt), gt)
    gt = gt * LOG2E
    upper = jnp.where(sub <= lane, 1.0, 0.0).astype(BF16)
    g1 = gt.astype(BF16)
    r1 = gt - g1.astype(F32)
    g2 = r1.astype(BF16)
    g3 = (r1 - g2.astype(F32)).astype(BF16)
    pre = _dot(g1, upper) + _dot(g2, upper) + _dot(g3, upper)
    tot = jnp.broadcast_to(pre[:, L - 1:L], pre.shape)
    bsum =jnp.where((row & (ng - 1)) < ng // 2, pre, tot - pre + gt)
    li = pltpu.roll(gt, M_HEADS, 0)
    a = tot - bsum + li
    mloc = jnp.broadcast_to(jnp.max(a, axis=1, keepdims=True), a.shape)
    b_s[...] = bsum
    w_s[...] = jnp.exp2(a - mloc)
    mloc_s[...] = mloc
    tot_s[...] = tot
    rdiff = li - bsum
    zpad = jnp.zeros((L - ng, L), F32)
    for c in range(nc):
        rcol_s[c * L:(c + 1) * L, :] = jnp.concatenate([rdiff[c * ng:(c + 1) * ng, :], zpad], axis=0).T

    def value_rows(hd, r0, dtype):
        vt = vt_ref[hd * HEAD_DIM:(hd + 1) * HEAD_DIM, pl.ds(r0, L)].astype(dtype)
        return jnp.concatenate([vt, jnp.ones((V_ROWS - HEAD_DIM, L), dtype)], axis=0)

    def local_state(c, carry):
        r0 = pl.multiple_of(c * L, L)
        g0 = pl.multiple_of(c * ng, ng)
        k = qk_ref[pl.ds(r0, L), M_W:2 * M_W]
        w = w_s[pl.ds(g0, ng), :]
        for hd in range(M_HEADS):
            vte = value_rows(hd, r0, F32)
            lhs = jnp.concatenate([vte * w[4 + hd:5 + hd, :], vte * w[12 + hd:13 + hd, :]], axis=0)
            cl = _dot(lhs.astype(BF16), k[:, (hd // 2) * LANES:(hd // 2 + 1) * LANES])
            cloc_s[0, hd, c] = cl[0:V_ROWS]
            cloc_s[1, hd, c] = cl[V_ROWS:2 * V_ROWS]
        return carry

    lax.fori_loop(0, nc, local_state, 0, unroll=2)

    jrow = lax.broadcasted_iota(jnp.int32, (ng, L), 0)

    def scan_stabilisers(i, m):
        cf = i
        cb = nc - 1 - i
        gf = pl.multiple_of(cf * ng, ng)
        gb = pl.multiple_of(cb * ng, ng)
        is_fwd = jrow < ng // 2
        blast = jnp.where(is_fwd, tot_s[pl.ds(gf, ng), :], tot_s[pl.ds(gb, ng), :])
        mloc_i = jnp.where(is_fwd, mloc_s[pl.ds(gf, ng), :], mloc_s[pl.ds(gb, ng), :])
        mprev_s[0, pl.ds(gf, ng), :] = m
        mprev_s[1, pl.ds(gb, ng), :] = m
        m_new = jnp.maximum(blast + m, mloc_i)
        gi = pl.multiple_of(i * ng, ng)
        keep_s[pl.ds(gi, ng), :] = jnp.exp2(blast + m - m_new)
        add_s[pl.ds(gi, ng), :] = jnp.exp2(mloc_i - m_new)
        return m_new

    lax.fori_loop(0, nc, scan_stabilisers, jnp.zeros((ng, L), F32))

    for hd in range(M_HEADS):
        def scan_states(i, states, hd=hd):
            gi = pl.multiple_of(i * ng, ng)
            keep = keep_s[pl.ds(gi, ng), :]
            add = add_s[pl.ds(gi, ng), :]
            new_states = []
            for d, ci in enumerate((i, nc - 1 - i)):
                r = 4 + hd + 8 * d
                cprev_s[d, hd, ci] = states[d].astype(BF16)
                new_states.append(keep[r:r + 1, :] * states[d] + add[r:r + 1, :] * cloc_s[d, hd, ci])
            return tuple(new_states)

        lax.fori_loop(0, nc, scan_states, (jnp.zeros((V_ROWS, LANES), F32), jnp.zeros((V_ROWS, LANES), F32)))

    causal = (sub <= lane, sub >= lane)

    def outputs(c):
        r0 = pl.multiple_of(c * L, L)
        g0 = pl.multiple_of(c * ng, ng)
        q = qk_ref[pl.ds(r0, L), 0:M_W]
        k = qk_ref[pl.ds(r0, L), M_W:2 * M_W]
        bc = b_s[pl.ds(g0, ng), :]
        mprev = (mprev_s[0, pl.ds(g0, ng), :], mprev_s[1, pl.ds(g0, ng), :])
        rc = rcol_s[pl.ds(r0, L), :]
        ys = []
        for hd in range(M_HEADS):
            t = hd // 2
            qp = q[:, t * LANES:(t + 1) * LANES]
            qm = jnp.where((lane >> 6) == (hd & 1), qp, jnp.zeros_like(qp))
            x = jnp.concatenate([k[:, t * LANES:(t + 1) * LANES], cprev_s[0, hd, c], cprev_s[1, hd, c]], axis=0)
            ys.append(_dot_nt(x, qm))

        def combine():
            houts = []
            for hd in range(M_HEADS):
                y = ys[hd]
                st = y[0:L]
                pts, scs, mts = [], [], []
                for d in range(2):
                    r = 4 + hd + 8 * d
                    b_row = bc[r:r + 1, :]
                    dm = jnp.where(causal[d], rc[:, r:r + 1] + b_row, NEG)
                    inter = b_row + mprev[d][r:r + 1, :]
                    m_t = jnp.maximum(inter, jnp.max(dm, axis=0, keepdims=True))
                    pts.append((jnp.exp2(dm - m_t) * st).astype(BF16))
                    scs.append(jnp.exp2(inter - m_t))
                    mts.append(m_t)
                n2 = _dot(value_rows(hd, r0, BF16), jnp.concatenate(pts, axis=1))
                ht = None
                for d in range(2):
                    tt = n2[:, d * L:(d + 1) * L] + scs[d] * y[L + d * V_ROWS:L + (d + 1) * V_ROWS]
                    den = tt[HEAD_DIM:HEAD_DIM + 1, :]
                    hd_out = tt[0:HEAD_DIM] / jnp.maximum(jnp.abs(den), jnp.exp2(-mts[d]))
                    ht = hd_out if ht is None else ht + hd_out
                ms = jnp.mean(ht * ht, axis=0, keepdims=True)
                houts.append(ht * lax.rsqrt(ms + EPS))
            hn = (jnp.concatenate(houts, axis=0) * hg_ref[...]).T
            result = (jax.nn.sigmoid(mo_ref[pl.ds(r0, L), :].astype(F32)) * hn).astype(BF16)

            def commit():
                o_ref[pl.ds(r0, L), :] = result
            return commit

        return combine

    return outputs


def _mixers_kernel(sink_ref, q_ref, kd_ref, vat_ref, qk_ref, mvt_ref, mo_ref, mgt_ref, gb_ref, hg_ref,
                   ya_ref, ym_ref, *scratch):
    nb = q_ref.shape[0] // BLK
    attend = _attn_blocks(sink_ref, q_ref, kd_ref, vat_ref, ya_ref)
    outputs = _mlstm_chunks(qk_ref, mvt_ref, mo_ref, mgt_ref, gb_ref, hg_ref, ym_ref, *scratch)

    def run(blocks):
        second_stages = [f(n) for n in blocks for f in (attend, outputs)]
        commits = [stage() for stage in second_stages]
        for commit in commits:
            commit()

    def pair(i, carry):
        run((1 + 2 * i, 2 + 2 * i))
        return carry

    run((0, nb - 1))
    lax.fori_loop(0, (nb - 2) // 2, pair, 0)


def _mixers(qa, kd, vat, sink, mqk, mvt, mo, mgt, gate_b, head_g, B, S):
    seq = lambda w: pl.BlockSpec((S, w), lambda b: (b, 0))
    tseq = lambda r: pl.BlockSpec((r, S), lambda b: (0, b))
    nc = S // BLK
    assert nc % 2 == 0 and nc >= 4
    gb = jnp.broadcast_to(gate_b.reshape(N_GATES, 1), (N_GATES, BLK))
    hg_t = jnp.broadcast_to(head_g.reshape(M_W, 1), (M_W, BLK))
    rows = nc * N_GATES
    return pl.pallas_call(
        _mixers_kernel,
        grid=(B,),
        in_specs=[pl.BlockSpec(memory_space=pltpu.SMEM), seq(ATTN_Q_W), seq(2 * ATTN_KV_W), tseq(ATTN_KV_W),
                  seq(2 * M_W), tseq(M_W), seq(M_W), tseq(N_GATES), _const_spec((N_GATES, BLK)),
                  _const_spec((M_W, BLK))],
        out_specs=[seq(ATTN_Q_W), seq(M_W)],
        out_shape=[jax.ShapeDtypeStruct((B * S, ATTN_Q_W), BF16), jax.ShapeDtypeStruct((B * S, M_W), BF16)],
        scratch_shapes=[pltpu.VMEM((rows, BLK), F32),
                        pltpu.VMEM((rows, BLK), F32),
                        pltpu.VMEM((rows, BLK), F32),
                        pltpu.VMEM((rows, BLK), F32),
                        pltpu.VMEM((rows, BLK), F32),
                        pltpu.VMEM((S, LANES), F32),
                        pltpu.VMEM((2, M_HEADS, nc, V_ROWS, LANES), F32),
                        pltpu.VMEM((2, M_HEADS, nc, V_ROWS, LANES), BF16),
                        pltpu.VMEM((2, rows, BLK), F32),
                        pltpu.VMEM((rows, BLK), F32),
                        pltpu.VMEM((rows, BLK), F32)],
        compiler_params=_params(1),
        name="mixers",
    )(sink, qa, kd, vat, mqk, mvt, mo, mgt, gb, hg_t)


def kernel(x, positions, norm_ffn1_g, ffn1_w_gate, ffn1_w_up, ffn1_w_down, norm_mix_g, w_in, q_norm_g, k_norm_g, attn_sink, mlstm_gate_b, mlstm_head_g, sgu_norm_g, sgu_w_s, sgu_b, w_out, norm_ffn2_g, ffn2_w_gate, ffn2_w_up, ffn2_w_down, norm_out_g):
    B, S, D = x.shape
    depth = w_in.shape[0]
    T = B * S
    xt = x.reshape(T, D)
    cos, sin = _rope_tables(positions)
    group = np.arange(M_W) // HEAD_DIM
    hmat = jnp.asarray((group[:, None] == group[None, :]) / HEAD_DIM, BF16)
    w_main = w_in[..., :MAIN_W].astype(BF16)
    w_gate = jnp.pad(w_in[..., MAIN_W:MAIN_W + N_GATES].astype(BF16), ((0, 0), (0, 0), (0, GATE_PAD - N_GATES)))
    w_sgu = w_in[..., MAIN_W + N_GATES:].astype(BF16)
    w_o = w_out.astype(BF16)
    for l in range(depth):
        xt = _ffn(xt, l, norm_ffn1_g[l], ffn1_w_gate, ffn1_w_up, ffn1_w_down)
        qa, kd, vat, mqk, mvt, mo, mgt, ys = _mix_in(xt, l, norm_mix_g[l], w_main, w_sgu, w_gate, q_norm_g[l],
                                                     k_norm_g[l], cos, sin, hmat, sgu_norm_g[l], sgu_w_s[l], sgu_b[l])
        ya, ym = _mixers(qa, kd, vat, attn_sink[l], mqk, mvt, mo, mgt, mlstm_gate_b[l], mlstm_head_g[l], B, S)
        xt = _ffn(xt, l, norm_ffn2_g[l], ffn2_w_gate, ffn2_w_up, ffn2_w_down, proj=(ya, ym, ys, w_o),
                  gout=norm_out_g[l])
    return xt.reshape(B, S, D)
```

```python
import functools

import numpy as np
import jax
import jax.numpy as jnp
from jax import lax
from jax.experimental import pallas as pl
from jax.experimental.pallas import tpu as pltpu

F32 = jnp.float32
BF16 = jnp.bfloat16

D_MODEL = 1024
HEAD_DIM = 64
N_Q_HEADS = 8
N_KV_HEADS = 2
Q_PER_KV = N_Q_HEADS // N_KV_HEADS
ATTN_Q_W = N_Q_HEADS * HEAD_DIM
ATTN_KV_W = N_KV_HEADS * HEAD_DIM
WINDOW = 128
BLK = 128
ROPE_THETA = 10000.0
M_HEADS = 4
M_W = M_HEADS * HEAD_DIM
N_GATES = 4 * M_HEADS
V_ROWS = HEAD_DIM + 16
SGU_GROUPS = 4
SGU_W = SGU_GROUPS * HEAD_DIM
D_FF = 2816
EPS = 1e-6
LANES = 128
GATE_PAD = LANES
MAIN_W = ATTN_Q_W + 2 * ATTN_KV_W + 4 * M_W
NEG = -1e30
LOG2E = float(np.log2(np.e))

VMEM_LIMIT = 56 * 1024 * 1024
FFN_TM = 512
FFN_SPLIT = 2
W_CHUNKS = 8
W_SLOTS = 3
MIX_TM = 512
MIX_SPLIT = 1


def _dot(a, b):
    return jnp.dot(a, b, preferred_element_type=F32)


def _dot_nt(a, b):
    return lax.dot_general(a, b, (((1,), (1,)), ((), ())), preferred_element_type=F32)


def _dot_tn(a, b):
    return lax.dot_general(a, b, (((0,), (0,)), ((), ())), preferred_element_type=F32)


def _aligned(x, m):
    return x if isinstance(x, int) else pl.multiple_of(x, m)


def _rms(x, g):
    ms = jnp.mean(x * x, axis=-1, keepdims=True)
    return x * lax.rsqrt(ms + EPS) * g


def _group_mean_sq(t, hmat):
    return _dot((t * t).astype(BF16), hmat)


def _const_spec(shape):
    nd = len(shape)
    return pl.BlockSpec(shape, lambda *_: (0,) * nd, pipeline_mode=pl.Buffered(1))


def _layer_spec(stacked_shape, layer):
    nd = len(stacked_shape) - 1
    return pl.BlockSpec((None,) + tuple(stacked_shape[1:]), lambda *_: (layer,) + (0,) * nd,
                        pipeline_mode=pl.Buffered(1))


def _params(n_grid):
    return pltpu.CompilerParams(dimension_semantics=("parallel",) * n_grid,
                                vmem_limit_bytes=VMEM_LIMIT)


ROPE_HALF = HEAD_DIM // 2
ROPE_PACK = LANES // ROPE_HALF
ROPE_STEPS = 8


def _rope_kernel(pos_ref, freq_ref, cos_ref, sin_ref):
    ang = pos_ref[...] * freq_ref[...]
    rows = ang.shape[0]
    token = lax.broadcasted_iota(jnp.int32, ang.shape, 1) >> 5
    for table, out_ref in ((jnp.cos(ang), cos_ref), (jnp.sin(ang), sin_ref)):
        for i in range(ROPE_PACK):
            x = jnp.where(token == i, table, 0.0)
            y = x + pltpu.roll(x, 2 * ROPE_HALF, 1)
            out_ref[pl.ds(i, rows, stride=ROPE_PACK), :] = y + pltpu.roll(y, ROPE_HALF, 1)


def _rope_tables(positions):
    T = positions.size
    freqs = ROPE_THETA ** (-jnp.arange(0, HEAD_DIM, 2, dtype=F32) / HEAD_DIM)
    pos = jnp.repeat(positions.reshape(T // ROPE_PACK, ROPE_PACK).astype(F32), ROPE_HALF, axis=1)
    freq_row = jnp.tile(freqs, ROPE_PACK).reshape(1, LANES)
    rows = T // ROPE_PACK // ROPE_STEPS
    return pl.pallas_call(
        _rope_kernel,
        grid=(ROPE_STEPS,),
        in_specs=[pl.BlockSpec((rows, LANES), lambda i: (i, 0)),
                  pl.BlockSpec((1, LANES), lambda i: (0, 0))],
        out_specs=[pl.BlockSpec((rows * ROPE_PACK, LANES), lambda i: (i, 0))] * 2,
        out_shape=[jax.ShapeDtypeStruct((T, LANES), F32)] * 2,
        compiler_params=_params(1),
        name="rope_tables",
    )(pos, freq_row)


def _load_weight_bf16(w_hbm, layer, dst, stage, sems):
    n_slots, rows = stage.shape[:2]
    n_chunks = dst.shape[0] // rows

    def copy(c):
        slot = c % n_slots
        return pltpu.make_async_copy(w_hbm.at[layer, pl.ds(c * rows, rows), :], stage.at[slot], sems.at[slot])

    for c in range(min(n_slots - 1, n_chunks)):
        copy(c).start()
    for c in range(n_chunks):
        if c + n_slots - 1 < n_chunks:
            copy(c + n_slots - 1).start()
        copy(c).wait()
        dst[c * rows:(c + 1) * rows, :] = stage[c % n_slots].astype(BF16)


def _ffn_kernel(*refs, layer, has_proj, has_final):
    refs = list(refs)
    x_ref = refs.pop(0)
    if has_proj:
        ya_ref, ym_ref, ys_ref, wo_ref = refs[:4]
        refs = refs[4:]
    g_ref, wg_hbm, wu_hbm, wd_hbm = refs[:4]
    refs = refs[4:]
    if has_final:
        gout_ref = refs.pop(0)
    out_ref, wg_ref, wu_ref, wd_ref, stage_in, stage_out, sems = refs

    @pl.when(pl.program_id(0) == 0)
    def _():
        _load_weight_bf16(wg_hbm, layer, wg_ref, stage_in, sems)
        _load_weight_bf16(wu_hbm, layer, wu_ref, stage_in, sems)
        _load_weight_bf16(wd_hbm, layer, wd_ref, stage_out, sems)

    hs = x_ref.shape[0] // FFN_SPLIT
    rows = [slice(i * hs, (i + 1) * hs) for i in range(FFN_SPLIT)]
    xs = []
    for r in rows:
        x = x_ref[r, :]
        if has_proj:
            x = (x + _dot(ya_ref[r, :], wo_ref[0:ATTN_Q_W, :])
                 + _dot(ym_ref[r, :], wo_ref[ATTN_Q_W:ATTN_Q_W + M_W, :])
                 + _dot(ys_ref[r, :], wo_ref[ATTN_Q_W + M_W:, :]))
        xs.append(x)
    hidden = [_rms(x, g_ref[...]).astype(BF16) for x in xs]
    gate_up = [(_dot(h, wg_ref[...]), _dot(h, wu_ref[...])) for h in hidden]
    for r, x, (gate, up) in zip(rows, xs, gate_up):
        act = (gate * jax.nn.sigmoid(gate) * up).astype(BF16)
        y = x + 0.5 * _dot(act, wd_ref[...])
        if has_final:
            y = _rms(y, gout_ref[...])
        out_ref[r, :] = y


def _ffn(x, layer, g, wg, wu, wd, proj=None, gout=None, tm=FFN_TM):
    T = x.shape[0]
    d_model, d_ff = wg.shape[1:]
    row = lambda w: pl.BlockSpec((tm, w), lambda i: (i, 0))
    hbm = pl.BlockSpec(memory_space=pl.ANY)
    args, specs = [x], [row(D_MODEL)]
    if proj is not None:
        ya, ym, ys, wo = proj
        args += [ya, ym, ys, wo]
        specs += [row(ATTN_Q_W), row(M_W), row(SGU_W), _layer_spec(wo.shape, layer)]
    args += [g.reshape(1, D_MODEL), wg, wu, wd]
    specs += [_const_spec((1, D_MODEL)), hbm, hbm, hbm]
    if gout is not None:
        args.append(gout.reshape(1, D_MODEL))
        specs.append(_const_spec((1, D_MODEL)))
    return pl.pallas_call(
        functools.partial(_ffn_kernel, layer=layer, has_proj=proj is not None, has_final=gout is not None),
        grid=(T // tm,),
        in_specs=specs,
        out_specs=row(D_MODEL),
        out_shape=jax.ShapeDtypeStruct((T, D_MODEL), F32),
        scratch_shapes=[pltpu.VMEM((d_model, d_ff), BF16), pltpu.VMEM((d_model, d_ff), BF16),
                        pltpu.VMEM((d_ff, d_model), BF16),
                        pltpu.VMEM((W_SLOTS, d_model // W_CHUNKS, d_ff), F32),
                        pltpu.VMEM((W_SLOTS, d_ff // W_CHUNKS, d_model), F32),
                        pltpu.SemaphoreType.DMA((W_SLOTS,))],
        compiler_params=pltpu.CompilerParams(dimension_semantics=("arbitrary",), vmem_limit_bytes=VMEM_LIMIT),
        name="ffn",
    )(*args)


def _first_half(shape):
    lane = lax.broadcasted_iota(jnp.int32, shape, 1)
    return (lane & (HEAD_DIM - 1)) < HEAD_DIM // 2


def _rope(t, cos, sin_signed):
    n = t.shape[-1]
    reps = n // LANES
    swapped = jnp.where(_first_half(t.shape),
                        pltpu.roll(t, n - HEAD_DIM // 2, 1), pltpu.roll(t, HEAD_DIM // 2, 1))
    return t * jnp.tile(cos, (1, reps)) + swapped * jnp.tile(sin_signed, (1, reps))


def _gelu(x):
    return 0.5 * x * (1.0 + lax.erf(x * np.float32(np.sqrt(0.5))))


def _mixin_kernel(x_ref, g_ref, w_ref, wsg_ref, wgt_ref, qg_ref, kg_ref, cos_ref, sin_ref, hmat_ref, ng_ref, ws_ref,
                  bs_ref, qa_ref, kd_ref, vat_ref, mqk_ref, mvt_ref, mo_ref, mgt_ref, ys_ref):
    scale = HEAD_DIM ** -0.5
    hmat = hmat_ref[...]
    gw = hmat.shape[0]
    c_q, c_kv, c_mq, c_mk, c_mv, c_mo = (int(c) for c in np.cumsum([0, ATTN_Q_W, 2 * ATTN_KV_W, M_W, M_W, M_W]))
    lane_group = lax.broadcasted_iota(jnp.int32, (BLK, SGU_W), 1) >> 6

    def projections(r0, nrows):
        r = slice(r0, r0 + nrows)
        h = _rms(x_ref[r, :], g_ref[...]).astype(BF16)
        proj = lambda ref, start, width: _dot(h, ref[:, start:start + width])
        su = proj(wsg_ref, 0, SGU_W)
        sv = proj(wsg_ref, SGU_W, SGU_W)
        aq = proj(w_ref, c_q, ATTN_Q_W)
        akv = proj(w_ref, c_kv, 2 * ATTN_KV_W)
        ak, av = akv[:, :ATTN_KV_W], akv[:, ATTN_KV_W:]
        v = _gelu(sv)
        v_ms = _group_mean_sq(v, hmat)
        mqk_ref[r, 0:M_W] = proj(w_ref, c_mq, M_W).astype(BF16)
        mqk_ref[r, M_W:2 * M_W] = (proj(w_ref, c_mk, M_W) * scale).astype(BF16)
        q_ms = jnp.concatenate([_group_mean_sq(aq[:, i:i + gw], hmat) for i in range(0, ATTN_Q_W, gw)], axis=1)
        k_ms = _group_mean_sq(ak, hmat[0:ATTN_KV_W, 0:ATTN_KV_W])
        mvt_ref[:, r] = proj(w_ref, c_mv, M_W).T.astype(BF16)
        mo_ref[r, :] = proj(w_ref, c_mo, M_W).astype(BF16)
        mgt_ref[:, r] = proj(wgt_ref, 0, GATE_PAD).T[0:N_GATES, :]

        def tail():
            cos = cos_ref[r, :]
            sin_signed = jnp.where(_first_half(cos.shape), -sin_ref[r, :], sin_ref[r, :])
            qn = aq * lax.rsqrt(q_ms + EPS) * qg_ref[...]
            qa_ref[r, :] = (_rope(qn, cos, sin_signed) * (scale * LOG2E)).astype(BF16)
            kn = ak * lax.rsqrt(k_ms + EPS) * kg_ref[...]
            kr = _rope(kn, cos, sin_signed)
            ks = pltpu.roll(kr, HEAD_DIM, 1)
            low = lax.broadcasted_iota(jnp.int32, kr.shape, 1) < HEAD_DIM
            kd_ref[r, 0:LANES] = jnp.where(low, kr, ks).astype(BF16)
            kd_ref[r, LANES:2 * LANES] = jnp.where(low, ks, kr).astype(BF16)
            vat_ref[:, r] = av.T.astype(BF16)
            u = _gelu(su)
            vn = (v * lax.rsqrt(v_ms + EPS) * ng_ref[...]).astype(BF16)
            for c in range(nrows // BLK):
                vc = vn[c * BLK:(c + 1) * BLK]
                stack = jnp.concatenate(
                    [jnp.where(lane_group == g, vc, jnp.zeros_like(vc)) for g in range(SGU_GROUPS)], axis=0)
                mixed = _dot(ws_ref[...], stack)
                ys_ref[r0 + c * BLK:r0 + (c + 1) * BLK, :] = (
                    u[c * BLK:(c + 1) * BLK] * (mixed + bs_ref[...])).astype(BF16)

        return tail

    nrows = x_ref.shape[0] // MIX_SPLIT
    tails = [projections(i * nrows, nrows) for i in range(MIX_SPLIT)]
    for tail in tails:
        tail()


def _mix_in(x, layer, g, w_main, w_sgu, w_gate, q_g, k_g, cos, sin, hmat, sgu_g, sgu_w, sgu_b, tm=MIX_TM):
    T = x.shape[0]
    row = lambda w: pl.BlockSpec((tm, w), lambda i: (i, 0))
    tcol = lambda r: pl.BlockSpec((r, tm), lambda i: (0, i))
    out_specs = [row(ATTN_Q_W), row(2 * ATTN_KV_W), tcol(ATTN_KV_W), row(2 * M_W), tcol(M_W), row(M_W),
                 tcol(N_GATES), row(SGU_W)]
    out_shape = [jax.ShapeDtypeStruct(s, d) for s, d in [
        ((T, ATTN_Q_W), BF16), ((T, 2 * ATTN_KV_W), BF16), ((ATTN_KV_W, T), BF16), ((T, 2 * M_W), BF16),
        ((M_W, T), BF16), ((T, M_W), BF16), ((N_GATES, T), F32), ((T, SGU_W), BF16)]]
    ws_cat = jnp.transpose(sgu_w, (1, 0, 2)).reshape(BLK, SGU_GROUPS * BLK).astype(BF16)
    bs_full = jnp.repeat(jnp.transpose(sgu_b), HEAD_DIM, axis=1)
    return pl.pallas_call(
        _mixin_kernel,
        grid=(T // tm,),
        in_specs=[row(D_MODEL), _const_spec((1, D_MODEL)), _layer_spec(w_main.shape, layer), _layer_spec(w_sgu.shape, layer),
                  _layer_spec(w_gate.shape, layer), _const_spec((1, ATTN_Q_W)), _const_spec((1, ATTN_KV_W)),
                  row(LANES), row(LANES), _const_spec(hmat.shape),
                  _const_spec((1, SGU_W)), _const_spec(ws_cat.shape), _const_spec(bs_full.shape)],
        out_specs=out_specs,
        out_shape=out_shape,
        compiler_params=_params(1),
        name="mix_in",
    )(x, g.reshape(1, D_MODEL), w_main, w_sgu, w_gate,
      jnp.tile(q_g, N_Q_HEADS).reshape(1, ATTN_Q_W), jnp.tile(k_g, N_KV_HEADS).reshape(1, ATTN_KV_W),
      cos, sin, hmat, sgu_g.reshape(1, SGU_W), ws_cat, bs_full)


def _attn_blocks(sink_ref, q_ref, kd_ref, vt_ref, o_ref):
    S = q_ref.shape[0]
    nb = S // BLK
    kc = lax.broadcasted_iota(jnp.int32, (BLK, BLK), 0)
    qi = lax.broadcasted_iota(jnp.int32, (BLK, BLK), 1)
    prev_bias = jnp.tile(jnp.where(kc >= qi, 0.0, NEG), (1, Q_PER_KV))
    next_bias = jnp.tile(jnp.where(kc <= qi, 0.0, NEG), (1, Q_PER_KV))
    half_mask = (qi < HEAD_DIM, qi >= HEAD_DIM)
    head_lane = lax.broadcasted_iota(jnp.int32, (1, Q_PER_KV * BLK), 1) >> 7
    sink_rows = []
    for kvh in range(N_KV_HEADS):
        row = jnp.zeros((1, Q_PER_KV * BLK), F32)
        for g in range(Q_PER_KV):
            row = jnp.where(head_lane == g, sink_ref[kvh * Q_PER_KV + g] * LOG2E, row)
        sink_rows.append(row)

    def block(q0, start, biases, with_scores):
        nk = len(biases)
        scores = []
        for kvh in range(N_KV_HEADS if with_scores else 0):
            kd = kd_ref[pl.ds(start, nk * BLK), kvh * LANES:(kvh + 1) * LANES]
            qs = []
            for g in range(Q_PER_KV):
                t = kvh * Q_PER_KV + g
                qt = q_ref[pl.ds(q0, BLK), (t // 2) * LANES:(t // 2 + 1) * LANES]
                qs.append(jnp.where(half_mask[t & 1], qt, jnp.zeros_like(qt)))
            scores.append(_dot_nt(kd, jnp.concatenate(qs, axis=0)))

        def weighted_values(scores):
            ones = jnp.ones((HEAD_DIM, nk * BLK), BF16)
            outs = []
            for kvh in range(N_KV_HEADS):
                v_ext = jnp.concatenate([vt_ref[kvh * HEAD_DIM:(kvh + 1) * HEAD_DIM, pl.ds(start, nk * BLK)], ones],
                                        axis=0)
                parts = [scores[kvh][j * BLK:(j + 1) * BLK] for j in range(nk)]
                parts = [p if b is None else p + b for p, b in zip(parts, biases)]
                mx = jnp.max(functools.reduce(jnp.maximum, parts), axis=0, keepdims=True)
                m = jnp.maximum(mx, sink_rows[kvh])
                p = jnp.concatenate([jnp.exp2(p - m) for p in parts], axis=0).astype(BF16)
                oe = _dot(v_ext, p)
                denom = oe[HEAD_DIM:HEAD_DIM + 1] + jnp.exp2(sink_rows[kvh] - m)
                on = oe[0:HEAD_DIM] / denom
                for pair in range(Q_PER_KV // 2):
                    two = jnp.concatenate([on[:, (2 * pair) * BLK:(2 * pair + 1) * BLK],
                                           on[:, (2 * pair + 1) * BLK:(2 * pair + 2) * BLK]], axis=0)
                    outs.append(two.T)
            result = jnp.concatenate(outs, axis=1).astype(BF16)

            def commit():
                o_ref[pl.ds(q0, BLK), :] = result
            return commit

        return (scores if with_scores else None), weighted_values

    def attend(n, with_scores=True):
        if isinstance(n, int) and n == 0:
            return block(0, 0, [None, next_bias], with_scores)
        if isinstance(n, int) and n == nb - 1:
            return block(n * BLK, (n - 1) * BLK, [prev_bias, None], with_scores)
        return block(_aligned(n * BLK, BLK), _aligned((n - 1) * BLK, BLK), [prev_bias, None, next_bias], with_scores)

    return attend


def _log_sigmoid(x):
    return jnp.minimum(x, 0.0) - jnp.log(1.0 + jnp.exp(-jnp.abs(x)))


def _mlstm_chunks(qk_ref, vt_ref, mo_ref, mgt_ref, gb_ref, hg_ref, o_ref,
                  gt_s, b_s, w_s, mloc_s, tot_s, rcol_s, cloc_s, cprev_s, mprev_s, keep_s, add_s):
    S = qk_ref.shape[0]
    L = BLK
    nc = S // L
    ng = N_GATES
    sub = lax.broadcasted_iota(jnp.int32, (L, L), 0)
    lane = lax.broadcasted_iota(jnp.int32, (L, L), 1)

    for c in range(nc):
        gt_s[c * ng:(c + 1) * ng, :] = mgt_ref[:, c * L:(c + 1) * L] + gb_ref[...]
    gt = gt_s[...]
    row = lax.broadcasted_iota(jnp.int32, gt.shape, 0)
    gt = jnp.where(((row >> 2) & 1) == 1, _log_sigmoid(gt), gt)
    gt = gt * LOG2E
    upper = jnp.where(sub <= lane, 1.0, 0.0).astype(BF16)
    g1 = gt.astype(BF16)
    r1 = gt - g1.astype(F32)
    g2 = r1.astype(BF16)
    g3 = (r1 - g2.astype(F32)).astype(BF16)
    pre = _dot(g1, upper) + _dot(g2, upper) + _dot(g3, upper)
    tot = jnp.broadcast_to(pre[:, L - 1:L], pre.shape)
    bsum =jnp.where((row & (ng - 1)) < ng // 2, pre, tot - pre + gt)
    li = pltpu.roll(gt, M_HEADS, 0)
    a = tot - bsum + li
    mloc = jnp.broadcast_to(jnp.max(a, axis=1, keepdims=True), a.shape)
    b_s[...] = bsum
    w_s[...] = jnp.exp2(a - mloc)
    mloc_s[...] = mloc
    tot_s[...] = tot
    rdiff = li - bsum
    zpad = jnp.zeros((L - ng, L), F32)
    for c in range(nc):
        rcol_s[c * L:(c + 1) * L, :] = jnp.concatenate([rdiff[c * ng:(c + 1) * ng, :], zpad], axis=0).T

    def value_rows(hd, r0, dtype):
        vt = vt_ref[hd * HEAD_DIM:(hd + 1) * HEAD_DIM, pl.ds(r0, L)].astype(dtype)
        return jnp.concatenate([vt, jnp.ones((V_ROWS - HEAD_DIM, L), dtype)], axis=0)

    def local_state(c, carry):
        r0 = pl.multiple_of(c * L, L)
        g0 = pl.multiple_of(c * ng, ng)
        k = qk_ref[pl.ds(r0, L), M_W:2 * M_W]
        w = w_s[pl.ds(g0, ng), :]
        for hd in range(M_HEADS):
            vte = value_rows(hd, r0, F32)
            lhs = jnp.concatenate([vte * w[4 + hd:5 + hd, :], vte * w[12 + hd:13 + hd, :]], axis=0)
            cl = _dot(lhs.astype(BF16), k[:, (hd // 2) * LANES:(hd // 2 + 1) * LANES])
            cloc_s[0, hd, c] = cl[0:V_ROWS]
            cloc_s[1, hd, c] = cl[V_ROWS:2 * V_ROWS]
        return carry

    lax.fori_loop(0, nc, local_state, 0, unroll=2)

    jrow = lax.broadcasted_iota(jnp.int32, (ng, L), 0)

    def scan_stabilisers(i, m):
        cf = i
        cb = nc - 1 - i
        gf = pl.multiple_of(cf * ng, ng)
        gb = pl.multiple_of(cb * ng, ng)
        is_fwd = jrow < ng // 2
        blast = jnp.where(is_fwd, tot_s[pl.ds(gf, ng), :], tot_s[pl.ds(gb, ng), :])
        mloc_i = jnp.where(is_fwd, mloc_s[pl.ds(gf, ng), :], mloc_s[pl.ds(gb, ng), :])
        mprev_s[0, pl.ds(gf, ng), :] = m
        mprev_s[1, pl.ds(gb, ng), :] = m
        m_new = jnp.maximum(blast + m, mloc_i)
        gi = pl.multiple_of(i * ng, ng)
        keep_s[pl.ds(gi, ng), :] = jnp.exp2(blast + m - m_new)
        add_s[pl.ds(gi, ng), :] = jnp.exp2(mloc_i - m_new)
        return m_new

    lax.fori_loop(0, nc, scan_stabilisers, jnp.zeros((ng, L), F32))

    for hd in range(M_HEADS):
        def scan_states(i, states, hd=hd):
            gi = pl.multiple_of(i * ng, ng)
            keep = keep_s[pl.ds(gi, ng), :]
            add = add_s[pl.ds(gi, ng), :]
            new_states = []
            for d, ci in enumerate((i, nc - 1 - i)):
                r = 4 + hd + 8 * d
                cprev_s[d, hd, ci] = states[d].astype(BF16)
                new_states.append(keep[r:r + 1, :] * states[d] + add[r:r + 1, :] * cloc_s[d, hd, ci])
            return tuple(new_states)

        lax.fori_loop(0, nc, scan_states, (jnp.zeros((V_ROWS, LANES), F32), jnp.zeros((V_ROWS, LANES), F32)))

    causal = (sub <= lane, sub >= lane)

    def outputs(c, with_scores=True):
        r0 = _aligned(c * L, L)
        g0 = _aligned(c * ng, ng)
        q = qk_ref[pl.ds(r0, L), 0:M_W]
        k = qk_ref[pl.ds(r0, L), M_W:2 * M_W]
        bc = b_s[pl.ds(g0, ng), :]
        mprev = (mprev_s[0, pl.ds(g0, ng), :], mprev_s[1, pl.ds(g0, ng), :])
        rc = rcol_s[pl.ds(r0, L), :]
        ys = []
        for hd in range(M_HEADS if with_scores else 0):
            t = hd // 2
            qp = q[:, t * LANES:(t + 1) * LANES]
            qm = jnp.where((lane >> 6) == (hd & 1), qp, jnp.zeros_like(qp))
            x = jnp.concatenate([k[:, t * LANES:(t + 1) * LANES], cprev_s[0, hd, c], cprev_s[1, hd, c]], axis=0)
            ys.append(_dot_nt(x, qm))

        def combine(ys):
            houts = []
            for hd in range(M_HEADS):
                y = ys[hd]
                st = y[0:L]
                pts, scs, mts = [], [], []
                for d in range(2):
                    r = 4 + hd + 8 * d
                    b_row = bc[r:r + 1, :]
                    dm = jnp.where(causal[d], rc[:, r:r + 1] + b_row, NEG)
                    inter = b_row + mprev[d][r:r + 1, :]
                    m_t = jnp.maximum(inter, jnp.max(dm, axis=0, keepdims=True))
                    pts.append((jnp.exp2(dm - m_t) * st).astype(BF16))
                    scs.append(jnp.exp2(inter - m_t))
                    mts.append(m_t)
                n2 = _dot(value_rows(hd, r0, BF16), jnp.concatenate(pts, axis=1))
                ht = None
                for d in range(2):
                    tt = n2[:, d * L:(d + 1) * L] + scs[d] * y[L + d * V_ROWS:L + (d + 1) * V_ROWS]
                    den = tt[HEAD_DIM:HEAD_DIM + 1, :]
                    hd_out = tt[0:HEAD_DIM] / jnp.maximum(jnp.abs(den), jnp.exp2(-mts[d]))
                    ht = hd_out if ht is None else ht + hd_out
                ms = jnp.mean(ht * ht, axis=0, keepdims=True)
                houts.append(ht * lax.rsqrt(ms + EPS))
            hn = (jnp.concatenate(houts, axis=0) * hg_ref[...]).T
            result = (jax.nn.sigmoid(mo_ref[pl.ds(r0, L), :].astype(F32)) * hn).astype(BF16)

            def commit():
                o_ref[pl.ds(r0, L), :] = result
            return commit

        return (ys if with_scores else None), combine

    return outputs


def _mixers_kernel(sink_ref, q_ref, kd_ref, vat_ref, qk_ref, mvt_ref, mo_ref, mgt_ref, gb_ref, hg_ref,
                   ya_ref, ym_ref, *scratch):
    *mlstm_scratch, s_buf, y_buf = scratch
    nb = q_ref.shape[0] // BLK
    n_groups = (nb - 2) // 2
    attend = _attn_blocks(sink_ref, q_ref, kd_ref, vat_ref, ya_ref)
    outputs = _mlstm_chunks(qk_ref, mvt_ref, mo_ref, mgt_ref, gb_ref, hg_ref, ym_ref, *mlstm_scratch)
    group = lambda g: (1 + 2 * g, 2 + 2 * g)

    def first(blocks):
        return [f(n)[0] for n in blocks for f in (attend, outputs)]

    def second(blocks, results):
        finishes = [f(n, with_scores=False)[1] for n in blocks for f in (attend, outputs)]
        commits = [finish(r) for finish, r in zip(finishes, results)]
        for commit in commits:
            commit()

    def park(slot, results):
        for b in range(2):
            for i, s in enumerate(results[2 * b]):
                s_buf[slot, N_KV_HEADS * b + i, 0:s.shape[0], :] = s
            for i, y in enumerate(results[2 * b + 1]):
                y_buf[slot, M_HEADS * b + i] = y

    def fetch(slot, key_blocks):
        results = []
        for b in range(2):
            results.append([s_buf[slot, N_KV_HEADS * b + i, 0:key_blocks * BLK, :] for i in range(N_KV_HEADS)])
            results.append([y_buf[slot, M_HEADS * b + i] for i in range(M_HEADS)])
        return results

    edges = (0, nb - 1)

    def advance(g, slot):
        results = first(group(g))
        if isinstance(g, int) and g == 0:
            second(edges, fetch(1 - slot, 2))
        else:
            second(group(g - 1), fetch(1 - slot, 3))
        park(slot, results)

    def two_steps(j, carry):
        advance(1 + 2 * j, 0)
        advance(2 + 2 * j, 1)
        return carry

    park(0, first(edges))
    advance(0, 1)
    lax.fori_loop(0, (n_groups - 1) // 2, two_steps, 0)
    last_slot = 1
    if (n_groups - 1) % 2:
        advance(n_groups - 1, 0)
        last_slot = 0
    second(group(n_groups - 1), fetch(last_slot, 3))


def _mixers(qa, kd, vat, sink, mqk, mvt, mo, mgt, gate_b, head_g, B, S):
    seq = lambda w: pl.BlockSpec((S, w), lambda b: (b, 0))
    tseq = lambda r: pl.BlockSpec((r, S), lambda b: (0, b))
    nc = S // BLK
    assert nc % 2 == 0 and nc >= 4
    gb = jnp.broadcast_to(gate_b.reshape(N_GATES, 1), (N_GATES, BLK))
    hg_t = jnp.broadcast_to(head_g.reshape(M_W, 1), (M_W, BLK))
    rows = nc * N_GATES
    return pl.pallas_call(
        _mixers_kernel,
        grid=(B,),
        in_specs=[pl.BlockSpec(memory_space=pltpu.SMEM), seq(ATTN_Q_W), seq(2 * ATTN_KV_W), tseq(ATTN_KV_W),
                  seq(2 * M_W), tseq(M_W), seq(M_W), tseq(N_GATES), _const_spec((N_GATES, BLK)),
                  _const_spec((M_W, BLK))],
        out_specs=[seq(ATTN_Q_W), seq(M_W)],
        out_shape=[jax.ShapeDtypeStruct((B * S, ATTN_Q_W), BF16), jax.ShapeDtypeStruct((B * S, M_W), BF16)],
        scratch_shapes=[pltpu.VMEM((rows, BLK), F32),
                        pltpu.VMEM((rows, BLK), F32),
                        pltpu.VMEM((rows, BLK), F32),
                        pltpu.VMEM((rows, BLK), F32),
                        pltpu.VMEM((rows, BLK), F32),
                        pltpu.VMEM((S, LANES), F32),
                        pltpu.VMEM((2, M_HEADS, nc, V_ROWS, LANES), F32),
                        pltpu.VMEM((2, M_HEADS, nc, V_ROWS, LANES), BF16),
                        pltpu.VMEM((2, rows, BLK), F32),
                        pltpu.VMEM((rows, BLK), F32),
                        pltpu.VMEM((rows, BLK), F32),
                        pltpu.VMEM((2, 2 * N_KV_HEADS, 3 * BLK, Q_PER_KV * BLK), F32),
                        pltpu.VMEM((2, 2 * M_HEADS, BLK + 2 * V_ROWS, BLK), F32)],
        compiler_params=_params(1),
        name="mixers",
    )(sink, qa, kd, vat, mqk, mvt, mo, mgt, gb, hg_t)


def kernel(x, positions, norm_ffn1_g, ffn1_w_gate, ffn1_w_up, ffn1_w_down, norm_mix_g, w_in, q_norm_g, k_norm_g, attn_sink, mlstm_gate_b, mlstm_head_g, sgu_norm_g, sgu_w_s, sgu_b, w_out, norm_ffn2_g, ffn2_w_gate, ffn2_w_up, ffn2_w_down, norm_out_g):
    B, S, D = x.shape
    depth = w_in.shape[0]
    T = B * S
    xt = x.reshape(T, D)
    cos, sin = _rope_tables(positions)
    group = np.arange(M_W) // HEAD_DIM
    hmat = jnp.asarray((group[:, None] == group[None, :]) / HEAD_DIM, BF16)
    w_main = w_in[..., :MAIN_W].astype(BF16)
    w_gate = jnp.pad(w_in[..., MAIN_W:MAIN_W + N_GATES].astype(BF16), ((0, 0), (0, 0), (0, GATE_PAD - N_GATES)))
    w_sgu = w_in[..., MAIN_W + N_GATES:].astype(BF16)
    w_o = w_out.astype(BF16)
    for l in range(depth):
        xt = _ffn(xt, l, norm_ffn1_g[l], ffn1_w_gate, ffn1_w_up, ffn1_w_down)
        qa, kd, vat, mqk, mvt, mo, mgt, ys = _mix_in(xt, l, norm_mix_g[l], w_main, w_sgu, w_gate, q_norm_g[l],
                                                     k_norm_g[l], cos, sin, hmat, sgu_norm_g[l], sgu_w_s[l], sgu_b[l])
        ya, ym = _mixers(qa, kd, vat, attn_sink[l], mqk, mvt, mo, mgt, mlstm_gate_b[l], mlstm_head_g[l], B, S)
        xt = _ffn(xt, l, norm_ffn2_g[l], ffn2_w_gate, ffn2_w_up, ffn2_w_down, proj=(ya, ym, ys, w_o),
                  gout=norm_out_g[l])
    return xt.reshape(B, S, D)
```

```python
import functools

import numpy as np
import jax
import jax.numpy as jnp
from jax import lax
from jax.experimental import pallas as pl
from jax.experimental.pallas import tpu as pltpu

F32 = jnp.float32
BF16 = jnp.bfloat16

D_MODEL = 1024
HEAD_DIM = 64
N_Q_HEADS = 8
N_KV_HEADS = 2
Q_PER_KV = N_Q_HEADS // N_KV_HEADS
ATTN_Q_W = N_Q_HEADS * HEAD_DIM
ATTN_KV_W = N_KV_HEADS * HEAD_DIM
WINDOW = 128
BLK = 128
ROPE_THETA = 10000.0
M_HEADS = 4
M_W = M_HEADS * HEAD_DIM
N_GATES = 4 * M_HEADS
V_ROWS = HEAD_DIM + 16
SGU_GROUPS = 4
SGU_W = SGU_GROUPS * HEAD_DIM
D_FF = 2816
EPS = 1e-6
LANES = 128
GATE_PAD = LANES
MAIN_W = ATTN_Q_W + 2 * ATTN_KV_W + 4 * M_W
NEG = -1e30
LOG2E = float(np.log2(np.e))

VMEM_LIMIT = 56 * 1024 * 1024
FFN_TM = 512
FFN_SPLIT = 2
W_CHUNKS = 8
W_SLOTS = 3
MIX_TM = 512
MIX_SPLIT = 1


def _dot(a, b):
    return jnp.dot(a, b, preferred_element_type=F32)


def _dot_nt(a, b):
    return lax.dot_general(a, b, (((1,), (1,)), ((), ())), preferred_element_type=F32)


def _dot_tn(a, b):
    return lax.dot_general(a, b, (((0,), (0,)), ((), ())), preferred_element_type=F32)


def _aligned(x, m):
    return x if isinstance(x, int) else pl.multiple_of(x, m)


def _rms(x, g):
    ms = jnp.mean(x * x, axis=-1, keepdims=True)
    return x * lax.rsqrt(ms + EPS) * g


def _group_mean_sq(t, hmat):
    return _dot((t * t).astype(BF16), hmat)


def _const_spec(shape):
    nd = len(shape)
    return pl.BlockSpec(shape, lambda *_: (0,) * nd, pipeline_mode=pl.Buffered(1))


def _layer_spec(stacked_shape, layer):
    nd = len(stacked_shape) - 1
    return pl.BlockSpec((None,) + tuple(stacked_shape[1:]), lambda *_: (layer,) + (0,) * nd,
                        pipeline_mode=pl.Buffered(1))


def _params(n_grid):
    return pltpu.CompilerParams(dimension_semantics=("parallel",) * n_grid,
                                vmem_limit_bytes=VMEM_LIMIT)


ROPE_HALF = HEAD_DIM // 2
ROPE_PACK = LANES // ROPE_HALF
ROPE_STEPS = 8


def _rope_kernel(pos_ref, freq_ref, cos_ref, sin_ref):
    ang = pos_ref[...] * freq_ref[...]
    rows = ang.shape[0]
    token = lax.broadcasted_iota(jnp.int32, ang.shape, 1) >> 5
    for table, out_ref in ((jnp.cos(ang), cos_ref), (jnp.sin(ang), sin_ref)):
        for i in range(ROPE_PACK):
            x = jnp.where(token == i, table, 0.0)
            y = x + pltpu.roll(x, 2 * ROPE_HALF, 1)
            out_ref[pl.ds(i, rows, stride=ROPE_PACK), :] = y + pltpu.roll(y, ROPE_HALF, 1)


def _rope_tables(positions):
    T = positions.size
    freqs = ROPE_THETA ** (-jnp.arange(0, HEAD_DIM, 2, dtype=F32) / HEAD_DIM)
    pos = jnp.repeat(positions.reshape(T // ROPE_PACK, ROPE_PACK).astype(F32), ROPE_HALF, axis=1)
    freq_row = jnp.tile(freqs, ROPE_PACK).reshape(1, LANES)
    rows = T // ROPE_PACK // ROPE_STEPS
    return pl.pallas_call(
        _rope_kernel,
        grid=(ROPE_STEPS,),
        in_specs=[pl.BlockSpec((rows, LANES), lambda i: (i, 0)),
                  pl.BlockSpec((1, LANES), lambda i: (0, 0))],
        out_specs=[pl.BlockSpec((rows * ROPE_PACK, LANES), lambda i: (i, 0))] * 2,
        out_shape=[jax.ShapeDtypeStruct((T, LANES), F32)] * 2,
        compiler_params=_params(1),
        name="rope_tables",
    )(pos, freq_row)


def _load_weight_bf16(w_hbm, layer, dst, stage, sems):
    n_slots, rows, cols = stage.shape
    n_chunks = dst.shape[0] // rows

    def copy(c):
        slot = c % n_slots
        return pltpu.make_async_copy(w_hbm.at[layer, pl.ds(c * rows, rows), pl.ds(0, cols)], stage.at[slot],
                                     sems.at[slot])

    for c in range(min(n_slots - 1, n_chunks)):
        copy(c).start()
    for c in range(n_chunks):
        if c + n_slots - 1 < n_chunks:
            copy(c + n_slots - 1).start()
        copy(c).wait()
        dst[c * rows:(c + 1) * rows, :] = stage[c % n_slots].astype(BF16)


def _ffn_kernel(*refs, layer, has_proj, has_final):
    refs = list(refs)
    x_ref = refs.pop(0)
    if has_proj:
        ya_ref, ym_ref, ys_ref, wo_ref = refs[:4]
        refs = refs[4:]
    g_ref, wg_hbm, wu_hbm, wd_hbm = refs[:4]
    refs = refs[4:]
    if has_final:
        gout_ref = refs.pop(0)
    out_ref, wg_ref, wu_ref, wd_ref, stage_in, stage_out, sems = refs

    @pl.when(pl.program_id(0) == 0)
    def _():
        _load_weight_bf16(wg_hbm, layer, wg_ref, stage_in, sems)
        _load_weight_bf16(wu_hbm, layer, wu_ref, stage_in, sems)
        _load_weight_bf16(wd_hbm, layer, wd_ref, stage_out, sems)

    hs = x_ref.shape[0] // FFN_SPLIT
    rows = [slice(i * hs, (i + 1) * hs) for i in range(FFN_SPLIT)]
    xs = []
    for r in rows:
        x = x_ref[r, :]
        if has_proj:
            x = (x + _dot(ya_ref[r, :], wo_ref[0:ATTN_Q_W, :])
                 + _dot(ym_ref[r, :], wo_ref[ATTN_Q_W:ATTN_Q_W + M_W, :])
                 + _dot(ys_ref[r, :], wo_ref[ATTN_Q_W + M_W:, :]))
        xs.append(x)
    hidden = [_rms(x, g_ref[...]).astype(BF16) for x in xs]
    gate_up = [(_dot(h, wg_ref[...]), _dot(h, wu_ref[...])) for h in hidden]
    for r, x, (gate, up) in zip(rows, xs, gate_up):
        act = (gate * jax.nn.sigmoid(gate) * up).astype(BF16)
        y = x + 0.5 * _dot(act, wd_ref[...])
        if has_final:
            y = _rms(y, gout_ref[...])
        out_ref[r, :] = y


def _ffn(x, layer, g, wg, wu, wd, proj=None, gout=None, tm=FFN_TM):
    T = x.shape[0]
    d_model, d_ff = wg.shape[1:]
    row = lambda w: pl.BlockSpec((tm, w), lambda i: (i, 0))
    hbm = pl.BlockSpec(memory_space=pl.ANY)
    args, specs = [x], [row(D_MODEL)]
    if proj is not None:
        ya, ym, ys, wo = proj
        args += [ya, ym, ys, wo]
        specs += [row(ATTN_Q_W), row(M_W), row(SGU_W), _layer_spec(wo.shape, layer)]
    args += [g.reshape(1, D_MODEL), wg, wu, wd]
    specs += [_const_spec((1, D_MODEL)), hbm, hbm, hbm]
    if gout is not None:
        args.append(gout.reshape(1, D_MODEL))
        specs.append(_const_spec((1, D_MODEL)))
    return pl.pallas_call(
        functools.partial(_ffn_kernel, layer=layer, has_proj=proj is not None, has_final=gout is not None),
        grid=(T // tm,),
        in_specs=specs,
        out_specs=row(D_MODEL),
        out_shape=jax.ShapeDtypeStruct((T, D_MODEL), F32),
        scratch_shapes=[pltpu.VMEM((d_model, d_ff), BF16), pltpu.VMEM((d_model, d_ff), BF16),
                        pltpu.VMEM((d_ff, d_model), BF16),
                        pltpu.VMEM((W_SLOTS, d_model // W_CHUNKS, d_ff), F32),
                        pltpu.VMEM((W_SLOTS, d_ff // W_CHUNKS, d_model), F32),
                        pltpu.SemaphoreType.DMA((W_SLOTS,))],
        compiler_params=pltpu.CompilerParams(dimension_semantics=("arbitrary",), vmem_limit_bytes=VMEM_LIMIT),
        name="ffn",
    )(*args)


def _first_half(shape):
    lane = lax.broadcasted_iota(jnp.int32, shape, 1)
    return (lane & (HEAD_DIM - 1)) < HEAD_DIM // 2


def _rope(t, cos, sin_signed):
    n = t.shape[-1]
    reps = n // LANES
    swapped = jnp.where(_first_half(t.shape),
                        pltpu.roll(t, n - HEAD_DIM // 2, 1), pltpu.roll(t, HEAD_DIM // 2, 1))
    return t * jnp.tile(cos, (1, reps)) + swapped * jnp.tile(sin_signed, (1, reps))


def _gelu(x):
    return 0.5 * x * (1.0 + lax.erf(x * np.float32(np.sqrt(0.5))))


def _mixin_kernel(x_ref, g_ref, w_hbm, wsg_ref, wgt_ref, qg_ref, kg_ref, cos_ref, sin_ref, hmat_ref, ng_ref, ws_ref,
                  bs_ref, qa_ref, kd_ref, vat_ref, mqk_ref, mvt_ref, mo_ref, mgt_ref, ys_ref, w_ref, stage, sems,
                  *, layer):
    @pl.when(pl.program_id(0) == 0)
    def _():
        _load_weight_bf16(w_hbm, layer, w_ref, stage, sems)

    scale = HEAD_DIM ** -0.5
    hmat = hmat_ref[...]
    gw = hmat.shape[0]
    c_q, c_kv, c_mq, c_mk, c_mv, c_mo = (int(c) for c in np.cumsum([0, ATTN_Q_W, 2 * ATTN_KV_W, M_W, M_W, M_W]))
    lane_group = lax.broadcasted_iota(jnp.int32, (BLK, SGU_W), 1) >> 6

    def projections(r0, nrows):
        r = slice(r0, r0 + nrows)
        h = _rms(x_ref[r, :], g_ref[...]).astype(BF16)
        proj = lambda ref, start, width: _dot(h, ref[:, start:start + width])
        su = proj(wsg_ref, 0, SGU_W)
        sv = proj(wsg_ref, SGU_W, SGU_W)
        aq = proj(w_ref, c_q, ATTN_Q_W)
        akv = proj(w_ref, c_kv, 2 * ATTN_KV_W)
        ak, av = akv[:, :ATTN_KV_W], akv[:, ATTN_KV_W:]
        v = _gelu(sv)
        v_ms = _group_mean_sq(v, hmat)
        mqk_ref[r, 0:M_W] = proj(w_ref, c_mq, M_W).astype(BF16)
        mqk_ref[r, M_W:2 * M_W] = (proj(w_ref, c_mk, M_W) * scale).astype(BF16)
        q_ms = jnp.concatenate([_group_mean_sq(aq[:, i:i + gw], hmat) for i in range(0, ATTN_Q_W, gw)], axis=1)
        k_ms = _group_mean_sq(ak, hmat[0:ATTN_KV_W, 0:ATTN_KV_W])
        mvt_ref[:, r] = proj(w_ref, c_mv, M_W).T.astype(BF16)
        mo_ref[r, :] = proj(w_ref, c_mo, M_W).astype(BF16)
        mgt_ref[:, r] = proj(wgt_ref, 0, GATE_PAD).T[0:N_GATES, :]

        def tail():
            cos = cos_ref[r, :]
            sin_signed = jnp.where(_first_half(cos.shape), -sin_ref[r, :], sin_ref[r, :])
            qn = aq * lax.rsqrt(q_ms + EPS) * qg_ref[...]
            qa_ref[r, :] = (_rope(qn, cos, sin_signed) * (scale * LOG2E)).astype(BF16)
            kn = ak * lax.rsqrt(k_ms + EPS) * kg_ref[...]
            kr = _rope(kn, cos, sin_signed)
            ks = pltpu.roll(kr, HEAD_DIM, 1)
            low = lax.broadcasted_iota(jnp.int32, kr.shape, 1) < HEAD_DIM
            kd_ref[r, 0:LANES] = jnp.where(low, kr, ks).astype(BF16)
            kd_ref[r, LANES:2 * LANES] = jnp.where(low, ks, kr).astype(BF16)
            vat_ref[:, r] = av.T.astype(BF16)
            u = _gelu(su)
            vn = (v * lax.rsqrt(v_ms + EPS) * ng_ref[...]).astype(BF16)
            for c in range(nrows // BLK):
                vc = vn[c * BLK:(c + 1) * BLK]
                stack = jnp.concatenate(
                    [jnp.where(lane_group == g, vc, jnp.zeros_like(vc)) for g in range(SGU_GROUPS)], axis=0)
                mixed = _dot(ws_ref[...], stack)
                ys_ref[r0 + c * BLK:r0 + (c + 1) * BLK, :] = (
                    u[c * BLK:(c + 1) * BLK] * (mixed + bs_ref[...])).astype(BF16)

        return tail

    nrows = x_ref.shape[0] // MIX_SPLIT
    tails = [projections(i * nrows, nrows) for i in range(MIX_SPLIT)]
    for tail in tails:
        tail()


def _mix_in(x, layer, g, w_in, w_sgu, w_gate, q_g, k_g, cos, sin, hmat, sgu_g, sgu_w, sgu_b, tm=MIX_TM):
    T = x.shape[0]
    d_model = w_in.shape[1]
    row = lambda w: pl.BlockSpec((tm, w), lambda i: (i, 0))
    tcol = lambda r: pl.BlockSpec((r, tm), lambda i: (0, i))
    out_specs = [row(ATTN_Q_W), row(2 * ATTN_KV_W), tcol(ATTN_KV_W), row(2 * M_W), tcol(M_W), row(M_W),
                 tcol(N_GATES), row(SGU_W)]
    out_shape = [jax.ShapeDtypeStruct(s, d) for s, d in [
        ((T, ATTN_Q_W), BF16), ((T, 2 * ATTN_KV_W), BF16), ((ATTN_KV_W, T), BF16), ((T, 2 * M_W), BF16),
        ((M_W, T), BF16), ((T, M_W), BF16), ((N_GATES, T), F32), ((T, SGU_W), BF16)]]
    ws_cat = jnp.transpose(sgu_w, (1, 0, 2)).reshape(BLK, SGU_GROUPS * BLK).astype(BF16)
    bs_full = jnp.repeat(jnp.transpose(sgu_b), HEAD_DIM, axis=1)
    return pl.pallas_call(
        functools.partial(_mixin_kernel, layer=layer),
        grid=(T // tm,),
        in_specs=[row(D_MODEL), _const_spec((1, D_MODEL)), pl.BlockSpec(memory_space=pl.ANY),
                  _layer_spec(w_sgu.shape, layer), _layer_spec(w_gate.shape, layer),
                  _const_spec((1, ATTN_Q_W)), _const_spec((1, ATTN_KV_W)),
                  row(LANES), row(LANES), _const_spec(hmat.shape),
                  _const_spec((1, SGU_W)), _const_spec(ws_cat.shape), _const_spec(bs_full.shape)],
        out_specs=out_specs,
        out_shape=out_shape,
        scratch_shapes=[pltpu.VMEM((d_model, MAIN_W), BF16),
                        pltpu.VMEM((W_SLOTS, d_model // W_CHUNKS, MAIN_W), F32),
                        pltpu.SemaphoreType.DMA((W_SLOTS,))],
        compiler_params=pltpu.CompilerParams(dimension_semantics=("arbitrary",), vmem_limit_bytes=VMEM_LIMIT),
        name="mix_in",
    )(x, g.reshape(1, D_MODEL), w_in, w_sgu, w_gate,
      jnp.tile(q_g, N_Q_HEADS).reshape(1, ATTN_Q_W), jnp.tile(k_g, N_KV_HEADS).reshape(1, ATTN_KV_W),
      cos, sin, hmat, sgu_g.reshape(1, SGU_W), ws_cat, bs_full)


def _attn_blocks(sink_ref, q_ref, kd_ref, vt_ref, o_ref):
    S = q_ref.shape[0]
    nb = S // BLK
    kc = lax.broadcasted_iota(jnp.int32, (BLK, BLK), 0)
    qi = lax.broadcasted_iota(jnp.int32, (BLK, BLK), 1)
    prev_bias = jnp.tile(jnp.where(kc >= qi, 0.0, NEG), (1, Q_PER_KV))
    next_bias = jnp.tile(jnp.where(kc <= qi, 0.0, NEG), (1, Q_PER_KV))
    half_mask = (qi < HEAD_DIM, qi >= HEAD_DIM)
    head_lane = lax.broadcasted_iota(jnp.int32, (1, Q_PER_KV * BLK), 1) >> 7
    sink_rows = []
    for kvh in range(N_KV_HEADS):
        row = jnp.zeros((1, Q_PER_KV * BLK), F32)
        for g in range(Q_PER_KV):
            row = jnp.where(head_lane == g, sink_ref[kvh * Q_PER_KV + g] * LOG2E, row)
        sink_rows.append(row)

    def block(q0, start, biases, with_scores):
        nk = len(biases)
        scores = []
        for kvh in range(N_KV_HEADS if with_scores else 0):
            kd = kd_ref[pl.ds(start, nk * BLK), kvh * LANES:(kvh + 1) * LANES]
            qs = []
            for g in range(Q_PER_KV):
                t = kvh * Q_PER_KV + g
                qt = q_ref[pl.ds(q0, BLK), (t // 2) * LANES:(t // 2 + 1) * LANES]
                qs.append(jnp.where(half_mask[t & 1], qt, jnp.zeros_like(qt)))
            scores.append(_dot_nt(kd, jnp.concatenate(qs, axis=0)))

        def weighted_values(scores):
            ones = jnp.ones((HEAD_DIM, nk * BLK), BF16)
            outs = []
            for kvh in range(N_KV_HEADS):
                v_ext = jnp.concatenate([vt_ref[kvh * HEAD_DIM:(kvh + 1) * HEAD_DIM, pl.ds(start, nk * BLK)], ones],
                                        axis=0)
                parts = [scores[kvh][j * BLK:(j + 1) * BLK] for j in range(nk)]
                parts = [p if b is None else p + b for p, b in zip(parts, biases)]
                mx = jnp.max(functools.reduce(jnp.maximum, parts), axis=0, keepdims=True)
                m = jnp.maximum(mx, sink_rows[kvh])
                p = jnp.concatenate([jnp.exp2(p - m) for p in parts], axis=0).astype(BF16)
                oe = _dot(v_ext, p)
                denom = oe[HEAD_DIM:HEAD_DIM + 1] + jnp.exp2(sink_rows[kvh] - m)
                on = oe[0:HEAD_DIM] / denom
                for pair in range(Q_PER_KV // 2):
                    two = jnp.concatenate([on[:, (2 * pair) * BLK:(2 * pair + 1) * BLK],
                                           on[:, (2 * pair + 1) * BLK:(2 * pair + 2) * BLK]], axis=0)
                    outs.append(two.T)
            result = jnp.concatenate(outs, axis=1).astype(BF16)

            def commit():
                o_ref[pl.ds(q0, BLK), :] = result
            return commit

        return (scores if with_scores else None), weighted_values

    def attend(n, with_scores=True):
        if isinstance(n, int) and n == 0:
            return block(0, 0, [None, next_bias], with_scores)
        if isinstance(n, int) and n == nb - 1:
            return block(n * BLK, (n - 1) * BLK, [prev_bias, None], with_scores)
        return block(_aligned(n * BLK, BLK), _aligned((n - 1) * BLK, BLK), [prev_bias, None, next_bias], with_scores)

    return attend


def _log_sigmoid(x):
    return jnp.minimum(x, 0.0) - jnp.log(1.0 + jnp.exp(-jnp.abs(x)))


def _mlstm_chunks(qk_ref, vt_ref, mo_ref, mgt_ref, gb_ref, hg_ref, o_ref,
                  gt_s, b_s, w_s, mloc_s, tot_s, rcol_s, cloc_s, cprev_s, mprev_s, keep_s, add_s):
    S = qk_ref.shape[0]
    L = BLK
    nc = S // L
    ng = N_GATES
    sub = lax.broadcasted_iota(jnp.int32, (L, L), 0)
    lane = lax.broadcasted_iota(jnp.int32, (L, L), 1)

    for c in range(nc):
        gt_s[c * ng:(c + 1) * ng, :] = mgt_ref[:, c * L:(c + 1) * L] + gb_ref[...]
    gt = gt_s[...]
    row = lax.broadcasted_iota(jnp.int32, gt.shape, 0)
    gt = jnp.where(((row >> 2) & 1) == 1, _log_sigmoid(gt), gt)
    gt = gt * LOG2E
    upper = jnp.where(sub <= lane, 1.0, 0.0).astype(BF16)
    g1 = gt.astype(BF16)
    r1 = gt - g1.astype(F32)
    g2 = r1.astype(BF16)
    g3 = (r1 - g2.astype(F32)).astype(BF16)
    pre = _dot(g1, upper) + _dot(g2, upper) + _dot(g3, upper)
    tot = jnp.broadcast_to(pre[:, L - 1:L], pre.shape)
    bsum =jnp.where((row & (ng - 1)) < ng // 2, pre, tot - pre + gt)
    li = pltpu.roll(gt, M_HEADS, 0)
    a = tot - bsum + li
    mloc = jnp.broadcast_to(jnp.max(a, axis=1, keepdims=True), a.shape)
    b_s[...] = bsum
    w_s[...] = jnp.exp2(a - mloc)
    mloc_s[...] = mloc
    tot_s[...] = tot
    rdiff = li - bsum
    zpad = jnp.zeros((L - ng, L), F32)
    for c in range(nc):
        rcol_s[c * L:(c + 1) * L, :] = jnp.concatenate([rdiff[c * ng:(c + 1) * ng, :], zpad], axis=0).T

    def value_rows(hd, r0, dtype):
        vt = vt_ref[hd * HEAD_DIM:(hd + 1) * HEAD_DIM, pl.ds(r0, L)].astype(dtype)
        return jnp.concatenate([vt, jnp.ones((V_ROWS - HEAD_DIM, L), dtype)], axis=0)

    def local_state(c, carry):
        r0 = pl.multiple_of(c * L, L)
        g0 = pl.multiple_of(c * ng, ng)
        k = qk_ref[pl.ds(r0, L), M_W:2 * M_W]
        w = w_s[pl.ds(g0, ng), :]
        for hd in range(M_HEADS):
            vte = value_rows(hd, r0, F32)
            lhs = jnp.concatenate([vte * w[4 + hd:5 + hd, :], vte * w[12 + hd:13 + hd, :]], axis=0)
            cl = _dot(lhs.astype(BF16), k[:, (hd // 2) * LANES:(hd // 2 + 1) * LANES])
            cloc_s[0, hd, c] = cl[0:V_ROWS]
            cloc_s[1, hd, c] = cl[V_ROWS:2 * V_ROWS]
        return carry

    lax.fori_loop(0, nc, local_state, 0, unroll=2)

    jrow = lax.broadcasted_iota(jnp.int32, (ng, L), 0)

    def scan_stabilisers(i, m):
        cf = i
        cb = nc - 1 - i
        gf = pl.multiple_of(cf * ng, ng)
        gb = pl.multiple_of(cb * ng, ng)
        is_fwd = jrow < ng // 2
        blast = jnp.where(is_fwd, tot_s[pl.ds(gf, ng), :], tot_s[pl.ds(gb, ng), :])
        mloc_i = jnp.where(is_fwd, mloc_s[pl.ds(gf, ng), :], mloc_s[pl.ds(gb, ng), :])
        mprev_s[0, pl.ds(gf, ng), :] = m
        mprev_s[1, pl.ds(gb, ng), :] = m
        m_new = jnp.maximum(blast + m, mloc_i)
        gi = pl.multiple_of(i * ng, ng)
        keep_s[pl.ds(gi, ng), :] = jnp.exp2(blast + m - m_new)
        add_s[pl.ds(gi, ng), :] = jnp.exp2(mloc_i - m_new)
        return m_new

    lax.fori_loop(0, nc, scan_stabilisers, jnp.zeros((ng, L), F32))

    for hd in range(M_HEADS):
        def scan_states(i, states, hd=hd):
            gi = pl.multiple_of(i * ng, ng)
            keep = keep_s[pl.ds(gi, ng), :]
            add = add_s[pl.ds(gi, ng), :]
            new_states = []
            for d, ci in enumerate((i, nc - 1 - i)):
                r = 4 + hd + 8 * d
                cprev_s[d, hd, ci] = states[d].astype(BF16)
                new_states.append(keep[r:r + 1, :] * states[d] + add[r:r + 1, :] * cloc_s[d, hd, ci])
            return tuple(new_states)

        lax.fori_loop(0, nc, scan_states, (jnp.zeros((V_ROWS, LANES), F32), jnp.zeros((V_ROWS, LANES), F32)))

    causal = (sub <= lane, sub >= lane)

    def outputs(c, with_scores=True):
        r0 = _aligned(c * L, L)
        g0 = _aligned(c * ng, ng)
        q = qk_ref[pl.ds(r0, L), 0:M_W]
        k = qk_ref[pl.ds(r0, L), M_W:2 * M_W]
        bc = b_s[pl.ds(g0, ng), :]
        mprev = (mprev_s[0, pl.ds(g0, ng), :], mprev_s[1, pl.ds(g0, ng), :])
        rc = rcol_s[pl.ds(r0, L), :]
        ys = []
        for hd in range(M_HEADS if with_scores else 0):
            t = hd // 2
            qp = q[:, t * LANES:(t + 1) * LANES]
            qm = jnp.where((lane >> 6) == (hd & 1), qp, jnp.zeros_like(qp))
            x = jnp.concatenate([k[:, t * LANES:(t + 1) * LANES], cprev_s[0, hd, c], cprev_s[1, hd, c]], axis=0)
            ys.append(_dot_nt(x, qm))

        def combine(ys):
            houts = []
            for hd in range(M_HEADS):
                y = ys[hd]
                st = y[0:L]
                pts, scs, mts = [], [], []
                for d in range(2):
                    r = 4 + hd + 8 * d
                    b_row = bc[r:r + 1, :]
                    dm = jnp.where(causal[d], rc[:, r:r + 1] + b_row, NEG)
                    inter = b_row + mprev[d][r:r + 1, :]
                    m_t = jnp.maximum(inter, jnp.max(dm, axis=0, keepdims=True))
                    pts.append((jnp.exp2(dm - m_t) * st).astype(BF16))
                    scs.append(jnp.exp2(inter - m_t))
                    mts.append(m_t)
                n2 = _dot(value_rows(hd, r0, BF16), jnp.concatenate(pts, axis=1))
                ht = None
                for d in range(2):
                    tt = n2[:, d * L:(d + 1) * L] + scs[d] * y[L + d * V_ROWS:L + (d + 1) * V_ROWS]
                    den = tt[HEAD_DIM:HEAD_DIM + 1, :]
                    hd_out = tt[0:HEAD_DIM] / jnp.maximum(jnp.abs(den), jnp.exp2(-mts[d]))
                    ht = hd_out if ht is None else ht + hd_out
                ms = jnp.mean(ht * ht, axis=0, keepdims=True)
                houts.append(ht * lax.rsqrt(ms + EPS))
            hn = (jnp.concatenate(houts, axis=0) * hg_ref[...]).T
            result = (jax.nn.sigmoid(mo_ref[pl.ds(r0, L), :].astype(F32)) * hn).astype(BF16)

            def commit():
                o_ref[pl.ds(r0, L), :] = result
            return commit

        return (ys if with_scores else None), combine

    return outputs


def _mixers_kernel(sink_ref, q_ref, kd_ref, vat_ref, qk_ref, mvt_ref, mo_ref, mgt_ref, gb_ref, hg_ref,
                   ya_ref, ym_ref, *scratch):
    *mlstm_scratch, s_buf, y_buf = scratch
    nb = q_ref.shape[0] // BLK
    n_groups = (nb - 2) // 2
    attend = _attn_blocks(sink_ref, q_ref, kd_ref, vat_ref, ya_ref)
    outputs = _mlstm_chunks(qk_ref, mvt_ref, mo_ref, mgt_ref, gb_ref, hg_ref, ym_ref, *mlstm_scratch)
    group = lambda g: (1 + 2 * g, 2 + 2 * g)

    def first(blocks):
        return [f(n)[0] for n in blocks for f in (attend, outputs)]

    def second(blocks, results):
        finishes = [f(n, with_scores=False)[1] for n in blocks for f in (attend, outputs)]
        commits = [finish(r) for finish, r in zip(finishes, results)]
        for commit in commits:
            commit()

    def park(slot, results):
        for b in range(2):
            for i, s in enumerate(results[2 * b]):
                s_buf[slot, N_KV_HEADS * b + i, 0:s.shape[0], :] = s
            for i, y in enumerate(results[2 * b + 1]):
                y_buf[slot, M_HEADS * b + i] = y

    def fetch(slot, key_blocks):
        results = []
        for b in range(2):
            results.append([s_buf[slot, N_KV_HEADS * b + i, 0:key_blocks * BLK, :] for i in range(N_KV_HEADS)])
            results.append([y_buf[slot, M_HEADS * b + i] for i in range(M_HEADS)])
        return results

    edges = (0, nb - 1)

    def advance(g, slot):
        results = first(group(g))
        if isinstance(g, int) and g == 0:
            second(edges, fetch(1 - slot, 2))
        else:
            second(group(g - 1), fetch(1 - slot, 3))
        park(slot, results)

    def two_steps(j, carry):
        advance(1 + 2 * j, 0)
        advance(2 + 2 * j, 1)
        return carry

    park(0, first(edges))
    advance(0, 1)
    lax.fori_loop(0, (n_groups - 1) // 2, two_steps, 0)
    last_slot = 1
    if (n_groups - 1) % 2:
        advance(n_groups - 1, 0)
        last_slot = 0
    second(group(n_groups - 1), fetch(last_slot, 3))


def _mixers(qa, kd, vat, sink, mqk, mvt, mo, mgt, gate_b, head_g, B, S):
    seq = lambda w: pl.BlockSpec((S, w), lambda b: (b, 0))
    tseq = lambda r: pl.BlockSpec((r, S), lambda b: (0, b))
    nc = S // BLK
    assert nc % 2 == 0 and nc >= 4
    gb = jnp.broadcast_to(gate_b.reshape(N_GATES, 1), (N_GATES, BLK))
    hg_t = jnp.broadcast_to(head_g.reshape(M_W, 1), (M_W, BLK))
    rows = nc * N_GATES
    return pl.pallas_call(
        _mixers_kernel,
        grid=(B,),
        in_specs=[pl.BlockSpec(memory_space=pltpu.SMEM), seq(ATTN_Q_W), seq(2 * ATTN_KV_W), tseq(ATTN_KV_W),
                  seq(2 * M_W), tseq(M_W), seq(M_W), tseq(N_GATES), _const_spec((N_GATES, BLK)),
                  _const_spec((M_W, BLK))],
        out_specs=[seq(ATTN_Q_W), seq(M_W)],
        out_shape=[jax.ShapeDtypeStruct((B * S, ATTN_Q_W), BF16), jax.ShapeDtypeStruct((B * S, M_W), BF16)],
        scratch_shapes=[pltpu.VMEM((rows, BLK), F32),
                        pltpu.VMEM((rows, BLK), F32),
                        pltpu.VMEM((rows, BLK), F32),
                        pltpu.VMEM((rows, BLK), F32),
                        pltpu.VMEM((rows, BLK), F32),
                        pltpu.VMEM((S, LANES), F32),
                        pltpu.VMEM((2, M_HEADS, nc, V_ROWS, LANES), F32),
                        pltpu.VMEM((2, M_HEADS, nc, V_ROWS, LANES), BF16),
                        pltpu.VMEM((2, rows, BLK), F32),
                        pltpu.VMEM((rows, BLK), F32),
                        pltpu.VMEM((rows, BLK), F32),
                        pltpu.VMEM((2, 2 * N_KV_HEADS, 3 * BLK, Q_PER_KV * BLK), F32),
                        pltpu.VMEM((2, 2 * M_HEADS, BLK + 2 * V_ROWS, BLK), F32)],
        compiler_params=_params(1),
        name="mixers",
    )(sink, qa, kd, vat, mqk, mvt, mo, mgt, gb, hg_t)


def kernel(x, positions, norm_ffn1_g, ffn1_w_gate, ffn1_w_up, ffn1_w_down, norm_mix_g, w_in, q_norm_g, k_norm_g, attn_sink, mlstm_gate_b, mlstm_head_g, sgu_norm_g, sgu_w_s, sgu_b, w_out, norm_ffn2_g, ffn2_w_gate, ffn2_w_up, ffn2_w_down, norm_out_g):
    B, S, D = x.shape
    depth = w_in.shape[0]
    T = B * S
    xt = x.reshape(T, D)
    cos, sin = _rope_tables(positions)
    group = np.arange(M_W) // HEAD_DIM
    hmat = jnp.asarray((group[:, None] == group[None, :]) / HEAD_DIM, BF16)
    w_gate = jnp.pad(w_in[..., MAIN_W:MAIN_W + N_GATES].astype(BF16), ((0, 0), (0, 0), (0, GATE_PAD - N_GATES)))
    w_sgu = w_in[..., MAIN_W + N_GATES:].astype(BF16)
    w_o = w_out.astype(BF16)
    for l in range(depth):
        xt = _ffn(xt, l, norm_ffn1_g[l], ffn1_w_gate, ffn1_w_up, ffn1_w_down)
        qa, kd, vat, mqk, mvt, mo, mgt, ys = _mix_in(xt, l, norm_mix_g[l], w_in, w_sgu, w_gate, q_norm_g[l],
                                                     k_norm_g[l], cos, sin, hmat, sgu_norm_g[l], sgu_w_s[l], sgu_b[l])
        ya, ym = _mixers(qa, kd, vat, attn_sink[l], mqk, mvt, mo, mgt, mlstm_gate_b[l], mlstm_head_g[l], B, S)
        xt = _ffn(xt, l, norm_ffn2_g[l], ffn2_w_gate, ffn2_w_up, ffn2_w_down, proj=(ya, ym, ys, w_o),
                  gout=norm_out_g[l])
    return xt.reshape(B, S, D)
```

```python
import functools

import numpy as np
import jax
import jax.numpy as jnp
from jax import lax
from jax.experimental import pallas as pl
from jax.experimental.pallas import tpu as pltpu

F32 = jnp.float32
BF16 = jnp.bfloat16

D_MODEL = 1024
HEAD_DIM = 64
N_Q_HEADS = 8
N_KV_HEADS = 2
Q_PER_KV = N_Q_HEADS // N_KV_HEADS
ATTN_Q_W = N_Q_HEADS * HEAD_DIM
ATTN_KV_W = N_KV_HEADS * HEAD_DIM
WINDOW = 128
BLK = 128
ROPE_THETA = 10000.0
M_HEADS = 4
M_W = M_HEADS * HEAD_DIM
N_GATES = 4 * M_HEADS
V_ROWS = HEAD_DIM + 16
SGU_GROUPS = 4
SGU_W = SGU_GROUPS * HEAD_DIM
D_FF = 2816
EPS = 1e-6
LANES = 128
NEG = -1e30
LOG2E = float(np.log2(np.e))

VMEM_LIMIT = 56 * 1024 * 1024
FFN_TM = 512
FFN_SPLIT = 2
W_CHUNKS = 8
W_SLOTS = 3
W_IN_CHUNKS = 5
MIX_TM = 512
MIX_SPLIT = 1


def _dot(a, b):
    return jnp.dot(a, b, preferred_element_type=F32)


def _dot_nt(a, b):
    return lax.dot_general(a, b, (((1,), (1,)), ((), ())), preferred_element_type=F32)


def _dot_tn(a, b):
    return lax.dot_general(a, b, (((0,), (0,)), ((), ())), preferred_element_type=F32)


def _aligned(x, m):
    return x if isinstance(x, int) else pl.multiple_of(x, m)


def _rms(x, g):
    ms = jnp.mean(x * x, axis=-1, keepdims=True)
    return x * lax.rsqrt(ms + EPS) * g


def _group_mean_sq(t, hmat):
    return _dot((t * t).astype(BF16), hmat)


def _const_spec(shape):
    nd = len(shape)
    return pl.BlockSpec(shape, lambda *_: (0,) * nd, pipeline_mode=pl.Buffered(1))


def _layer_spec(stacked_shape, layer):
    nd = len(stacked_shape) - 1
    return pl.BlockSpec((None,) + tuple(stacked_shape[1:]), lambda *_: (layer,) + (0,) * nd,
                        pipeline_mode=pl.Buffered(1))


def _params(n_grid):
    return pltpu.CompilerParams(dimension_semantics=("parallel",) * n_grid,
                                vmem_limit_bytes=VMEM_LIMIT)


ROPE_HALF = HEAD_DIM // 2
ROPE_PACK = LANES // ROPE_HALF
ROPE_STEPS = 8


def _rope_kernel(pos_ref, freq_ref, cos_ref, sin_ref):
    ang = pos_ref[...] * freq_ref[...]
    rows = ang.shape[0]
    token = lax.broadcasted_iota(jnp.int32, ang.shape, 1) >> 5
    for table, out_ref in ((jnp.cos(ang), cos_ref), (jnp.sin(ang), sin_ref)):
        for i in range(ROPE_PACK):
            x = jnp.where(token == i, table, 0.0)
            y = x + pltpu.roll(x, 2 * ROPE_HALF, 1)
            out_ref[pl.ds(i, rows, stride=ROPE_PACK), :] = y + pltpu.roll(y, ROPE_HALF, 1)


def _rope_tables(positions):
    T = positions.size
    freqs = ROPE_THETA ** (-jnp.arange(0, HEAD_DIM, 2, dtype=F32) / HEAD_DIM)
    pos = jnp.repeat(positions.reshape(T // ROPE_PACK, ROPE_PACK).astype(F32), ROPE_HALF, axis=1)
    freq_row = jnp.tile(freqs, ROPE_PACK).reshape(1, LANES)
    rows = T // ROPE_PACK // ROPE_STEPS
    return pl.pallas_call(
        _rope_kernel,
        grid=(ROPE_STEPS,),
        in_specs=[pl.BlockSpec((rows, LANES), lambda i: (i, 0)),
                  pl.BlockSpec((1, LANES), lambda i: (0, 0))],
        out_specs=[pl.BlockSpec((rows * ROPE_PACK, LANES), lambda i: (i, 0))] * 2,
        out_shape=[jax.ShapeDtypeStruct((T, LANES), F32)] * 2,
        compiler_params=_params(1),
        name="rope_tables",
    )(pos, freq_row)


def _load_weight_bf16(w_hbm, layer, dst, stage, sems):
    n_slots, rows, cols = stage.shape
    n_chunks = dst.shape[0] // rows

    def copy(c):
        slot = c % n_slots
        return pltpu.make_async_copy(w_hbm.at[layer, pl.ds(c * rows, rows), pl.ds(0, cols)], stage.at[slot],
                                     sems.at[slot])

    for c in range(min(n_slots - 1, n_chunks)):
        copy(c).start()
    for c in range(n_chunks):
        if c + n_slots - 1 < n_chunks:
            copy(c + n_slots - 1).start()
        copy(c).wait()
        dst[c * rows:(c + 1) * rows, :] = stage[c % n_slots].astype(BF16)


def _ffn_kernel(*refs, layer, has_proj, has_final):
    refs = list(refs)
    x_ref = refs.pop(0)
    if has_proj:
        ya_ref, ym_ref, ys_ref, wo_ref = refs[:4]
        refs = refs[4:]
    g_ref, wg_hbm, wu_hbm, wd_hbm = refs[:4]
    refs = refs[4:]
    if has_final:
        gout_ref = refs.pop(0)
    out_ref, wg_ref, wu_ref, wd_ref, stage_in, stage_out, sems = refs

    @pl.when(pl.program_id(0) == 0)
    def _():
        _load_weight_bf16(wg_hbm, layer, wg_ref, stage_in, sems)
        _load_weight_bf16(wu_hbm, layer, wu_ref, stage_in, sems)
        _load_weight_bf16(wd_hbm, layer, wd_ref, stage_out, sems)

    hs = x_ref.shape[0] // FFN_SPLIT
    rows = [slice(i * hs, (i + 1) * hs) for i in range(FFN_SPLIT)]
    xs = []
    for r in rows:
        x = x_ref[r, :]
        if has_proj:
            x = (x + _dot(ya_ref[r, :], wo_ref[0:ATTN_Q_W, :])
                 + _dot(ym_ref[r, :], wo_ref[ATTN_Q_W:ATTN_Q_W + M_W, :])
                 + _dot(ys_ref[r, :], wo_ref[ATTN_Q_W + M_W:, :]))
        xs.append(x)
    hidden = [_rms(x, g_ref[...]).astype(BF16) for x in xs]
    gate_up = [(_dot(h, wg_ref[...]), _dot(h, wu_ref[...])) for h in hidden]
    for r, x, (gate, up) in zip(rows, xs, gate_up):
        act = (gate * jax.nn.sigmoid(gate) * up).astype(BF16)
        y = x + 0.5 * _dot(act, wd_ref[...])
        if has_final:
            y = _rms(y, gout_ref[...])
        out_ref[r, :] = y


def _ffn(x, layer, g, wg, wu, wd, proj=None, gout=None, tm=FFN_TM):
    T = x.shape[0]
    d_model, d_ff = wg.shape[1:]
    row = lambda w: pl.BlockSpec((tm, w), lambda i: (i, 0))
    hbm = pl.BlockSpec(memory_space=pl.ANY)
    args, specs = [x], [row(D_MODEL)]
    if proj is not None:
        ya, ym, ys, wo = proj
        args += [ya, ym, ys, wo]
        specs += [row(ATTN_Q_W), row(M_W), row(SGU_W), _layer_spec(wo.shape, layer)]
    args += [g.reshape(1, D_MODEL), wg, wu, wd]
    specs += [_const_spec((1, D_MODEL)), hbm, hbm, hbm]
    if gout is not None:
        args.append(gout.reshape(1, D_MODEL))
        specs.append(_const_spec((1, D_MODEL)))
    return pl.pallas_call(
        functools.partial(_ffn_kernel, layer=layer, has_proj=proj is not None, has_final=gout is not None),
        grid=(T // tm,),
        in_specs=specs,
        out_specs=row(D_MODEL),
        out_shape=jax.ShapeDtypeStruct((T, D_MODEL), F32),
        scratch_shapes=[pltpu.VMEM((d_model, d_ff), BF16), pltpu.VMEM((d_model, d_ff), BF16),
                        pltpu.VMEM((d_ff, d_model), BF16),
                        pltpu.VMEM((W_SLOTS, d_model // W_CHUNKS, d_ff), F32),
                        pltpu.VMEM((W_SLOTS, d_ff // W_CHUNKS, d_model), F32),
                        pltpu.SemaphoreType.DMA((W_SLOTS,))],
        compiler_params=pltpu.CompilerParams(dimension_semantics=("arbitrary",), vmem_limit_bytes=VMEM_LIMIT),
        name="ffn",
    )(*args)


def _first_half(shape):
    lane = lax.broadcasted_iota(jnp.int32, shape, 1)
    return (lane & (HEAD_DIM - 1)) < HEAD_DIM // 2


def _rope(t, cos, sin_signed):
    n = t.shape[-1]
    reps = n // LANES
    swapped = jnp.where(_first_half(t.shape),
                        pltpu.roll(t, n - HEAD_DIM // 2, 1), pltpu.roll(t, HEAD_DIM // 2, 1))
    return t * jnp.tile(cos, (1, reps)) + swapped * jnp.tile(sin_signed, (1, reps))


def _gelu(x):
    return 0.5 * x * (1.0 + lax.erf(x * np.float32(np.sqrt(0.5))))


def _mixin_kernel(x_ref, g_ref, w_hbm, qg_ref, kg_ref, cos_ref, sin_ref, hmat_ref, ng_ref, ws_ref,
                  bs_ref, qa_ref, kd_ref, vat_ref, mqk_ref, mvt_ref, mo_ref, mgt_ref, ys_ref, w_ref, stage, sems,
                  *, layer):
    @pl.when(pl.program_id(0) == 0)
    def _():
        _load_weight_bf16(w_hbm, layer, w_ref, stage, sems)

    scale = HEAD_DIM ** -0.5
    hmat = hmat_ref[...]
    gw = hmat.shape[0]
    c_q, c_kv, c_mq, c_mk, c_mv, c_mo, c_g, c_su, c_sv = (int(c) for c in np.cumsum(
        [0, ATTN_Q_W, 2 * ATTN_KV_W, M_W, M_W, M_W, M_W, N_GATES, SGU_W]))
    lane_group = lax.broadcasted_iota(jnp.int32, (BLK, SGU_W), 1) >> 6

    def projections(r0, nrows):
        r = slice(r0, r0 + nrows)
        h = _rms(x_ref[r, :], g_ref[...]).astype(BF16)
        proj = lambda start, width: _dot_nt(h, w_ref[start:start + width, :])
        su = proj(c_su, SGU_W)
        sv = proj(c_sv, SGU_W)
        aq = proj(c_q, ATTN_Q_W)
        akv = proj(c_kv, 2 * ATTN_KV_W)
        ak, av = akv[:, :ATTN_KV_W], akv[:, ATTN_KV_W:]
        v = _gelu(sv)
        v_ms = _group_mean_sq(v, hmat)
        mqk_ref[r, 0:M_W] = proj(c_mq, M_W).astype(BF16)
        mqk_ref[r, M_W:2 * M_W] = (proj(c_mk, M_W) * scale).astype(BF16)
        q_ms = jnp.concatenate([_group_mean_sq(aq[:, i:i + gw], hmat) for i in range(0, ATTN_Q_W, gw)], axis=1)
        k_ms = _group_mean_sq(ak, hmat[0:ATTN_KV_W, 0:ATTN_KV_W])
        mvt_ref[:, r] = proj(c_mv, M_W).T.astype(BF16)
        mo_ref[r, :] = proj(c_mo, M_W).astype(BF16)
        mgt_ref[:, r] = proj(c_g, LANES).T[0:N_GATES, :]

        def tail():
            cos = cos_ref[r, :]
            sin_signed = jnp.where(_first_half(cos.shape), -sin_ref[r, :], sin_ref[r, :])
            qn = aq * lax.rsqrt(q_ms + EPS) * qg_ref[...]
            qa_ref[r, :] = (_rope(qn, cos, sin_signed) * (scale * LOG2E)).astype(BF16)
            kn = ak * lax.rsqrt(k_ms + EPS) * kg_ref[...]
            kr = _rope(kn, cos, sin_signed)
            ks = pltpu.roll(kr, HEAD_DIM, 1)
            low = lax.broadcasted_iota(jnp.int32, kr.shape, 1) < HEAD_DIM
            kd_ref[r, 0:LANES] = jnp.where(low, kr, ks).astype(BF16)
            kd_ref[r, LANES:2 * LANES] = jnp.where(low, ks, kr).astype(BF16)
            vat_ref[:, r] = av.T.astype(BF16)
            u = _gelu(su)
            vn = (v * lax.rsqrt(v_ms + EPS) * ng_ref[...]).astype(BF16)
            for c in range(nrows // BLK):
                vc = vn[c * BLK:(c + 1) * BLK]
                stack = jnp.concatenate(
                    [jnp.where(lane_group == g, vc, jnp.zeros_like(vc)) for g in range(SGU_GROUPS)], axis=0)
                mixed = _dot(ws_ref[...], stack)
                ys_ref[r0 + c * BLK:r0 + (c + 1) * BLK, :] = (
                    u[c * BLK:(c + 1) * BLK] * (mixed + bs_ref[...])).astype(BF16)

        return tail

    nrows = x_ref.shape[0] // MIX_SPLIT
    tails = [projections(i * nrows, nrows) for i in range(MIX_SPLIT)]
    for tail in tails:
        tail()


def _mix_in(x, layer, g, w_in_t, q_g, k_g, cos, sin, hmat, sgu_g, sgu_w, sgu_b, tm=MIX_TM):
    T = x.shape[0]
    d_in, d_model = w_in_t.shape[1:]
    chunk = d_in // W_IN_CHUNKS
    assert chunk * W_IN_CHUNKS == d_in and chunk % 16 == 0
    row = lambda w: pl.BlockSpec((tm, w), lambda i: (i, 0))
    tcol = lambda r: pl.BlockSpec((r, tm), lambda i: (0, i))
    out_specs = [row(ATTN_Q_W), row(2 * ATTN_KV_W), tcol(ATTN_KV_W), row(2 * M_W), tcol(M_W), row(M_W),
                 tcol(N_GATES), row(SGU_W)]
    out_shape = [jax.ShapeDtypeStruct(s, d) for s, d in [
        ((T, ATTN_Q_W), BF16), ((T, 2 * ATTN_KV_W), BF16), ((ATTN_KV_W, T), BF16), ((T, 2 * M_W), BF16),
        ((M_W, T), BF16), ((T, M_W), BF16), ((N_GATES, T), F32), ((T, SGU_W), BF16)]]
    ws_cat = jnp.transpose(sgu_w, (1, 0, 2)).reshape(BLK, SGU_GROUPS * BLK).astype(BF16)
    bs_full = jnp.repeat(jnp.transpose(sgu_b), HEAD_DIM, axis=1)
    return pl.pallas_call(
        functools.partial(_mixin_kernel, layer=layer),
        grid=(T // tm,),
        in_specs=[row(D_MODEL), _const_spec((1, D_MODEL)), pl.BlockSpec(memory_space=pl.ANY),
                  _const_spec((1, ATTN_Q_W)), _const_spec((1, ATTN_KV_W)),
                  row(LANES), row(LANES), _const_spec(hmat.shape),
                  _const_spec((1, SGU_W)), _const_spec(ws_cat.shape), _const_spec(bs_full.shape)],
        out_specs=out_specs,
        out_shape=out_shape,
        scratch_shapes=[pltpu.VMEM((d_in, d_model), BF16),
                        pltpu.VMEM((W_SLOTS, chunk, d_model), F32),
                        pltpu.SemaphoreType.DMA((W_SLOTS,))],
        compiler_params=pltpu.CompilerParams(dimension_semantics=("arbitrary",), vmem_limit_bytes=VMEM_LIMIT),
        name="mix_in",
    )(x, g.reshape(1, D_MODEL), w_in_t,
      jnp.tile(q_g, N_Q_HEADS).reshape(1, ATTN_Q_W), jnp.tile(k_g, N_KV_HEADS).reshape(1, ATTN_KV_W),
      cos, sin, hmat, sgu_g.reshape(1, SGU_W), ws_cat, bs_full)


def _attn_blocks(sink_ref, q_ref, kd_ref, vt_ref, o_ref):
    S = q_ref.shape[0]
    nb = S // BLK
    kc = lax.broadcasted_iota(jnp.int32, (BLK, BLK), 0)
    qi = lax.broadcasted_iota(jnp.int32, (BLK, BLK), 1)
    prev_bias = jnp.tile(jnp.where(kc >= qi, 0.0, NEG), (1, Q_PER_KV))
    next_bias = jnp.tile(jnp.where(kc <= qi, 0.0, NEG), (1, Q_PER_KV))
    half_mask = (qi < HEAD_DIM, qi >= HEAD_DIM)
    head_lane = lax.broadcasted_iota(jnp.int32, (1, Q_PER_KV * BLK), 1) >> 7
    sink_rows = []
    for kvh in range(N_KV_HEADS):
        row = jnp.zeros((1, Q_PER_KV * BLK), F32)
        for g in range(Q_PER_KV):
            row = jnp.where(head_lane == g, sink_ref[kvh * Q_PER_KV + g] * LOG2E, row)
        sink_rows.append(row)

    def block(q0, start, biases, with_scores):
        nk = len(biases)
        scores = []
        for kvh in range(N_KV_HEADS if with_scores else 0):
            kd = kd_ref[pl.ds(start, nk * BLK), kvh * LANES:(kvh + 1) * LANES]
            qs = []
            for g in range(Q_PER_KV):
                t = kvh * Q_PER_KV + g
                qt = q_ref[pl.ds(q0, BLK), (t // 2) * LANES:(t // 2 + 1) * LANES]
                qs.append(jnp.where(half_mask[t & 1], qt, jnp.zeros_like(qt)))
            scores.append(_dot_nt(kd, jnp.concatenate(qs, axis=0)))

        def weighted_values(scores):
            ones = jnp.ones((HEAD_DIM, nk * BLK), BF16)
            outs = []
            for kvh in range(N_KV_HEADS):
                v_ext = jnp.concatenate([vt_ref[kvh * HEAD_DIM:(kvh + 1) * HEAD_DIM, pl.ds(start, nk * BLK)], ones],
                                        axis=0)
                parts = [scores[kvh][j * BLK:(j + 1) * BLK] for j in range(nk)]
                parts = [p if b is None else p + b for p, b in zip(parts, biases)]
                mx = jnp.max(functools.reduce(jnp.maximum, parts), axis=0, keepdims=True)
                m = jnp.maximum(mx, sink_rows[kvh])
                p = jnp.concatenate([jnp.exp2(p - m) for p in parts], axis=0).astype(BF16)
                oe = _dot(v_ext, p)
                denom = oe[HEAD_DIM:HEAD_DIM + 1] + jnp.exp2(sink_rows[kvh] - m)
                on = oe[0:HEAD_DIM] / denom
                for pair in range(Q_PER_KV // 2):
                    two = jnp.concatenate([on[:, (2 * pair) * BLK:(2 * pair + 1) * BLK],
                                           on[:, (2 * pair + 1) * BLK:(2 * pair + 2) * BLK]], axis=0)
                    outs.append(two.T)
            result = jnp.concatenate(outs, axis=1).astype(BF16)

            def commit():
                o_ref[pl.ds(q0, BLK), :] = result
            return commit

        return (scores if with_scores else None), weighted_values

    def attend(n, with_scores=True):
        if isinstance(n, int) and n == 0:
            return block(0, 0, [None, next_bias], with_scores)
        if isinstance(n, int) and n == nb - 1:
            return block(n * BLK, (n - 1) * BLK, [prev_bias, None], with_scores)
        return block(_aligned(n * BLK, BLK), _aligned((n - 1) * BLK, BLK), [prev_bias, None, next_bias], with_scores)

    return attend


def _log_sigmoid(x):
    return jnp.minimum(x, 0.0) - jnp.log(1.0 + jnp.exp(-jnp.abs(x)))


def _mlstm_chunks(qk_ref, vt_ref, mo_ref, mgt_ref, gb_ref, hg_ref, o_ref,
                  gt_s, b_s, w_s, mloc_s, tot_s, rcol_s, cloc_s, cprev_s, mprev_s, keep_s, add_s):
    S = qk_ref.shape[0]
    L = BLK
    nc = S // L
    ng = N_GATES
    sub = lax.broadcasted_iota(jnp.int32, (L, L), 0)
    lane = lax.broadcasted_iota(jnp.int32, (L, L), 1)

    for c in range(nc):
        gt_s[c * ng:(c + 1) * ng, :] = mgt_ref[:, c * L:(c + 1) * L] + gb_ref[...]
    gt = gt_s[...]
    row = lax.broadcasted_iota(jnp.int32, gt.shape, 0)
    gt = jnp.where(((row >> 2) & 1) == 1, _log_sigmoid(gt), gt)
    gt = gt * LOG2E
    upper = jnp.where(sub <= lane, 1.0, 0.0).astype(BF16)
    g1 = gt.astype(BF16)
    r1 = gt - g1.astype(F32)
    g2 = r1.astype(BF16)
    g3 = (r1 - g2.astype(F32)).astype(BF16)
    pre = _dot(g1, upper) + _dot(g2, upper) + _dot(g3, upper)
    tot = jnp.broadcast_to(pre[:, L - 1:L], pre.shape)
    bsum =jnp.where((row & (ng - 1)) < ng // 2, pre, tot - pre + gt)
    li = pltpu.roll(gt, M_HEADS, 0)
    a = tot - bsum + li
    mloc = jnp.broadcast_to(jnp.max(a, axis=1, keepdims=True), a.shape)
    b_s[...] = bsum
    w_s[...] = jnp.exp2(a - mloc)
    mloc_s[...] = mloc
    tot_s[...] = tot
    rdiff = li - bsum
    zpad = jnp.zeros((L - ng, L), F32)
    for c in range(nc):
        rcol_s[c * L:(c + 1) * L, :] = jnp.concatenate([rdiff[c * ng:(c + 1) * ng, :], zpad], axis=0).T

    def value_rows(hd, r0, dtype):
        vt = vt_ref[hd * HEAD_DIM:(hd + 1) * HEAD_DIM, pl.ds(r0, L)].astype(dtype)
        return jnp.concatenate([vt, jnp.ones((V_ROWS - HEAD_DIM, L), dtype)], axis=0)

    def local_state(c, carry):
        r0 = pl.multiple_of(c * L, L)
        g0 = pl.multiple_of(c * ng, ng)
        k = qk_ref[pl.ds(r0, L), M_W:2 * M_W]
        w = w_s[pl.ds(g0, ng), :]
        for hd in range(M_HEADS):
            vte = value_rows(hd, r0, F32)
            lhs = jnp.concatenate([vte * w[4 + hd:5 + hd, :], vte * w[12 + hd:13 + hd, :]], axis=0)
            cl = _dot(lhs.astype(BF16), k[:, (hd // 2) * LANES:(hd // 2 + 1) * LANES])
            cloc_s[0, hd, c] = cl[0:V_ROWS]
            cloc_s[1, hd, c] = cl[V_ROWS:2 * V_ROWS]
        return carry

    lax.fori_loop(0, nc, local_state, 0, unroll=2)

    jrow = lax.broadcasted_iota(jnp.int32, (ng, L), 0)

    def scan_stabilisers(i, m):
        cf = i
        cb = nc - 1 - i
        gf = pl.multiple_of(cf * ng, ng)
        gb = pl.multiple_of(cb * ng, ng)
        is_fwd = jrow < ng // 2
        blast = jnp.where(is_fwd, tot_s[pl.ds(gf, ng), :], tot_s[pl.ds(gb, ng), :])
        mloc_i = jnp.where(is_fwd, mloc_s[pl.ds(gf, ng), :], mloc_s[pl.ds(gb, ng), :])
        mprev_s[0, pl.ds(gf, ng), :] = m
        mprev_s[1, pl.ds(gb, ng), :] = m
        m_new = jnp.maximum(blast + m, mloc_i)
        gi = pl.multiple_of(i * ng, ng)
        keep_s[pl.ds(gi, ng), :] = jnp.exp2(blast + m - m_new)
        add_s[pl.ds(gi, ng), :] = jnp.exp2(mloc_i - m_new)
        return m_new

    lax.fori_loop(0, nc, scan_stabilisers, jnp.zeros((ng, L), F32))

    for hd in range(M_HEADS):
        def scan_states(i, states, hd=hd):
            gi = pl.multiple_of(i * ng, ng)
            keep = keep_s[pl.ds(gi, ng), :]
            add = add_s[pl.ds(gi, ng), :]
            new_states = []
            for d, ci in enumerate((i, nc - 1 - i)):
                r = 4 + hd + 8 * d
                cprev_s[d, hd, ci] = states[d].astype(BF16)
                new_states.append(keep[r:r + 1, :] * states[d] + add[r:r + 1, :] * cloc_s[d, hd, ci])
            return tuple(new_states)

        lax.fori_loop(0, nc, scan_states, (jnp.zeros((V_ROWS, LANES), F32), jnp.zeros((V_ROWS, LANES), F32)))

    causal = (sub <= lane, sub >= lane)

    def outputs(c, with_scores=True):
        r0 = _aligned(c * L, L)
        g0 = _aligned(c * ng, ng)
        q = qk_ref[pl.ds(r0, L), 0:M_W]
        k = qk_ref[pl.ds(r0, L), M_W:2 * M_W]
        bc = b_s[pl.ds(g0, ng), :]
        mprev = (mprev_s[0, pl.ds(g0, ng), :], mprev_s[1, pl.ds(g0, ng), :])
        rc = rcol_s[pl.ds(r0, L), :]
        ys = []
        for hd in range(M_HEADS if with_scores else 0):
            t = hd // 2
            qp = q[:, t * LANES:(t + 1) * LANES]
            qm = jnp.where((lane >> 6) == (hd & 1), qp, jnp.zeros_like(qp))
            x = jnp.concatenate([k[:, t * LANES:(t + 1) * LANES], cprev_s[0, hd, c], cprev_s[1, hd, c]], axis=0)
            ys.append(_dot_nt(x, qm))

        def combine(ys):
            houts = []
            for hd in range(M_HEADS):
                y = ys[hd]
                st = y[0:L]
                pts, scs, mts = [], [], []
                for d in range(2):
                    r = 4 + hd + 8 * d
                    b_row = bc[r:r + 1, :]
                    dm = jnp.where(causal[d], rc[:, r:r + 1] + b_row, NEG)
                    inter = b_row + mprev[d][r:r + 1, :]
                    m_t = jnp.maximum(inter, jnp.max(dm, axis=0, keepdims=True))
                    pts.append((jnp.exp2(dm - m_t) * st).astype(BF16))
                    scs.append(jnp.exp2(inter - m_t))
                    mts.append(m_t)
                n2 = _dot(value_rows(hd, r0, BF16), jnp.concatenate(pts, axis=1))
                ht = None
                for d in range(2):
                    tt = n2[:, d * L:(d + 1) * L] + scs[d] * y[L + d * V_ROWS:L + (d + 1) * V_ROWS]
                    den = tt[HEAD_DIM:HEAD_DIM + 1, :]
                    hd_out = tt[0:HEAD_DIM] / jnp.maximum(jnp.abs(den), jnp.exp2(-mts[d]))
                    ht = hd_out if ht is None else ht + hd_out
                ms = jnp.mean(ht * ht, axis=0, keepdims=True)
                houts.append(ht * lax.rsqrt(ms + EPS))
            hn = (jnp.concatenate(houts, axis=0) * hg_ref[...]).T
            result = (jax.nn.sigmoid(mo_ref[pl.ds(r0, L), :].astype(F32)) * hn).astype(BF16)

            def commit():
                o_ref[pl.ds(r0, L), :] = result
            return commit

        return (ys if with_scores else None), combine

    return outputs


def _mixers_kernel(sink_ref, q_ref, kd_ref, vat_ref, qk_ref, mvt_ref, mo_ref, mgt_ref, gb_ref, hg_ref,
                   ya_ref, ym_ref, *scratch):
    *mlstm_scratch, s_buf, y_buf = scratch
    nb = q_ref.shape[0] // BLK
    n_groups = (nb - 2) // 2
    attend = _attn_blocks(sink_ref, q_ref, kd_ref, vat_ref, ya_ref)
    outputs = _mlstm_chunks(qk_ref, mvt_ref, mo_ref, mgt_ref, gb_ref, hg_ref, ym_ref, *mlstm_scratch)
    group = lambda g: (1 + 2 * g, 2 + 2 * g)

    def first(blocks):
        return [f(n)[0] for n in blocks for f in (attend, outputs)]

    def second(blocks, results):
        finishes = [f(n, with_scores=False)[1] for n in blocks for f in (attend, outputs)]
        commits = [finish(r) for finish, r in zip(finishes, results)]
        for commit in commits:
            commit()

    def park(slot, results):
        for b in range(2):
            for i, s in enumerate(results[2 * b]):
                s_buf[slot, N_KV_HEADS * b + i, 0:s.shape[0], :] = s
            for i, y in enumerate(results[2 * b + 1]):
                y_buf[slot, M_HEADS * b + i] = y

    def fetch(slot, key_blocks):
        results = []
        for b in range(2):
            results.append([s_buf[slot, N_KV_HEADS * b + i, 0:key_blocks * BLK, :] for i in range(N_KV_HEADS)])
            results.append([y_buf[slot, M_HEADS * b + i] for i in range(M_HEADS)])
        return results

    edges = (0, nb - 1)

    def advance(g, slot):
        results = first(group(g))
        if isinstance(g, int) and g == 0:
            second(edges, fetch(1 - slot, 2))
        else:
            second(group(g - 1), fetch(1 - slot, 3))
        park(slot, results)

    def two_steps(j, carry):
        advance(1 + 2 * j, 0)
        advance(2 + 2 * j, 1)
        return carry

    park(0, first(edges))
    advance(0, 1)
    lax.fori_loop(0, (n_groups - 1) // 2, two_steps, 0)
    last_slot = 1
    if (n_groups - 1) % 2:
        advance(n_groups - 1, 0)
        last_slot = 0
    second(group(n_groups - 1), fetch(last_slot, 3))


def _mixers(qa, kd, vat, sink, mqk, mvt, mo, mgt, gate_b, head_g, B, S):
    seq = lambda w: pl.BlockSpec((S, w), lambda b: (b, 0))
    tseq = lambda r: pl.BlockSpec((r, S), lambda b: (0, b))
    nc = S // BLK
    assert nc % 2 == 0 and nc >= 4
    gb = jnp.broadcast_to(gate_b.reshape(N_GATES, 1), (N_GATES, BLK))
    hg_t = jnp.broadcast_to(head_g.reshape(M_W, 1), (M_W, BLK))
    rows = nc * N_GATES
    return pl.pallas_call(
        _mixers_kernel,
        grid=(B,),
        in_specs=[pl.BlockSpec(memory_space=pltpu.SMEM), seq(ATTN_Q_W), seq(2 * ATTN_KV_W), tseq(ATTN_KV_W),
                  seq(2 * M_W), tseq(M_W), seq(M_W), tseq(N_GATES), _const_spec((N_GATES, BLK)),
                  _const_spec((M_W, BLK))],
        out_specs=[seq(ATTN_Q_W), seq(M_W)],
        out_shape=[jax.ShapeDtypeStruct((B * S, ATTN_Q_W), BF16), jax.ShapeDtypeStruct((B * S, M_W), BF16)],
        scratch_shapes=[pltpu.VMEM((rows, BLK), F32),
                        pltpu.VMEM((rows, BLK), F32),
                        pltpu.VMEM((rows, BLK), F32),
                        pltpu.VMEM((rows, BLK), F32),
                        pltpu.VMEM((rows, BLK), F32),
                        pltpu.VMEM((S, LANES), F32),
                        pltpu.VMEM((2, M_HEADS, nc, V_ROWS, LANES), F32),
                        pltpu.VMEM((2, M_HEADS, nc, V_ROWS, LANES), BF16),
                        pltpu.VMEM((2, rows, BLK), F32),
                        pltpu.VMEM((rows, BLK), F32),
                        pltpu.VMEM((rows, BLK), F32),
                        pltpu.VMEM((2, 2 * N_KV_HEADS, 3 * BLK, Q_PER_KV * BLK), F32),
                        pltpu.VMEM((2, 2 * M_HEADS, BLK + 2 * V_ROWS, BLK), F32)],
        compiler_params=_params(1),
        name="mixers",
    )(sink, qa, kd, vat, mqk, mvt, mo, mgt, gb, hg_t)


def kernel(x, positions, norm_ffn1_g, ffn1_w_gate, ffn1_w_up, ffn1_w_down, norm_mix_g, w_in, q_norm_g, k_norm_g, attn_sink, mlstm_gate_b, mlstm_head_g, sgu_norm_g, sgu_w_s, sgu_b, w_out, norm_ffn2_g, ffn2_w_gate, ffn2_w_up, ffn2_w_down, norm_out_g):
    B, S, D = x.shape
    depth = w_in.shape[0]
    T = B * S
    xt = x.reshape(T, D)
    cos, sin = _rope_tables(positions)
    group = np.arange(M_W) // HEAD_DIM
    hmat = jnp.asarray((group[:, None] == group[None, :]) / HEAD_DIM, BF16)
    w_in_t = jnp.swapaxes(w_in, 1, 2)
    w_o = w_out.astype(BF16)
    for l in range(depth):
        xt = _ffn(xt, l, norm_ffn1_g[l], ffn1_w_gate, ffn1_w_up, ffn1_w_down)
        qa, kd, vat, mqk, mvt, mo, mgt, ys = _mix_in(xt, l, norm_mix_g[l], w_in_t, q_norm_g[l],
                                                     k_norm_g[l], cos, sin, hmat, sgu_norm_g[l], sgu_w_s[l], sgu_b[l])
        ya, ym = _mixers(qa, kd, vat, attn_sink[l], mqk, mvt, mo, mgt, mlstm_gate_b[l], mlstm_head_g[l], B, S)
        xt = _ffn(xt, l, norm_ffn2_g[l], ffn2_w_gate, ffn2_w_up, ffn2_w_down, proj=(ya, ym, ys, w_o),
                  gout=norm_out_g[l])
    return xt.reshape(B, S, D)
```

```python
import functools

import numpy as np
import jax
import jax.numpy as jnp
from jax import lax
from jax.experimental import pallas as pl
from jax.experimental.pallas import tpu as pltpu

F32 = jnp.float32
BF16 = jnp.bfloat16

D_MODEL = 1024
HEAD_DIM = 64
N_Q_HEADS = 8
N_KV_HEADS = 2
Q_PER_KV = N_Q_HEADS // N_KV_HEADS
ATTN_Q_W = N_Q_HEADS * HEAD_DIM
ATTN_KV_W = N_KV_HEADS * HEAD_DIM
WINDOW = 128
BLK = 128
ROPE_THETA = 10000.0
M_HEADS = 4
M_W = M_HEADS * HEAD_DIM
N_GATES = 4 * M_HEADS
V_ROWS = HEAD_DIM + 16
SCAN_HEADS = 2
SGU_GROUPS = 4
SGU_W = SGU_GROUPS * HEAD_DIM
D_FF = 2816
EPS = 1e-6
LANES = 128
NEG = -1e30
LOG2E = float(np.log2(np.e))

VMEM_LIMIT = 56 * 1024 * 1024
FFN_TM = 512
FFN_SPLIT = 2
W_CHUNKS = 8
W_SLOTS = 3
W_OUT_CHUNKS = 4
W_IN_CHUNKS = 5
MIX_TM = 512
MIX_SPLIT = 1


def _dot(a, b):
    return jnp.dot(a, b, preferred_element_type=F32)


def _dot_nt(a, b):
    return lax.dot_general(a, b, (((1,), (1,)), ((), ())), preferred_element_type=F32)


def _dot_tn(a, b):
    return lax.dot_general(a, b, (((0,), (0,)), ((), ())), preferred_element_type=F32)


def _aligned(x, m):
    return x if isinstance(x, int) else pl.multiple_of(x, m)


def _rms(x, g):
    ms = jnp.mean(x * x, axis=-1, keepdims=True)
    return x * lax.rsqrt(ms + EPS) * g


def _group_mean_sq(t, hmat):
    return _dot((t * t).astype(BF16), hmat)


def _const_spec(shape):
    nd = len(shape)
    return pl.BlockSpec(shape, lambda *_: (0,) * nd, pipeline_mode=pl.Buffered(1))


def _layer_spec(stacked_shape, layer):
    nd = len(stacked_shape) - 1
    return pl.BlockSpec((None,) + tuple(stacked_shape[1:]), lambda *_: (layer,) + (0,) * nd,
                        pipeline_mode=pl.Buffered(1))


def _params(n_grid):
    return pltpu.CompilerParams(dimension_semantics=("parallel",) * n_grid,
                                vmem_limit_bytes=VMEM_LIMIT)


ROPE_HALF = HEAD_DIM // 2
ROPE_PACK = LANES // ROPE_HALF
ROPE_STEPS = 8


def _rope_kernel(pos_ref, freq_ref, cos_ref, sin_ref):
    ang = pos_ref[...] * freq_ref[...]
    rows = ang.shape[0]
    token = lax.broadcasted_iota(jnp.int32, ang.shape, 1) >> 5
    for table, out_ref in ((jnp.cos(ang), cos_ref), (jnp.sin(ang), sin_ref)):
        for i in range(ROPE_PACK):
            x = jnp.where(token == i, table, 0.0)
            y = x + pltpu.roll(x, 2 * ROPE_HALF, 1)
            out_ref[pl.ds(i, rows, stride=ROPE_PACK), :] = y + pltpu.roll(y, ROPE_HALF, 1)


def _rope_tables(positions):
    T = positions.size
    freqs = ROPE_THETA ** (-jnp.arange(0, HEAD_DIM, 2, dtype=F32) / HEAD_DIM)
    pos = jnp.repeat(positions.reshape(T // ROPE_PACK, ROPE_PACK).astype(F32), ROPE_HALF, axis=1)
    freq_row = jnp.tile(freqs, ROPE_PACK).reshape(1, LANES)
    rows = T // ROPE_PACK // ROPE_STEPS
    return pl.pallas_call(
        _rope_kernel,
        grid=(ROPE_STEPS,),
        in_specs=[pl.BlockSpec((rows, LANES), lambda i: (i, 0)),
                  pl.BlockSpec((1, LANES), lambda i: (0, 0))],
        out_specs=[pl.BlockSpec((rows * ROPE_PACK, LANES), lambda i: (i, 0))] * 2,
        out_shape=[jax.ShapeDtypeStruct((T, LANES), F32)] * 2,
        compiler_params=_params(1),
        name="rope_tables",
    )(pos, freq_row)


def _load_weight_bf16(w_hbm, layer, dst, stage, sems):
    n_slots, rows, cols = stage.shape
    n_chunks = dst.shape[0] // rows

    def copy(c):
        slot = c % n_slots
        return pltpu.make_async_copy(w_hbm.at[layer, pl.ds(c * rows, rows), pl.ds(0, cols)], stage.at[slot],
                                     sems.at[slot])

    for c in range(min(n_slots - 1, n_chunks)):
        copy(c).start()
    for c in range(n_chunks):
        if c + n_slots - 1 < n_chunks:
            copy(c + n_slots - 1).start()
        copy(c).wait()
        dst[c * rows:(c + 1) * rows, :] = stage[c % n_slots].astype(BF16)


def _ffn_kernel(*refs, layer, has_proj, has_final):
    refs = list(refs)
    x_ref = refs.pop(0)
    if has_proj:
        ya_ref, ym_ref, ys_ref, wo_hbm = refs[:4]
        refs = refs[4:]
    g_ref, wg_hbm, wu_hbm, wd_hbm = refs[:4]
    refs = refs[4:]
    if has_final:
        gout_ref = refs.pop(0)
    out_ref, wg_ref, wu_ref, wd_ref, stage_in, stage_out, sems = refs[:7]
    if has_proj:
        wo_ref = refs[7]

    @pl.when(pl.program_id(0) == 0)
    def _():
        if has_proj:
            rows = wo_ref.shape[0] // W_OUT_CHUNKS
            _load_weight_bf16(wo_hbm, layer, wo_ref, stage_out.at[:, 0:rows, :], sems)
        _load_weight_bf16(wg_hbm, layer, wg_ref, stage_in, sems)
        _load_weight_bf16(wu_hbm, layer, wu_ref, stage_in, sems)
        _load_weight_bf16(wd_hbm, layer, wd_ref, stage_out, sems)

    hs = x_ref.shape[0] // FFN_SPLIT
    rows = [slice(i * hs, (i + 1) * hs) for i in range(FFN_SPLIT)]
    xs = []
    for r in rows:
        x = x_ref[r, :]
        if has_proj:
            x = (x + _dot(ya_ref[r, :], wo_ref[0:ATTN_Q_W, :])
                 + _dot(ym_ref[r, :], wo_ref[ATTN_Q_W:ATTN_Q_W + M_W, :])
                 + _dot(ys_ref[r, :], wo_ref[ATTN_Q_W + M_W:, :]))
        xs.append(x)
    hidden = [_rms(x, g_ref[...]).astype(BF16) for x in xs]
    gate_up = [(_dot(h, wg_ref[...]), _dot(h, wu_ref[...])) for h in hidden]
    for r, x, (gate, up) in zip(rows, xs, gate_up):
        act = (gate * jax.nn.sigmoid(gate) * up).astype(BF16)
        y = x + 0.5 * _dot(act, wd_ref[...])
        if has_final:
            y = _rms(y, gout_ref[...])
        out_ref[r, :] = y


def _ffn(x, layer, g, wg, wu, wd, proj=None, gout=None, tm=FFN_TM):
    T = x.shape[0]
    d_model, d_ff = wg.shape[1:]
    row = lambda w: pl.BlockSpec((tm, w), lambda i: (i, 0))
    hbm = pl.BlockSpec(memory_space=pl.ANY)
    args, specs = [x], [row(D_MODEL)]
    if proj is not None:
        ya, ym, ys, wo = proj
        args += [ya, ym, ys, wo]
        specs += [row(ATTN_Q_W), row(M_W), row(SGU_W), hbm]
    args += [g.reshape(1, D_MODEL), wg, wu, wd]
    specs += [_const_spec((1, D_MODEL)), hbm, hbm, hbm]
    if gout is not None:
        args.append(gout.reshape(1, D_MODEL))
        specs.append(_const_spec((1, D_MODEL)))
    return pl.pallas_call(
        functools.partial(_ffn_kernel, layer=layer, has_proj=proj is not None, has_final=gout is not None),
        grid=(T // tm,),
        in_specs=specs,
        out_specs=row(D_MODEL),
        out_shape=jax.ShapeDtypeStruct((T, D_MODEL), F32),
        scratch_shapes=[pltpu.VMEM((d_model, d_ff), BF16), pltpu.VMEM((d_model, d_ff), BF16),
                        pltpu.VMEM((d_ff, d_model), BF16),
                        pltpu.VMEM((W_SLOTS, d_model // W_CHUNKS, d_ff), F32),
                        pltpu.VMEM((W_SLOTS, d_ff // W_CHUNKS, d_model), F32),
                        pltpu.SemaphoreType.DMA((W_SLOTS,))]
        + ([pltpu.VMEM(proj[3].shape[1:], BF16)] if proj is not None else []),
        compiler_params=pltpu.CompilerParams(dimension_semantics=("arbitrary",), vmem_limit_bytes=VMEM_LIMIT),
        name="ffn",
    )(*args)


def _first_half(shape):
    lane = lax.broadcasted_iota(jnp.int32, shape, 1)
    return (lane & (HEAD_DIM - 1)) < HEAD_DIM // 2


def _rope(t, cos, sin_signed):
    n = t.shape[-1]
    reps = n // LANES
    swapped = jnp.where(_first_half(t.shape),
                        pltpu.roll(t, n - HEAD_DIM // 2, 1), pltpu.roll(t, HEAD_DIM // 2, 1))
    return t * jnp.tile(cos, (1, reps)) + swapped * jnp.tile(sin_signed, (1, reps))


def _gelu(x):
    return 0.5 * x * (1.0 + lax.erf(x * np.float32(np.sqrt(0.5))))


def _mixin_kernel(x_ref, g_ref, w_hbm, qg_ref, kg_ref, cos_ref, sin_ref, hmat_ref, ng_ref, ws_ref,
                  bs_ref, qa_ref, kd_ref, vat_ref, mqk_ref, mvt_ref, mo_ref, mgt_ref, ys_ref, w_ref, stage, sems,
                  *, layer):
    @pl.when(pl.program_id(0) == 0)
    def _():
        _load_weight_bf16(w_hbm, layer, w_ref, stage, sems)

    scale = HEAD_DIM ** -0.5
    hmat = hmat_ref[...]
    gw = hmat.shape[0]
    c_q, c_kv, c_mq, c_mk, c_mv, c_mo, c_g, c_su, c_sv = (int(c) for c in np.cumsum(
        [0, ATTN_Q_W, 2 * ATTN_KV_W, M_W, M_W, M_W, M_W, N_GATES, SGU_W]))
    lane_group = lax.broadcasted_iota(jnp.int32, (BLK, SGU_W), 1) >> 6

    def projections(r0, nrows):
        r = slice(r0, r0 + nrows)
        h = _rms(x_ref[r, :], g_ref[...]).astype(BF16)
        proj = lambda start, width: _dot_nt(h, w_ref[start:start + width, :])
        su = proj(c_su, SGU_W)
        sv = proj(c_sv, SGU_W)
        aq = proj(c_q, ATTN_Q_W)
        akv = proj(c_kv, 2 * ATTN_KV_W)
        ak, av = akv[:, :ATTN_KV_W], akv[:, ATTN_KV_W:]
        v = _gelu(sv)
        v_ms = _group_mean_sq(v, hmat)
        mqk_ref[r, 0:M_W] = proj(c_mq, M_W).astype(BF16)
        mqk_ref[r, M_W:2 * M_W] = (proj(c_mk, M_W) * scale).astype(BF16)
        q_ms = jnp.concatenate([_group_mean_sq(aq[:, i:i + gw], hmat) for i in range(0, ATTN_Q_W, gw)], axis=1)
        k_ms = _group_mean_sq(ak, hmat[0:ATTN_KV_W, 0:ATTN_KV_W])
        mvt_ref[:, r] = proj(c_mv, M_W).T.astype(BF16)
        mo_ref[r, :] = proj(c_mo, M_W).astype(BF16)
        mgt_ref[:, r] = proj(c_g, LANES).T[0:N_GATES, :]

        def tail():
            cos = cos_ref[r, :]
            sin_signed = jnp.where(_first_half(cos.shape), -sin_ref[r, :], sin_ref[r, :])
            qn = aq * lax.rsqrt(q_ms + EPS) * qg_ref[...]
            qa_ref[r, :] = (_rope(qn, cos, sin_signed) * (scale * LOG2E)).astype(BF16)
            kn = ak * lax.rsqrt(k_ms + EPS) * kg_ref[...]
            kr = _rope(kn, cos, sin_signed)
            ks = pltpu.roll(kr, HEAD_DIM, 1)
            low = lax.broadcasted_iota(jnp.int32, kr.shape, 1) < HEAD_DIM
            kd_ref[r, 0:LANES] = jnp.where(low, kr, ks).astype(BF16)
            kd_ref[r, LANES:2 * LANES] = jnp.where(low, ks, kr).astype(BF16)
            vat_ref[:, r] = av.T.astype(BF16)
            u = _gelu(su)
            vn = (v * lax.rsqrt(v_ms + EPS) * ng_ref[...]).astype(BF16)
            for c in range(nrows // BLK):
                vc = vn[c * BLK:(c + 1) * BLK]
                stack = jnp.concatenate(
                    [jnp.where(lane_group == g, vc, jnp.zeros_like(vc)) for g in range(SGU_GROUPS)], axis=0)
                mixed = _dot(ws_ref[...], stack)
                ys_ref[r0 + c * BLK:r0 + (c + 1) * BLK, :] = (
                    u[c * BLK:(c + 1) * BLK] * (mixed + bs_ref[...])).astype(BF16)

        return tail

    nrows = x_ref.shape[0] // MIX_SPLIT
    tails = [projections(i * nrows, nrows) for i in range(MIX_SPLIT)]
    for tail in tails:
        tail()


def _mix_in(x, layer, g, w_in_t, q_g, k_g, cos, sin, hmat, sgu_g, sgu_w, sgu_b, tm=MIX_TM):
    T = x.shape[0]
    d_in, d_model = w_in_t.shape[1:]
    chunk = d_in // W_IN_CHUNKS
    assert chunk * W_IN_CHUNKS == d_in and chunk % 16 == 0
    row = lambda w: pl.BlockSpec((tm, w), lambda i: (i, 0))
    tcol = lambda r: pl.BlockSpec((r, tm), lambda i: (0, i))
    out_specs = [row(ATTN_Q_W), row(2 * ATTN_KV_W), tcol(ATTN_KV_W), row(2 * M_W), tcol(M_W), row(M_W),
                 tcol(N_GATES), row(SGU_W)]
    out_shape = [jax.ShapeDtypeStruct(s, d) for s, d in [
        ((T, ATTN_Q_W), BF16), ((T, 2 * ATTN_KV_W), BF16), ((ATTN_KV_W, T), BF16), ((T, 2 * M_W), BF16),
        ((M_W, T), BF16), ((T, M_W), BF16), ((N_GATES, T), F32), ((T, SGU_W), BF16)]]
    ws_cat = jnp.transpose(sgu_w, (1, 0, 2)).reshape(BLK, SGU_GROUPS * BLK).astype(BF16)
    bs_full = jnp.repeat(jnp.transpose(sgu_b), HEAD_DIM, axis=1)
    return pl.pallas_call(
        functools.partial(_mixin_kernel, layer=layer),
        grid=(T // tm,),
        in_specs=[row(D_MODEL), _const_spec((1, D_MODEL)), pl.BlockSpec(memory_space=pl.ANY),
                  _const_spec((1, ATTN_Q_W)), _const_spec((1, ATTN_KV_W)),
                  row(LANES), row(LANES), _const_spec(hmat.shape),
                  _const_spec((1, SGU_W)), _const_spec(ws_cat.shape), _const_spec(bs_full.shape)],
        out_specs=out_specs,
        out_shape=out_shape,
        scratch_shapes=[pltpu.VMEM((d_in, d_model), BF16),
                        pltpu.VMEM((W_SLOTS, chunk, d_model), F32),
                        pltpu.SemaphoreType.DMA((W_SLOTS,))],
        compiler_params=pltpu.CompilerParams(dimension_semantics=("arbitrary",), vmem_limit_bytes=VMEM_LIMIT),
        name="mix_in",
    )(x, g.reshape(1, D_MODEL), w_in_t,
      jnp.tile(q_g, N_Q_HEADS).reshape(1, ATTN_Q_W), jnp.tile(k_g, N_KV_HEADS).reshape(1, ATTN_KV_W),
      cos, sin, hmat, sgu_g.reshape(1, SGU_W), ws_cat, bs_full)


def _attn_blocks(sink_ref, q_ref, kd_ref, vt_ref, o_ref):
    S = q_ref.shape[0]
    nb = S // BLK
    kc = lax.broadcasted_iota(jnp.int32, (BLK, BLK), 0)
    qi = lax.broadcasted_iota(jnp.int32, (BLK, BLK), 1)
    prev_bias = jnp.tile(jnp.where(kc >= qi, 0.0, NEG), (1, Q_PER_KV))
    next_bias = jnp.tile(jnp.where(kc <= qi, 0.0, NEG), (1, Q_PER_KV))
    half_mask = (qi < HEAD_DIM, qi >= HEAD_DIM)
    head_lane = lax.broadcasted_iota(jnp.int32, (1, Q_PER_KV * BLK), 1) >> 7
    sink_rows = []
    for kvh in range(N_KV_HEADS):
        row = jnp.zeros((1, Q_PER_KV * BLK), F32)
        for g in range(Q_PER_KV):
            row = jnp.where(head_lane == g, sink_ref[kvh * Q_PER_KV + g] * LOG2E, row)
        sink_rows.append(row)

    def block(q0, start, biases, with_scores):
        nk = len(biases)
        scores = []
        for kvh in range(N_KV_HEADS if with_scores else 0):
            kd = kd_ref[pl.ds(start, nk * BLK), kvh * LANES:(kvh + 1) * LANES]
            qs = []
            for g in range(Q_PER_KV):
                t = kvh * Q_PER_KV + g
                qt = q_ref[pl.ds(q0, BLK), (t // 2) * LANES:(t // 2 + 1) * LANES]
                qs.append(jnp.where(half_mask[t & 1], qt, jnp.zeros_like(qt)))
            scores.append(_dot_nt(kd, jnp.concatenate(qs, axis=0)))

        def weighted_values(scores):
            ones = jnp.ones((HEAD_DIM, nk * BLK), BF16)
            outs = []
            for kvh in range(N_KV_HEADS):
                v_ext = jnp.concatenate([vt_ref[kvh * HEAD_DIM:(kvh + 1) * HEAD_DIM, pl.ds(start, nk * BLK)], ones],
                                        axis=0)
                parts = [scores[kvh][j * BLK:(j + 1) * BLK] for j in range(nk)]
                parts = [p if b is None else p + b for p, b in zip(parts, biases)]
                mx = jnp.max(functools.reduce(jnp.maximum, parts), axis=0, keepdims=True)
                m = jnp.maximum(mx, sink_rows[kvh])
                p = jnp.concatenate([jnp.exp2(p - m) for p in parts], axis=0).astype(BF16)
                oe = _dot(v_ext, p)
                denom = oe[HEAD_DIM:HEAD_DIM + 1] + jnp.exp2(sink_rows[kvh] - m)
                on = oe[0:HEAD_DIM] / denom
                for pair in range(Q_PER_KV // 2):
                    two = jnp.concatenate([on[:, (2 * pair) * BLK:(2 * pair + 1) * BLK],
                                           on[:, (2 * pair + 1) * BLK:(2 * pair + 2) * BLK]], axis=0)
                    outs.append(two.T)
            result = jnp.concatenate(outs, axis=1).astype(BF16)

            def commit():
                o_ref[pl.ds(q0, BLK), :] = result
            return commit

        return (scores if with_scores else None), weighted_values

    def attend(n, with_scores=True):
        if isinstance(n, int) and n == 0:
            return block(0, 0, [None, next_bias], with_scores)
        if isinstance(n, int) and n == nb - 1:
            return block(n * BLK, (n - 1) * BLK, [prev_bias, None], with_scores)
        return block(_aligned(n * BLK, BLK), _aligned((n - 1) * BLK, BLK), [prev_bias, None, next_bias], with_scores)

    return attend


def _log_sigmoid(x):
    return jnp.minimum(x, 0.0) - jnp.log(1.0 + jnp.exp(-jnp.abs(x)))


def _mlstm_chunks(qk_ref, vt_ref, mo_ref, mgt_ref, gb_ref, hg_ref, o_ref,
                  gt_s, b_s, w_s, mloc_s, tot_s, rcol_s, cloc_s, cprev_s, mprev_s, keep_s, add_s):
    S = qk_ref.shape[0]
    L = BLK
    nc = S // L
    ng = N_GATES
    sub = lax.broadcasted_iota(jnp.int32, (L, L), 0)
    lane = lax.broadcasted_iota(jnp.int32, (L, L), 1)

    for c in range(nc):
        gt_s[c * ng:(c + 1) * ng, :] = mgt_ref[:, c * L:(c + 1) * L] + gb_ref[...]
    gt = gt_s[...]
    row = lax.broadcasted_iota(jnp.int32, gt.shape, 0)
    gt = jnp.where(((row >> 2) & 1) == 1, _log_sigmoid(gt), gt)
    gt = gt * LOG2E
    upper = jnp.where(sub <= lane, 1.0, 0.0).astype(BF16)
    g1 = gt.astype(BF16)
    r1 = gt - g1.astype(F32)
    g2 = r1.astype(BF16)
    g3 = (r1 - g2.astype(F32)).astype(BF16)
    pre = _dot(g1, upper) + _dot(g2, upper) + _dot(g3, upper)
    tot = jnp.broadcast_to(pre[:, L - 1:L], pre.shape)
    bsum =jnp.where((row & (ng - 1)) < ng // 2, pre, tot - pre + gt)
    li = pltpu.roll(gt, M_HEADS, 0)
    a = tot - bsum + li
    mloc = jnp.broadcast_to(jnp.max(a, axis=1, keepdims=True), a.shape)
    b_s[...] = bsum
    w_s[...] = jnp.exp2(a - mloc)
    mloc_s[...] = mloc
    tot_s[...] = tot
    rdiff = li - bsum
    zpad = jnp.zeros((L - ng, L), F32)
    for c in range(nc):
        rcol_s[c * L:(c + 1) * L, :] = jnp.concatenate([rdiff[c * ng:(c + 1) * ng, :], zpad], axis=0).T

    def value_rows(hd, r0, dtype):
        vt = vt_ref[hd * HEAD_DIM:(hd + 1) * HEAD_DIM, pl.ds(r0, L)].astype(dtype)
        return jnp.concatenate([vt, jnp.ones((V_ROWS - HEAD_DIM, L), dtype)], axis=0)

    def local_state(c, carry):
        r0 = pl.multiple_of(c * L, L)
        g0 = pl.multiple_of(c * ng, ng)
        k = qk_ref[pl.ds(r0, L), M_W:2 * M_W]
        w = w_s[pl.ds(g0, ng), :]
        for hd in range(M_HEADS):
            vte = value_rows(hd, r0, F32)
            lhs = jnp.concatenate([vte * w[4 + hd:5 + hd, :], vte * w[12 + hd:13 + hd, :]], axis=0)
            cl = _dot(lhs.astype(BF16), k[:, (hd // 2) * LANES:(hd // 2 + 1) * LANES])
            cloc_s[0, hd, c] = cl[0:V_ROWS]
            cloc_s[1, hd, c] = cl[V_ROWS:2 * V_ROWS]
        return carry

    lax.fori_loop(0, nc, local_state, 0, unroll=8)

    jrow = lax.broadcasted_iota(jnp.int32, (ng, L), 0)

    def scan_stabilisers(i, m):
        cf = i
        cb = nc - 1 - i
        gf = pl.multiple_of(cf * ng, ng)
        gb = pl.multiple_of(cb * ng, ng)
        is_fwd = jrow < ng // 2
        blast = jnp.where(is_fwd, tot_s[pl.ds(gf, ng), :], tot_s[pl.ds(gb, ng), :])
        mloc_i = jnp.where(is_fwd, mloc_s[pl.ds(gf, ng), :], mloc_s[pl.ds(gb, ng), :])
        mprev_s[0, pl.ds(gf, ng), :] = m
        mprev_s[1, pl.ds(gb, ng), :] = m
        m_new = jnp.maximum(blast + m, mloc_i)
        gi = pl.multiple_of(i * ng, ng)
        keep_s[pl.ds(gi, ng), :] = jnp.exp2(blast + m - m_new)
        add_s[pl.ds(gi, ng), :] = jnp.exp2(mloc_i - m_new)
        return m_new

    lax.fori_loop(0, nc, scan_stabilisers, jnp.zeros((ng, L), F32))

    for h0 in range(0, M_HEADS, SCAN_HEADS):
        def scan_states(i, states, h0=h0):
            gi = pl.multiple_of(i * ng, ng)
            keep = keep_s[pl.ds(gi, ng), :]
            add = add_s[pl.ds(gi, ng), :]
            new_states = []
            for j in range(SCAN_HEADS):
                for d, ci in enumerate((i, nc - 1 - i)):
                    hd = h0 + j
                    r = 4 + hd + 8 * d
                    st = states[2 * j + d]
                    cprev_s[d, hd, ci] = st.astype(BF16)
                    new_states.append(keep[r:r + 1, :] * st + add[r:r + 1, :] * cloc_s[d, hd, ci])
            return tuple(new_states)

        lax.fori_loop(0, nc, scan_states, tuple(jnp.zeros((V_ROWS, LANES), F32) for _ in range(2 * SCAN_HEADS)))

    causal = (sub <= lane, sub >= lane)

    def outputs(c, with_scores=True):
        r0 = _aligned(c * L, L)
        g0 = _aligned(c * ng, ng)
        q = qk_ref[pl.ds(r0, L), 0:M_W]
        k = qk_ref[pl.ds(r0, L), M_W:2 * M_W]
        bc = b_s[pl.ds(g0, ng), :]
        mprev = (mprev_s[0, pl.ds(g0, ng), :], mprev_s[1, pl.ds(g0, ng), :])
        rc = rcol_s[pl.ds(r0, L), :]
        ys = []
        for hd in range(M_HEADS if with_scores else 0):
            t = hd // 2
            qp = q[:, t * LANES:(t + 1) * LANES]
            qm = jnp.where((lane >> 6) == (hd & 1), qp, jnp.zeros_like(qp))
            x = jnp.concatenate([k[:, t * LANES:(t + 1) * LANES], cprev_s[0, hd, c], cprev_s[1, hd, c]], axis=0)
            ys.append(_dot_nt(x, qm))

        def combine(ys):
            houts = []
            for hd in range(M_HEADS):
                y = ys[hd]
                st = y[0:L]
                pts, scs, mts = [], [], []
                for d in range(2):
                    r = 4 + hd + 8 * d
                    b_row = bc[r:r + 1, :]
                    dm = jnp.where(causal[d], rc[:, r:r + 1] + b_row, NEG)
                    inter = b_row + mprev[d][r:r + 1, :]
                    m_t = jnp.maximum(inter, jnp.max(dm, axis=0, keepdims=True))
                    pts.append((jnp.exp2(dm - m_t) * st).astype(BF16))
                    scs.append(jnp.exp2(inter - m_t))
                    mts.append(m_t)
                n2 = _dot(value_rows(hd, r0, BF16), jnp.concatenate(pts, axis=1))
                ht = None
                for d in range(2):
                    tt = n2[:, d * L:(d + 1) * L] + scs[d] * y[L + d * V_ROWS:L + (d + 1) * V_ROWS]
                    den = tt[HEAD_DIM:HEAD_DIM + 1, :]
                    hd_out = tt[0:HEAD_DIM] / jnp.maximum(jnp.abs(den), jnp.exp2(-mts[d]))
                    ht = hd_out if ht is None else ht + hd_out
                ms = jnp.mean(ht * ht, axis=0, keepdims=True)
                houts.append(ht * lax.rsqrt(ms + EPS))
            hn = (jnp.concatenate(houts, axis=0) * hg_ref[...]).T
            result = (jax.nn.sigmoid(mo_ref[pl.ds(r0, L), :].astype(F32)) * hn).astype(BF16)

            def commit():
                o_ref[pl.ds(r0, L), :] = result
            return commit

        return (ys if with_scores else None), combine

    return outputs


def _mixers_kernel(sink_ref, q_ref, kd_ref, vat_ref, qk_ref, mvt_ref, mo_ref, mgt_ref, gb_ref, hg_ref,
                   ya_ref, ym_ref, *scratch):
    *mlstm_scratch, s_buf, y_buf = scratch
    nb = q_ref.shape[0] // BLK
    n_groups = (nb - 2) // 2
    attend = _attn_blocks(sink_ref, q_ref, kd_ref, vat_ref, ya_ref)
    outputs = _mlstm_chunks(qk_ref, mvt_ref, mo_ref, mgt_ref, gb_ref, hg_ref, ym_ref, *mlstm_scratch)
    group = lambda g: (1 + 2 * g, 2 + 2 * g)

    def first(blocks):
        return [f(n)[0] for n in blocks for f in (attend, outputs)]

    def second(blocks, results):
        finishes = [f(n, with_scores=False)[1] for n in blocks for f in (attend, outputs)]
        commits = [finish(r) for finish, r in zip(finishes, results)]
        for commit in commits:
            commit()

    def park(slot, results):
        for b in range(2):
            for i, s in enumerate(results[2 * b]):
                s_buf[slot, N_KV_HEADS * b + i, 0:s.shape[0], :] = s
            for i, y in enumerate(results[2 * b + 1]):
                y_buf[slot, M_HEADS * b + i] = y

    def fetch(slot, key_blocks):
        results = []
        for b in range(2):
            results.append([s_buf[slot, N_KV_HEADS * b + i, 0:key_blocks * BLK, :] for i in range(N_KV_HEADS)])
            results.append([y_buf[slot, M_HEADS * b + i] for i in range(M_HEADS)])
        return results

    edges = (0, nb - 1)

    def advance(g, slot):
        results = first(group(g))
        if isinstance(g, int) and g == 0:
            second(edges, fetch(1 - slot, 2))
        else:
            second(group(g - 1), fetch(1 - slot, 3))
        park(slot, results)

    def two_steps(j, carry):
        advance(1 + 2 * j, 0)
        advance(2 + 2 * j, 1)
        return carry

    park(0, first(edges))
    advance(0, 1)
    lax.fori_loop(0, (n_groups - 1) // 2, two_steps, 0)
    last_slot = 1
    if (n_groups - 1) % 2:
        advance(n_groups - 1, 0)
        last_slot = 0
    second(group(n_groups - 1), fetch(last_slot, 3))


def _mixers(qa, kd, vat, sink, mqk, mvt, mo, mgt, gate_b, head_g, B, S):
    seq = lambda w: pl.BlockSpec((S, w), lambda b: (b, 0))
    tseq = lambda r: pl.BlockSpec((r, S), lambda b: (0, b))
    nc = S // BLK
    assert nc % 2 == 0 and nc >= 4
    gb = jnp.broadcast_to(gate_b.reshape(N_GATES, 1), (N_GATES, BLK))
    hg_t = jnp.broadcast_to(head_g.reshape(M_W, 1), (M_W, BLK))
    rows = nc * N_GATES
    return pl.pallas_call(
        _mixers_kernel,
        grid=(B,),
        in_specs=[pl.BlockSpec(memory_space=pltpu.SMEM), seq(ATTN_Q_W), seq(2 * ATTN_KV_W), tseq(ATTN_KV_W),
                  seq(2 * M_W), tseq(M_W), seq(M_W), tseq(N_GATES), _const_spec((N_GATES, BLK)),
                  _const_spec((M_W, BLK))],
        out_specs=[seq(ATTN_Q_W), seq(M_W)],
        out_shape=[jax.ShapeDtypeStruct((B * S, ATTN_Q_W), BF16), jax.ShapeDtypeStruct((B * S, M_W), BF16)],
        scratch_shapes=[pltpu.VMEM((rows, BLK), F32),
                        pltpu.VMEM((rows, BLK), F32),
                        pltpu.VMEM((rows, BLK), F32),
                        pltpu.VMEM((rows, BLK), F32),
                        pltpu.VMEM((rows, BLK), F32),
                        pltpu.VMEM((S, LANES), F32),
                        pltpu.VMEM((2, M_HEADS, nc, V_ROWS, LANES), F32),
                        pltpu.VMEM((2, M_HEADS, nc, V_ROWS, LANES), BF16),
                        pltpu.VMEM((2, rows, BLK), F32),
                        pltpu.VMEM((rows, BLK), F32),
                        pltpu.VMEM((rows, BLK), F32),
                        pltpu.VMEM((2, 2 * N_KV_HEADS, 3 * BLK, Q_PER_KV * BLK), F32),
                        pltpu.VMEM((2, 2 * M_HEADS, BLK + 2 * V_ROWS, BLK), F32)],
        compiler_params=_params(1),
        name="mixers",
    )(sink, qa, kd, vat, mqk, mvt, mo, mgt, gb, hg_t)


def kernel(x, positions, norm_ffn1_g, ffn1_w_gate, ffn1_w_up, ffn1_w_down, norm_mix_g, w_in, q_norm_g, k_norm_g, attn_sink, mlstm_gate_b, mlstm_head_g, sgu_norm_g, sgu_w_s, sgu_b, w_out, norm_ffn2_g, ffn2_w_gate, ffn2_w_up, ffn2_w_down, norm_out_g):
    B, S, D = x.shape
    depth = w_in.shape[0]
    T = B * S
    xt = x.reshape(T, D)
    cos, sin = _rope_tables(positions)
    group = np.arange(M_W) // HEAD_DIM
    hmat = jnp.asarray((group[:, None] == group[None, :]) / HEAD_DIM, BF16)
    w_in_t = jnp.swapaxes(w_in, 1, 2)
    for l in range(depth):
        xt = _ffn(xt, l, norm_ffn1_g[l], ffn1_w_gate, ffn1_w_up, ffn1_w_down)
        qa, kd, vat, mqk, mvt, mo, mgt, ys = _mix_in(xt, l, norm_mix_g[l], w_in_t, q_norm_g[l],
                                                     k_norm_g[l], cos, sin, hmat, sgu_norm_g[l], sgu_w_s[l], sgu_b[l])
        ya, ym = _mixers(qa, kd, vat, attn_sink[l], mqk, mvt, mo, mgt, mlstm_gate_b[l], mlstm_head_g[l], B, S)
        xt = _ffn(xt, l, norm_ffn2_g[l], ffn2_w_gate, ffn2_w_up, ffn2_w_down, proj=(ya, ym, ys, w_out),
                  gout=norm_out_g[l])
    return xt.reshape(B, S, D)
```

```python
import functools

import numpy as np
import jax
import jax.numpy as jnp
from jax import lax
from jax.experimental import pallas as pl
from jax.experimental.pallas import tpu as pltpu

F32 = jnp.float32
BF16 = jnp.bfloat16

D_MODEL = 1024
HEAD_DIM = 64
N_Q_HEADS = 8
N_KV_HEADS = 2
Q_PER_KV = N_Q_HEADS // N_KV_HEADS
ATTN_Q_W = N_Q_HEADS * HEAD_DIM
ATTN_KV_W = N_KV_HEADS * HEAD_DIM
WINDOW = 128
BLK = 128
ROPE_THETA = 10000.0
M_HEADS = 4
M_W = M_HEADS * HEAD_DIM
N_GATES = 4 * M_HEADS
V_ROWS = HEAD_DIM + 16
SCAN_HEADS = 2
SGU_GROUPS = 4
SGU_W = SGU_GROUPS * HEAD_DIM
D_FF = 2816
EPS = 1e-6
LANES = 128
NEG = -1e30
LOG2E = float(np.log2(np.e))

VMEM_LIMIT = 56 * 1024 * 1024
FFN_TM = 512
FFN_SPLIT = 2
W_CHUNKS = 8
W_SLOTS = 3
W_OUT_CHUNKS = 4
W_IN_CHUNKS = 5
MIX_TM = 1024
MIX_SPLIT = 1


def _dot(a, b):
    return jnp.dot(a, b, preferred_element_type=F32)


def _dot_nt(a, b):
    return lax.dot_general(a, b, (((1,), (1,)), ((), ())), preferred_element_type=F32)


def _dot_tn(a, b):
    return lax.dot_general(a, b, (((0,), (0,)), ((), ())), preferred_element_type=F32)


def _aligned(x, m):
    return x if isinstance(x, int) else pl.multiple_of(x, m)


def _rms(x, g):
    ms = jnp.mean(x * x, axis=-1, keepdims=True)
    return x * lax.rsqrt(ms + EPS) * g


def _group_mean_sq(t, hmat):
    return _dot((t * t).astype(BF16), hmat)


def _const_spec(shape):
    nd = len(shape)
    return pl.BlockSpec(shape, lambda *_: (0,) * nd, pipeline_mode=pl.Buffered(1))


def _layer_spec(stacked_shape, layer):
    nd = len(stacked_shape) - 1
    return pl.BlockSpec((None,) + tuple(stacked_shape[1:]), lambda *_: (layer,) + (0,) * nd,
                        pipeline_mode=pl.Buffered(1))


def _params(n_grid):
    return pltpu.CompilerParams(dimension_semantics=("parallel",) * n_grid,
                                vmem_limit_bytes=VMEM_LIMIT)


ROPE_HALF = HEAD_DIM // 2
ROPE_PACK = LANES // ROPE_HALF


def _rope_kernel(pos_ref, freq_ref, cos_ref, sin_ref):
    ang = pos_ref[...] * freq_ref[...]
    rows = ang.shape[0]
    token = lax.broadcasted_iota(jnp.int32, ang.shape, 1) >> 5
    for table, out_ref in ((jnp.cos(ang), cos_ref), (jnp.sin(ang), sin_ref)):
        for i in range(ROPE_PACK):
            x = jnp.where(token == i, table, 0.0)
            y = x + pltpu.roll(x, 2 * ROPE_HALF, 1)
            out_ref[pl.ds(i, rows, stride=ROPE_PACK), :] = y + pltpu.roll(y, ROPE_HALF, 1)


def _rope_inputs(positions):
    T = positions.size
    freqs = ROPE_THETA ** (-jnp.arange(0, HEAD_DIM, 2, dtype=F32) / HEAD_DIM)
    pos = jnp.repeat(positions.reshape(T // ROPE_PACK, ROPE_PACK).astype(F32), ROPE_HALF, axis=1)
    return pos, jnp.tile(freqs, ROPE_PACK).reshape(1, LANES)


def _load_weight_bf16(w_hbm, layer, dst, stage, sems):
    n_slots, rows, cols = stage.shape
    n_chunks = dst.shape[0] // rows

    def copy(c):
        slot = c % n_slots
        return pltpu.make_async_copy(w_hbm.at[layer, pl.ds(c * rows, rows), pl.ds(0, cols)], stage.at[slot],
                                     sems.at[slot])

    for c in range(min(n_slots - 1, n_chunks)):
        copy(c).start()
    for c in range(n_chunks):
        if c + n_slots - 1 < n_chunks:
            copy(c + n_slots - 1).start()
        copy(c).wait()
        dst[c * rows:(c + 1) * rows, :] = stage[c % n_slots].astype(BF16)


def _ffn_kernel(*refs, layer, has_rope, has_proj, has_final):
    refs = list(refs)
    x_ref = refs.pop(0)
    if has_rope:
        pos_ref, freq_ref = refs.pop(0), refs.pop(0)
    if has_proj:
        ya_ref, ym_ref, ys_ref, wo_hbm = refs[:4]
        refs = refs[4:]
    g_ref, wg_hbm, wu_hbm, wd_hbm = refs[:4]
    refs = refs[4:]
    if has_final:
        gout_ref = refs.pop(0)
    out_ref = refs.pop(0)
    if has_rope:
        cos_ref, sin_ref = refs.pop(0), refs.pop(0)
    wg_ref, wu_ref, wd_ref, stage_in, stage_out, sems = refs[:6]
    if has_proj:
        wo_ref = refs[6]

    @pl.when(pl.program_id(0) == 0)
    def _():
        if has_proj:
            rows = wo_ref.shape[0] // W_OUT_CHUNKS
            _load_weight_bf16(wo_hbm, layer, wo_ref, stage_out.at[:, 0:rows, :], sems)
        _load_weight_bf16(wg_hbm, layer, wg_ref, stage_in, sems)
        _load_weight_bf16(wu_hbm, layer, wu_ref, stage_in, sems)
        _load_weight_bf16(wd_hbm, layer, wd_ref, stage_out, sems)

    hs = x_ref.shape[0] // FFN_SPLIT
    rows = [slice(i * hs, (i + 1) * hs) for i in range(FFN_SPLIT)]
    xs = []
    for r in rows:
        x = x_ref[r, :]
        if has_proj:
            x = (x + _dot(ya_ref[r, :], wo_ref[0:ATTN_Q_W, :])
                 + _dot(ym_ref[r, :], wo_ref[ATTN_Q_W:ATTN_Q_W + M_W, :])
                 + _dot(ys_ref[r, :], wo_ref[ATTN_Q_W + M_W:, :]))
        xs.append(x)
    hidden = [_rms(x, g_ref[...]).astype(BF16) for x in xs]
    gate_up = [(_dot(h, wg_ref[...]), _dot(h, wu_ref[...])) for h in hidden]
    for r, x, (gate, up) in zip(rows, xs, gate_up):
        act = (gate * jax.nn.sigmoid(gate) * up).astype(BF16)
        y = x + 0.5 * _dot(act, wd_ref[...])
        if has_final:
            y = _rms(y, gout_ref[...])
        out_ref[r, :] = y

    if has_rope:
        _rope_kernel(pos_ref, freq_ref, cos_ref, sin_ref)


def _ffn(x, layer, g, wg, wu, wd, rope=None, proj=None, gout=None, tm=FFN_TM):
    T = x.shape[0]
    d_model, d_ff = wg.shape[1:]
    row = lambda w: pl.BlockSpec((tm, w), lambda i: (i, 0))
    hbm = pl.BlockSpec(memory_space=pl.ANY)
    args, specs = [x], [row(D_MODEL)]
    out_specs, out_shape = row(D_MODEL), jax.ShapeDtypeStruct((T, D_MODEL), F32)
    if rope is not None:
        args += list(rope)
        specs += [pl.BlockSpec((tm // ROPE_PACK, LANES), lambda i: (i, 0)), _const_spec((1, LANES))]
        out_specs = [out_specs, row(LANES), row(LANES)]
        out_shape = [out_shape] + [jax.ShapeDtypeStruct((T, LANES), F32)] * 2
    if proj is not None:
        ya, ym, ys, wo = proj
        args += [ya, ym, ys, wo]
        specs += [row(ATTN_Q_W), row(M_W), row(SGU_W), hbm]
    args += [g.reshape(1, D_MODEL), wg, wu, wd]
    specs += [_const_spec((1, D_MODEL)), hbm, hbm, hbm]
    if gout is not None:
        args.append(gout.reshape(1, D_MODEL))
        specs.append(_const_spec((1, D_MODEL)))
    return pl.pallas_call(
        functools.partial(_ffn_kernel, layer=layer, has_rope=rope is not None, has_proj=proj is not None,
                          has_final=gout is not None),
        grid=(T // tm,),
        in_specs=specs,
        out_specs=out_specs,
        out_shape=out_shape,
        scratch_shapes=[pltpu.VMEM((d_model, d_ff), BF16), pltpu.VMEM((d_model, d_ff), BF16),
                        pltpu.VMEM((d_ff, d_model), BF16),
                        pltpu.VMEM((W_SLOTS, d_model // W_CHUNKS, d_ff), F32),
                        pltpu.VMEM((W_SLOTS, d_ff // W_CHUNKS, d_model), F32),
                        pltpu.SemaphoreType.DMA((W_SLOTS,))]
        + ([pltpu.VMEM(proj[3].shape[1:], BF16)] if proj is not None else []),
        compiler_params=pltpu.CompilerParams(dimension_semantics=("arbitrary",), vmem_limit_bytes=VMEM_LIMIT),
        name="ffn",
    )(*args)


def _first_half(shape):
    lane = lax.broadcasted_iota(jnp.int32, shape, 1)
    return (lane & (HEAD_DIM - 1)) < HEAD_DIM // 2


def _rope(t, cos, sin_signed):
    n = t.shape[-1]
    reps = n // LANES
    swapped = jnp.where(_first_half(t.shape),
                        pltpu.roll(t, n - HEAD_DIM // 2, 1), pltpu.roll(t, HEAD_DIM // 2, 1))
    return t * jnp.tile(cos, (1, reps)) + swapped * jnp.tile(sin_signed, (1, reps))


def _gelu(x):
    return 0.5 * x * (1.0 + lax.erf(x * np.float32(np.sqrt(0.5))))


def _mixin_kernel(x_ref, g_ref, w_hbm, qg_ref, kg_ref, cos_ref, sin_ref, hmat_ref, ng_ref, ws_ref,
                  bs_ref, qa_ref, kd_ref, vat_ref, mqk_ref, mvt_ref, mo_ref, mgt_ref, ys_ref, w_ref, stage, sems,
                  *, layer):
    @pl.when(pl.program_id(0) == 0)
    def _():
        _load_weight_bf16(w_hbm, layer, w_ref, stage, sems)

    scale = HEAD_DIM ** -0.5
    hmat = hmat_ref[...]
    gw = hmat.shape[0]
    c_q, c_kv, c_mq, c_mk, c_mv, c_mo, c_g, c_su, c_sv = (int(c) for c in np.cumsum(
        [0, ATTN_Q_W, 2 * ATTN_KV_W, M_W, M_W, M_W, M_W, N_GATES, SGU_W]))
    lane_group = lax.broadcasted_iota(jnp.int32, (BLK, SGU_W), 1) >> 6

    def projections(r0, nrows):
        r = slice(r0, r0 + nrows)
        h = _rms(x_ref[r, :], g_ref[...]).astype(BF16)
        proj = lambda start, width: _dot_nt(h, w_ref[start:start + width, :])
        su = proj(c_su, SGU_W)
        sv = proj(c_sv, SGU_W)
        aq = proj(c_q, ATTN_Q_W)
        akv = proj(c_kv, 2 * ATTN_KV_W)
        ak, av = akv[:, :ATTN_KV_W], akv[:, ATTN_KV_W:]
        v = _gelu(sv)
        v_ms = _group_mean_sq(v, hmat)
        mqk_ref[r, 0:M_W] = proj(c_mq, M_W).astype(BF16)
        mqk_ref[r, M_W:2 * M_W] = (proj(c_mk, M_W) * scale).astype(BF16)
        q_ms = jnp.concatenate([_group_mean_sq(aq[:, i:i + gw], hmat) for i in range(0, ATTN_Q_W, gw)], axis=1)
        k_ms = _group_mean_sq(ak, hmat[0:ATTN_KV_W, 0:ATTN_KV_W])
        mvt_ref[:, r] = proj(c_mv, M_W).T.astype(BF16)
        mo_ref[r, :] = proj(c_mo, M_W).astype(BF16)
        mgt_ref[:, r] = proj(c_g, LANES).T[0:N_GATES, :]

        def tail():
            cos = cos_ref[r, :]
            sin_signed = jnp.where(_first_half(cos.shape), -sin_ref[r, :], sin_ref[r, :])
            qn = aq * lax.rsqrt(q_ms + EPS) * qg_ref[...]
            qa_ref[r, :] = (_rope(qn, cos, sin_signed) * (scale * LOG2E)).astype(BF16)
            kn = ak * lax.rsqrt(k_ms + EPS) * kg_ref[...]
            kr = _rope(kn, cos, sin_signed)
            ks = pltpu.roll(kr, HEAD_DIM, 1)
            low = lax.broadcasted_iota(jnp.int32, kr.shape, 1) < HEAD_DIM
            kd_ref[r, 0:LANES] = jnp.where(low, kr, ks).astype(BF16)
            kd_ref[r, LANES:2 * LANES] = jnp.where(low, ks, kr).astype(BF16)
            vat_ref[:, r] = av.T.astype(BF16)
            u = _gelu(su)
            vn = (v * lax.rsqrt(v_ms + EPS) * ng_ref[...]).astype(BF16)
            for c in range(nrows // BLK):
                vc = vn[c * BLK:(c + 1) * BLK]
                stack = jnp.concatenate(
                    [jnp.where(lane_group == g, vc, jnp.zeros_like(vc)) for g in range(SGU_GROUPS)], axis=0)
                mixed = _dot(ws_ref[...], stack)
                ys_ref[r0 + c * BLK:r0 + (c + 1) * BLK, :] = (
                    u[c * BLK:(c + 1) * BLK] * (mixed + bs_ref[...])).astype(BF16)

        return tail

    nrows = x_ref.shape[0] // MIX_SPLIT
    tails = [projections(i * nrows, nrows) for i in range(MIX_SPLIT)]
    for tail in tails:
        tail()


def _mix_in(x, layer, g, w_in_t, q_g, k_g, cos, sin, hmat, sgu_g, sgu_w, sgu_b, tm=MIX_TM):
    T = x.shape[0]
    d_in, d_model = w_in_t.shape[1:]
    chunk = d_in // W_IN_CHUNKS
    assert chunk * W_IN_CHUNKS == d_in and chunk % 16 == 0
    row = lambda w: pl.BlockSpec((tm, w), lambda i: (i, 0))
    tcol = lambda r: pl.BlockSpec((r, tm), lambda i: (0, i))
    out_specs = [row(ATTN_Q_W), row(2 * ATTN_KV_W), tcol(ATTN_KV_W), row(2 * M_W), tcol(M_W), row(M_W),
                 tcol(N_GATES), row(SGU_W)]
    out_shape = [jax.ShapeDtypeStruct(s, d) for s, d in [
        ((T, ATTN_Q_W), BF16), ((T, 2 * ATTN_KV_W), BF16), ((ATTN_KV_W, T), BF16), ((T, 2 * M_W), BF16),
        ((M_W, T), BF16), ((T, M_W), BF16), ((N_GATES, T), F32), ((T, SGU_W), BF16)]]
    ws_cat = jnp.transpose(sgu_w, (1, 0, 2)).reshape(BLK, SGU_GROUPS * BLK).astype(BF16)
    bs_full = jnp.repeat(jnp.transpose(sgu_b), HEAD_DIM, axis=1)
    return pl.pallas_call(
        functools.partial(_mixin_kernel, layer=layer),
        grid=(T // tm,),
        in_specs=[row(D_MODEL), _const_spec((1, D_MODEL)), pl.BlockSpec(memory_space=pl.ANY),
                  _const_spec((1, ATTN_Q_W)), _const_spec((1, ATTN_KV_W)),
                  row(LANES), row(LANES), _const_spec(hmat.shape),
                  _const_spec((1, SGU_W)), _const_spec(ws_cat.shape), _const_spec(bs_full.shape)],
        out_specs=out_specs,
        out_shape=out_shape,
        scratch_shapes=[pltpu.VMEM((d_in, d_model), BF16),
                        pltpu.VMEM((W_SLOTS, chunk, d_model), F32),
                        pltpu.SemaphoreType.DMA((W_SLOTS,))],
        compiler_params=pltpu.CompilerParams(dimension_semantics=("arbitrary",), vmem_limit_bytes=VMEM_LIMIT),
        name="mix_in",
    )(x, g.reshape(1, D_MODEL), w_in_t,
      jnp.tile(q_g, N_Q_HEADS).reshape(1, ATTN_Q_W), jnp.tile(k_g, N_KV_HEADS).reshape(1, ATTN_KV_W),
      cos, sin, hmat, sgu_g.reshape(1, SGU_W), ws_cat, bs_full)


def _attn_blocks(sink_ref, q_ref, kd_ref, vt_ref, o_ref):
    S = q_ref.shape[0]
    nb = S // BLK
    kc = lax.broadcasted_iota(jnp.int32, (BLK, BLK), 0)
    qi = lax.broadcasted_iota(jnp.int32, (BLK, BLK), 1)
    prev_bias = jnp.tile(jnp.where(kc >= qi, 0.0, NEG), (1, Q_PER_KV))
    next_bias = jnp.tile(jnp.where(kc <= qi, 0.0, NEG), (1, Q_PER_KV))
    half_mask = (qi < HEAD_DIM, qi >= HEAD_DIM)
    head_lane = lax.broadcasted_iota(jnp.int32, (1, Q_PER_KV * BLK), 1) >> 7
    sink_rows = []
    for kvh in range(N_KV_HEADS):
        row = jnp.zeros((1, Q_PER_KV * BLK), F32)
        for g in range(Q_PER_KV):
            row = jnp.where(head_lane == g, sink_ref[kvh * Q_PER_KV + g] * LOG2E, row)
        sink_rows.append(row)

    def block(q0, start, biases, with_scores):
        nk = len(biases)
        scores = []
        for kvh in range(N_KV_HEADS if with_scores else 0):
            kd = kd_ref[pl.ds(start, nk * BLK), kvh * LANES:(kvh + 1) * LANES]
            qs = []
            for g in range(Q_PER_KV):
                t = kvh * Q_PER_KV + g
                qt = q_ref[pl.ds(q0, BLK), (t // 2) * LANES:(t // 2 + 1) * LANES]
                qs.append(jnp.where(half_mask[t & 1], qt, jnp.zeros_like(qt)))
            scores.append(_dot_nt(kd, jnp.concatenate(qs, axis=0)))

        def weighted_values(scores):
            ones = jnp.ones((HEAD_DIM, nk * BLK), BF16)
            outs = []
            for kvh in range(N_KV_HEADS):
                v_ext = jnp.concatenate([vt_ref[kvh * HEAD_DIM:(kvh + 1) * HEAD_DIM, pl.ds(start, nk * BLK)], ones],
                                        axis=0)
                parts = [scores[kvh][j * BLK:(j + 1) * BLK] for j in range(nk)]
                parts = [p if b is None else p + b for p, b in zip(parts, biases)]
                mx = jnp.max(functools.reduce(jnp.maximum, parts), axis=0, keepdims=True)
                m = jnp.maximum(mx, sink_rows[kvh])
                p = jnp.concatenate([jnp.exp2(p - m) for p in parts], axis=0).astype(BF16)
                oe = _dot(v_ext, p)
                denom = oe[HEAD_DIM:HEAD_DIM + 1] + jnp.exp2(sink_rows[kvh] - m)
                on = oe[0:HEAD_DIM] / denom
                for pair in range(Q_PER_KV // 2):
                    two = jnp.concatenate([on[:, (2 * pair) * BLK:(2 * pair + 1) * BLK],
                                           on[:, (2 * pair + 1) * BLK:(2 * pair + 2) * BLK]], axis=0)
                    outs.append(two.T)
            result = jnp.concatenate(outs, axis=1).astype(BF16)

            def commit():
                o_ref[pl.ds(q0, BLK), :] = result
            return commit

        return (scores if with_scores else None), weighted_values

    def attend(n, with_scores=True):
        if isinstance(n, int) and n == 0:
            return block(0, 0, [None, next_bias], with_scores)
        if isinstance(n, int) and n == nb - 1:
            return block(n * BLK, (n - 1) * BLK, [prev_bias, None], with_scores)
        return block(_aligned(n * BLK, BLK), _aligned((n - 1) * BLK, BLK), [prev_bias, None, next_bias], with_scores)

    return attend


def _log_sigmoid(x):
    return jnp.minimum(x, 0.0) - jnp.log(1.0 + jnp.exp(-jnp.abs(x)))


def _mlstm_chunks(qk_ref, vt_ref, mo_ref, mgt_ref, gb_ref, hg_ref, o_ref,
                  gt_s, b_s, w_s, mloc_s, tot_s, rcol_s, cloc_s, cprev_s, mprev_s, keep_s, add_s):
    S = qk_ref.shape[0]
    L = BLK
    nc = S // L
    ng = N_GATES
    sub = lax.broadcasted_iota(jnp.int32, (L, L), 0)
    lane = lax.broadcasted_iota(jnp.int32, (L, L), 1)

    for c in range(nc):
        gt_s[c * ng:(c + 1) * ng, :] = mgt_ref[:, c * L:(c + 1) * L] + gb_ref[...]
    gt = gt_s[...]
    row = lax.broadcasted_iota(jnp.int32, gt.shape, 0)
    gt = jnp.where(((row >> 2) & 1) == 1, _log_sigmoid(gt), gt)
    gt = gt * LOG2E
    upper = jnp.where(sub <= lane, 1.0, 0.0).astype(BF16)
    g1 = gt.astype(BF16)
    r1 = gt - g1.astype(F32)
    g2 = r1.astype(BF16)
    g3 = (r1 - g2.astype(F32)).astype(BF16)
    pre = _dot(g1, upper) + _dot(g2, upper) + _dot(g3, upper)
    tot = jnp.broadcast_to(pre[:, L - 1:L], pre.shape)
    bsum =jnp.where((row & (ng - 1)) < ng // 2, pre, tot - pre + gt)
    li = pltpu.roll(gt, M_HEADS, 0)
    a = tot - bsum + li
    mloc = jnp.broadcast_to(jnp.max(a, axis=1, keepdims=True), a.shape)
    b_s[...] = bsum
    w_s[...] = jnp.exp2(a - mloc)
    mloc_s[...] = mloc
    tot_s[...] = tot
    rdiff = li - bsum
    zpad = jnp.zeros((L - ng, L), F32)
    for c in range(nc):
        rcol_s[c * L:(c + 1) * L, :] = jnp.concatenate([rdiff[c * ng:(c + 1) * ng, :], zpad], axis=0).T

    def value_rows(hd, r0, dtype):
        vt = vt_ref[hd * HEAD_DIM:(hd + 1) * HEAD_DIM, pl.ds(r0, L)].astype(dtype)
        return jnp.concatenate([vt, jnp.ones((V_ROWS - HEAD_DIM, L), dtype)], axis=0)

    def local_state(c, carry):
        r0 = pl.multiple_of(c * L, L)
        g0 = pl.multiple_of(c * ng, ng)
        k = qk_ref[pl.ds(r0, L), M_W:2 * M_W]
        w = w_s[pl.ds(g0, ng), :]
        for hd in range(M_HEADS):
            vte = value_rows(hd, r0, F32)
            lhs = jnp.concatenate([vte * w[4 + hd:5 + hd, :], vte * w[12 + hd:13 + hd, :]], axis=0)
            cl = _dot(lhs.astype(BF16), k[:, (hd // 2) * LANES:(hd // 2 + 1) * LANES])
            cloc_s[0, hd, c] = cl[0:V_ROWS]
            cloc_s[1, hd, c] = cl[V_ROWS:2 * V_ROWS]
        return carry

    lax.fori_loop(0, nc, local_state, 0, unroll=8)

    jrow = lax.broadcasted_iota(jnp.int32, (ng, L), 0)

    def scan_stabilisers(i, m):
        cf = i
        cb = nc - 1 - i
        gf = pl.multiple_of(cf * ng, ng)
        gb = pl.multiple_of(cb * ng, ng)
        is_fwd = jrow < ng // 2
        blast = jnp.where(is_fwd, tot_s[pl.ds(gf, ng), :], tot_s[pl.ds(gb, ng), :])
        mloc_i = jnp.where(is_fwd, mloc_s[pl.ds(gf, ng), :], mloc_s[pl.ds(gb, ng), :])
        mprev_s[0, pl.ds(gf, ng), :] = m
        mprev_s[1, pl.ds(gb, ng), :] = m
        m_new = jnp.maximum(blast + m, mloc_i)
        gi = pl.multiple_of(i * ng, ng)
        keep_s[pl.ds(gi, ng), :] = jnp.exp2(blast + m - m_new)
        add_s[pl.ds(gi, ng), :] = jnp.exp2(mloc_i - m_new)
        return m_new

    lax.fori_loop(0, nc, scan_stabilisers, jnp.zeros((ng, L), F32))

    for h0 in range(0, M_HEADS, SCAN_HEADS):
        def scan_states(i, states, h0=h0):
            gi = pl.multiple_of(i * ng, ng)
            keep = keep_s[pl.ds(gi, ng), :]
            add = add_s[pl.ds(gi, ng), :]
            new_states = []
            for j in range(SCAN_HEADS):
                for d, ci in enumerate((i, nc - 1 - i)):
                    hd = h0 + j
                    r = 4 + hd + 8 * d
                    st = states[2 * j + d]
                    cprev_s[d, hd, ci] = st.astype(BF16)
                    new_states.append(keep[r:r + 1, :] * st + add[r:r + 1, :] * cloc_s[d, hd, ci])
            return tuple(new_states)

        lax.fori_loop(0, nc, scan_states, tuple(jnp.zeros((V_ROWS, LANES), F32) for _ in range(2 * SCAN_HEADS)))

    causal = (sub <= lane, sub >= lane)

    def outputs(c, with_scores=True):
        r0 = _aligned(c * L, L)
        g0 = _aligned(c * ng, ng)
        q = qk_ref[pl.ds(r0, L), 0:M_W]
        k = qk_ref[pl.ds(r0, L), M_W:2 * M_W]
        bc = b_s[pl.ds(g0, ng), :]
        mprev = (mprev_s[0, pl.ds(g0, ng), :], mprev_s[1, pl.ds(g0, ng), :])
        rc = rcol_s[pl.ds(r0, L), :]
        ys = []
        for hd in range(M_HEADS if with_scores else 0):
            t = hd // 2
            qp = q[:, t * LANES:(t + 1) * LANES]
            qm = jnp.where((lane >> 6) == (hd & 1), qp, jnp.zeros_like(qp))
            x = jnp.concatenate([k[:, t * LANES:(t + 1) * LANES], cprev_s[0, hd, c], cprev_s[1, hd, c]], axis=0)
            ys.append(_dot_nt(x, qm))

        def combine(ys):
            houts = []
            for hd in range(M_HEADS):
                y = ys[hd]
                st = y[0:L]
                pts, scs, mts = [], [], []
                for d in range(2):
                    r = 4 + hd + 8 * d
                    b_row = bc[r:r + 1, :]
                    dm = jnp.where(causal[d], rc[:, r:r + 1] + b_row, NEG)
                    inter = b_row + mprev[d][r:r + 1, :]
                    m_t = jnp.maximum(inter, jnp.max(dm, axis=0, keepdims=True))
                    pts.append((jnp.exp2(dm - m_t) * st).astype(BF16))
                    scs.append(jnp.exp2(inter - m_t))
                    mts.append(m_t)
                n2 = _dot(value_rows(hd, r0, BF16), jnp.concatenate(pts, axis=1))
                ht = None
                for d in range(2):
                    tt = n2[:, d * L:(d + 1) * L] + scs[d] * y[L + d * V_ROWS:L + (d + 1) * V_ROWS]
                    den = tt[HEAD_DIM:HEAD_DIM + 1, :]
                    hd_out = tt[0:HEAD_DIM] / jnp.maximum(jnp.abs(den), jnp.exp2(-mts[d]))
                    ht = hd_out if ht is None else ht + hd_out
                ms = jnp.mean(ht * ht, axis=0, keepdims=True)
                houts.append(ht * lax.rsqrt(ms + EPS))
            hn = (jnp.concatenate(houts, axis=0) * hg_ref[...]).T
            result = (jax.nn.sigmoid(mo_ref[pl.ds(r0, L), :].astype(F32)) * hn).astype(BF16)

            def commit():
                o_ref[pl.ds(r0, L), :] = result
            return commit

        return (ys if with_scores else None), combine

    return outputs


def _mixers_kernel(sink_ref, q_ref, kd_ref, vat_ref, qk_ref, mvt_ref, mo_ref, mgt_ref, gb_ref, hg_ref,
                   ya_ref, ym_ref, *scratch):
    *mlstm_scratch, s_buf, y_buf = scratch
    nb = q_ref.shape[0] // BLK
    n_groups = (nb - 2) // 2
    attend = _attn_blocks(sink_ref, q_ref, kd_ref, vat_ref, ya_ref)
    outputs = _mlstm_chunks(qk_ref, mvt_ref, mo_ref, mgt_ref, gb_ref, hg_ref, ym_ref, *mlstm_scratch)
    group = lambda g: (1 + 2 * g, 2 + 2 * g)

    def first(blocks):
        return [f(n)[0] for n in blocks for f in (attend, outputs)]

    def second(blocks, results):
        finishes = [f(n, with_scores=False)[1] for n in blocks for f in (attend, outputs)]
        commits = [finish(r) for finish, r in zip(finishes, results)]
        for commit in commits:
            commit()

    def park(slot, results):
        for b in range(2):
            for i, s in enumerate(results[2 * b]):
                s_buf[slot, N_KV_HEADS * b + i, 0:s.shape[0], :] = s
            for i, y in enumerate(results[2 * b + 1]):
                y_buf[slot, M_HEADS * b + i] = y

    def fetch(slot, key_blocks):
        results = []
        for b in range(2):
            results.append([s_buf[slot, N_KV_HEADS * b + i, 0:key_blocks * BLK, :] for i in range(N_KV_HEADS)])
            results.append([y_buf[slot, M_HEADS * b + i] for i in range(M_HEADS)])
        return results

    edges = (0, nb - 1)

    def advance(g, slot):
        results = first(group(g))
        if isinstance(g, int) and g == 0:
            second(edges, fetch(1 - slot, 2))
        else:
            second(group(g - 1), fetch(1 - slot, 3))
        park(slot, results)

    def two_steps(j, carry):
        advance(1 + 2 * j, 0)
        advance(2 + 2 * j, 1)
        return carry

    park(0, first(edges))
    advance(0, 1)
    lax.fori_loop(0, (n_groups - 1) // 2, two_steps, 0)
    last_slot = 1
    if (n_groups - 1) % 2:
        advance(n_groups - 1, 0)
        last_slot = 0
    second(group(n_groups - 1), fetch(last_slot, 3))


def _mixers(qa, kd, vat, sink, mqk, mvt, mo, mgt, gate_b, head_g, B, S):
    seq = lambda w: pl.BlockSpec((S, w), lambda b: (b, 0))
    tseq = lambda r: pl.BlockSpec((r, S), lambda b: (0, b))
    nc = S // BLK
    assert nc % 2 == 0 and nc >= 4
    gb = jnp.broadcast_to(gate_b.reshape(N_GATES, 1), (N_GATES, BLK))
    hg_t = jnp.broadcast_to(head_g.reshape(M_W, 1), (M_W, BLK))
    rows = nc * N_GATES
    return pl.pallas_call(
        _mixers_kernel,
        grid=(B,),
        in_specs=[pl.BlockSpec(memory_space=pltpu.SMEM), seq(ATTN_Q_W), seq(2 * ATTN_KV_W), tseq(ATTN_KV_W),
                  seq(2 * M_W), tseq(M_W), seq(M_W), tseq(N_GATES), _const_spec((N_GATES, BLK)),
                  _const_spec((M_W, BLK))],
        out_specs=[seq(ATTN_Q_W), seq(M_W)],
        out_shape=[jax.ShapeDtypeStruct((B * S, ATTN_Q_W), BF16), jax.ShapeDtypeStruct((B * S, M_W), BF16)],
        scratch_shapes=[pltpu.VMEM((rows, BLK), F32),
                        pltpu.VMEM((rows, BLK), F32),
                        pltpu.VMEM((rows, BLK), F32),
                        pltpu.VMEM((rows, BLK), F32),
                        pltpu.VMEM((rows, BLK), F32),
                        pltpu.VMEM((S, LANES), F32),
                        pltpu.VMEM((2, M_HEADS, nc, V_ROWS, LANES), F32),
                        pltpu.VMEM((2, M_HEADS, nc, V_ROWS, LANES), BF16),
                        pltpu.VMEM((2, rows, BLK), F32),
                        pltpu.VMEM((rows, BLK), F32),
                        pltpu.VMEM((rows, BLK), F32),
                        pltpu.VMEM((2, 2 * N_KV_HEADS, 3 * BLK, Q_PER_KV * BLK), F32),
                        pltpu.VMEM((2, 2 * M_HEADS, BLK + 2 * V_ROWS, BLK), F32)],
        compiler_params=_params(1),
        name="mixers",
    )(sink, qa, kd, vat, mqk, mvt, mo, mgt, gb, hg_t)


def kernel(x, positions, norm_ffn1_g, ffn1_w_gate, ffn1_w_up, ffn1_w_down, norm_mix_g, w_in, q_norm_g, k_norm_g, attn_sink, mlstm_gate_b, mlstm_head_g, sgu_norm_g, sgu_w_s, sgu_b, w_out, norm_ffn2_g, ffn2_w_gate, ffn2_w_up, ffn2_w_down, norm_out_g):
    B, S, D = x.shape
    depth = w_in.shape[0]
    T = B * S
    xt = x.reshape(T, D)
    group = np.arange(M_W) // HEAD_DIM
    hmat = jnp.asarray((group[:, None] == group[None, :]) / HEAD_DIM, BF16)
    w_in_t = jnp.swapaxes(w_in, 1, 2)
    for l in range(depth):
        if l == 0:
            xt, cos, sin = _ffn(xt, l, norm_ffn1_g[l], ffn1_w_gate, ffn1_w_up, ffn1_w_down,
                                rope=_rope_inputs(positions))
        else:
            xt = _ffn(xt, l, norm_ffn1_g[l], ffn1_w_gate, ffn1_w_up, ffn1_w_down)
        qa, kd, vat, mqk, mvt, mo, mgt, ys = _mix_in(xt, l, norm_mix_g[l], w_in_t, q_norm_g[l],
                                                     k_norm_g[l], cos, sin, hmat, sgu_norm_g[l], sgu_w_s[l], sgu_b[l])
        ya, ym = _mixers(qa, kd, vat, attn_sink[l], mqk, mvt, mo, mgt, mlstm_gate_b[l], mlstm_head_g[l], B, S)
        xt = _ffn(xt, l, norm_ffn2_g[l], ffn2_w_gate, ffn2_w_up, ffn2_w_down, proj=(ya, ym, ys, w_out),
                  gout=norm_out_g[l])
    return xt.reshape(B, S, D)
```

```python
import functools

import numpy as np
import jax
import jax.numpy as jnp
from jax import lax
from jax.experimental import pallas as pl
from jax.experimental.pallas import tpu as pltpu

F32 = jnp.float32
BF16 = jnp.bfloat16

D_MODEL = 1024
HEAD_DIM = 64
N_Q_HEADS = 8
N_KV_HEADS = 2
Q_PER_KV = N_Q_HEADS // N_KV_HEADS
ATTN_Q_W = N_Q_HEADS * HEAD_DIM
ATTN_KV_W = N_KV_HEADS * HEAD_DIM
WINDOW = 128
BLK = 128
ROPE_THETA = 10000.0
M_HEADS = 4
M_W = M_HEADS * HEAD_DIM
N_GATES = 4 * M_HEADS
V_ROWS = HEAD_DIM + 16
SCAN_HEADS = 2
SGU_GROUPS = 4
SGU_W = SGU_GROUPS * HEAD_DIM
EPS = 1e-6
LANES = 128
NEG = -1e30
LOG2E = float(np.log2(np.e))

VMEM_LIMIT = 56 * 1024 * 1024
FFN_TM = 512
FFN_SPLIT = 2
W_CHUNKS = 8
W_SLOTS = 3
W_OUT_CHUNKS = 4
W_IN_CHUNKS = 5
MIX_TM = 1024
MIX_SPLIT = 2


def _dot(a, b):
    return jnp.dot(a, b, preferred_element_type=F32)


def _dot_nt(a, b):
    return lax.dot_general(a, b, (((1,), (1,)), ((), ())), preferred_element_type=F32)


def _aligned(x, m):
    return x if isinstance(x, int) else pl.multiple_of(x, m)


def _rms(x, g):
    ms = jnp.mean(x * x, axis=-1, keepdims=True)
    return x * lax.rsqrt(ms + EPS) * g


def _group_mean_sq(t, hmat):
    return _dot((t * t).astype(BF16), hmat)


def _const_spec(shape):
    nd = len(shape)
    return pl.BlockSpec(shape, lambda *_: (0,) * nd, pipeline_mode=pl.Buffered(1))


def _params(semantics):
    return pltpu.CompilerParams(dimension_semantics=(semantics,), vmem_limit_bytes=VMEM_LIMIT)


ROPE_HALF = HEAD_DIM // 2
ROPE_PACK = LANES // ROPE_HALF


def _rope_kernel(pos_ref, freq_ref, cos_ref, sin_ref):
    ang = pos_ref[...] * freq_ref[...]
    rows = ang.shape[0]
    token = lax.broadcasted_iota(jnp.int32, ang.shape, 1) >> 5
    for table, out_ref in ((jnp.cos(ang), cos_ref), (jnp.sin(ang), sin_ref)):
        for i in range(ROPE_PACK):
            x = jnp.where(token == i, table, 0.0)
            y = x + pltpu.roll(x, 2 * ROPE_HALF, 1)
            out_ref[pl.ds(i, rows, stride=ROPE_PACK), :] = y + pltpu.roll(y, ROPE_HALF, 1)


def _rope_inputs(positions):
    T = positions.size
    freqs = ROPE_THETA ** (-jnp.arange(0, HEAD_DIM, 2, dtype=F32) / HEAD_DIM)
    pos = jnp.repeat(positions.reshape(T // ROPE_PACK, ROPE_PACK).astype(F32), ROPE_HALF, axis=1)
    return pos, jnp.tile(freqs, ROPE_PACK).reshape(1, LANES)


def _load_weight_bf16(w_hbm, layer, dst, stage, sems):
    n_slots, rows, cols = stage.shape
    n_chunks = dst.shape[0] // rows

    def copy(c):
        slot = c % n_slots
        return pltpu.make_async_copy(w_hbm.at[layer, pl.ds(c * rows, rows), pl.ds(0, cols)], stage.at[slot],
                                     sems.at[slot])

    for c in range(min(n_slots - 1, n_chunks)):
        copy(c).start()
    for c in range(n_chunks):
        if c + n_slots - 1 < n_chunks:
            copy(c + n_slots - 1).start()
        copy(c).wait()
        dst[c * rows:(c + 1) * rows, :] = stage[c % n_slots].astype(BF16)


def _ffn_kernel(*refs, layer, has_rope, has_proj, has_final):
    refs = list(refs)
    x_ref = refs.pop(0)
    if has_rope:
        pos_ref, freq_ref = refs.pop(0), refs.pop(0)
    if has_proj:
        ya_ref, ym_ref, ys_ref, wo_hbm = refs[:4]
        refs = refs[4:]
    g_ref, wg_hbm, wu_hbm, wd_hbm = refs[:4]
    refs = refs[4:]
    if has_final:
        gout_ref = refs.pop(0)
    out_ref = refs.pop(0)
    if has_rope:
        cos_ref, sin_ref = refs.pop(0), refs.pop(0)
    wg_ref, wu_ref, wd_ref, stage_in, stage_out, sems = refs[:6]
    if has_proj:
        wo_ref = refs[6]

    @pl.when(pl.program_id(0) == 0)
    def _():
        if has_proj:
            rows = wo_ref.shape[0] // W_OUT_CHUNKS
            _load_weight_bf16(wo_hbm, layer, wo_ref, stage_out.at[:, 0:rows, :], sems)
        _load_weight_bf16(wg_hbm, layer, wg_ref, stage_in, sems)
        _load_weight_bf16(wu_hbm, layer, wu_ref, stage_in, sems)
        _load_weight_bf16(wd_hbm, layer, wd_ref, stage_out, sems)

    hs = x_ref.shape[0] // FFN_SPLIT
    rows = [slice(i * hs, (i + 1) * hs) for i in range(FFN_SPLIT)]
    xs = []
    for r in rows:
        x = x_ref[r, :]
        if has_proj:
            x = (x + _dot(ya_ref[r, :], wo_ref[0:ATTN_Q_W, :])
                 + _dot(ym_ref[r, :], wo_ref[ATTN_Q_W:ATTN_Q_W + M_W, :])
                 + _dot(ys_ref[r, :], wo_ref[ATTN_Q_W + M_W:, :]))
        xs.append(x)
    hidden = [_rms(x, g_ref[...]).astype(BF16) for x in xs]
    gate_up = [(_dot(h, wg_ref[...]), _dot(h, wu_ref[...])) for h in hidden]
    for r, x, (gate, up) in zip(rows, xs, gate_up):
        act = (gate * jax.nn.sigmoid(gate) * up).astype(BF16)
        y = x + 0.5 * _dot(act, wd_ref[...])
        if has_final:
            y = _rms(y, gout_ref[...])
        out_ref[r, :] = y

    if has_rope:
        _rope_kernel(pos_ref, freq_ref, cos_ref, sin_ref)


def _ffn(x, layer, g, wg, wu, wd, rope=None, proj=None, gout=None, tm=FFN_TM):
    T = x.shape[0]
    d_model, d_ff = wg.shape[1:]
    row = lambda w: pl.BlockSpec((tm, w), lambda i: (i, 0))
    hbm = pl.BlockSpec(memory_space=pl.ANY)
    args, specs = [x], [row(D_MODEL)]
    out_specs, out_shape = row(D_MODEL), jax.ShapeDtypeStruct((T, D_MODEL), F32)
    if rope is not None:
        args += list(rope)
        specs += [pl.BlockSpec((tm // ROPE_PACK, LANES), lambda i: (i, 0)), _const_spec((1, LANES))]
        out_specs = [out_specs, row(LANES), row(LANES)]
        out_shape = [out_shape] + [jax.ShapeDtypeStruct((T, LANES), F32)] * 2
    if proj is not None:
        ya, ym, ys, wo = proj
        args += [ya, ym, ys, wo]
        specs += [row(ATTN_Q_W), row(M_W), row(SGU_W), hbm]
    args += [g.reshape(1, D_MODEL), wg, wu, wd]
    specs += [_const_spec((1, D_MODEL)), hbm, hbm, hbm]
    if gout is not None:
        args.append(gout.reshape(1, D_MODEL))
        specs.append(_const_spec((1, D_MODEL)))
    return pl.pallas_call(
        functools.partial(_ffn_kernel, layer=layer, has_rope=rope is not None, has_proj=proj is not None,
                          has_final=gout is not None),
        grid=(T // tm,),
        in_specs=specs,
        out_specs=out_specs,
        out_shape=out_shape,
        scratch_shapes=[pltpu.VMEM((d_model, d_ff), BF16), pltpu.VMEM((d_model, d_ff), BF16),
                        pltpu.VMEM((d_ff, d_model), BF16),
                        pltpu.VMEM((W_SLOTS, d_model // W_CHUNKS, d_ff), F32),
                        pltpu.VMEM((W_SLOTS, d_ff // W_CHUNKS, d_model), F32),
                        pltpu.SemaphoreType.DMA((W_SLOTS,))]
        + ([pltpu.VMEM(proj[3].shape[1:], BF16)] if proj is not None else []),
        compiler_params=_params("arbitrary"),
        name="ffn",
    )(*args)


def _first_half(shape):
    lane = lax.broadcasted_iota(jnp.int32, shape, 1)
    return (lane & (HEAD_DIM - 1)) < HEAD_DIM // 2


def _rope(t, cos, sin_signed):
    n = t.shape[-1]
    reps = n // LANES
    swapped = jnp.where(_first_half(t.shape),
                        pltpu.roll(t, n - HEAD_DIM // 2, 1), pltpu.roll(t, HEAD_DIM // 2, 1))
    return t * jnp.tile(cos, (1, reps)) + swapped * jnp.tile(sin_signed, (1, reps))


def _gelu(x):
    return 0.5 * x * (1.0 + lax.erf(x * np.float32(np.sqrt(0.5))))


def _mixin_kernel(x_ref, g_ref, w_hbm, qg_ref, kg_ref, cos_ref, sin_ref, hmat_ref, ng_ref, ws_ref,
                  bs_ref, qa_ref, kd_ref, vat_ref, mqk_ref, mvt_ref, mo_ref, mgt_ref, ys_ref, w_ref, stage, sems,
                  *, layer):
    @pl.when(pl.program_id(0) == 0)
    def _():
        _load_weight_bf16(w_hbm, layer, w_ref, stage, sems)

    scale = HEAD_DIM ** -0.5
    hmat = hmat_ref[...]
    gw = hmat.shape[0]
    c_q, c_kv, c_mq, c_mk, c_mv, c_mo, c_g, c_su, c_sv = (int(c) for c in np.cumsum(
        [0, ATTN_Q_W, 2 * ATTN_KV_W, M_W, M_W, M_W, M_W, N_GATES, SGU_W]))
    lane_group = lax.broadcasted_iota(jnp.int32, (BLK, SGU_W), 1) >> 6

    def projections(r0, nrows):
        r = slice(r0, r0 + nrows)
        h = _rms(x_ref[r, :], g_ref[...]).astype(BF16)
        proj = lambda start, width: _dot_nt(h, w_ref[start:start + width, :])
        su = proj(c_su, SGU_W)
        sv = proj(c_sv, SGU_W)
        aq = proj(c_q, ATTN_Q_W)
        akv = proj(c_kv, 2 * ATTN_KV_W)
        ak, av = akv[:, :ATTN_KV_W], akv[:, ATTN_KV_W:]
        v = _gelu(sv)
        v_ms = _group_mean_sq(v, hmat)
        mqk_ref[r, 0:M_W] = proj(c_mq, M_W).astype(BF16)
        mqk_ref[r, M_W:2 * M_W] = (proj(c_mk, M_W) * scale).astype(BF16)
        q_ms = jnp.concatenate([_group_mean_sq(aq[:, i:i + gw], hmat) for i in range(0, ATTN_Q_W, gw)], axis=1)
        k_ms = _group_mean_sq(ak, hmat[0:ATTN_KV_W, 0:ATTN_KV_W])
        mvt_ref[:, r] = proj(c_mv, M_W).T.astype(BF16)
        mo_ref[r, :] = proj(c_mo, M_W).astype(BF16)
        mgt_ref[:, r] = proj(c_g, LANES).T[0:N_GATES, :]

        def tail():
            cos = cos_ref[r, :]
            sin_signed = jnp.where(_first_half(cos.shape), -sin_ref[r, :], sin_ref[r, :])
            qn = aq * lax.rsqrt(q_ms + EPS) * qg_ref[...]
            qa_ref[r, :] = (_rope(qn, cos, sin_signed) * (scale * LOG2E)).astype(BF16)
            kn = ak * lax.rsqrt(k_ms + EPS) * kg_ref[...]
            kr = _rope(kn, cos, sin_signed)
            ks = pltpu.roll(kr, HEAD_DIM, 1)
            low = lax.broadcasted_iota(jnp.int32, kr.shape, 1) < HEAD_DIM
            kd_ref[r, 0:LANES] = jnp.where(low, kr, ks).astype(BF16)
            kd_ref[r, LANES:2 * LANES] = jnp.where(low, ks, kr).astype(BF16)
            vat_ref[:, r] = av.T.astype(BF16)
            u = _gelu(su)
            vn = (v * lax.rsqrt(v_ms + EPS) * ng_ref[...]).astype(BF16)
            for c in range(nrows // BLK):
                vc = vn[c * BLK:(c + 1) * BLK]
                stack = jnp.concatenate(
                    [jnp.where(lane_group == g, vc, jnp.zeros_like(vc)) for g in range(SGU_GROUPS)], axis=0)
                mixed = _dot(ws_ref[...], stack)
                ys_ref[r0 + c * BLK:r0 + (c + 1) * BLK, :] = (
                    u[c * BLK:(c + 1) * BLK] * (mixed + bs_ref[...])).astype(BF16)

        return tail

    nrows = x_ref.shape[0] // MIX_SPLIT
    tails = [projections(i * nrows, nrows) for i in range(MIX_SPLIT)]
    for tail in tails:
        tail()


def _mix_in(x, layer, g, w_in_t, q_g, k_g, cos, sin, hmat, sgu_g, sgu_w, sgu_b, tm=MIX_TM):
    T = x.shape[0]
    d_in, d_model = w_in_t.shape[1:]
    chunk = d_in // W_IN_CHUNKS
    assert chunk * W_IN_CHUNKS == d_in and chunk % 16 == 0
    row = lambda w: pl.BlockSpec((tm, w), lambda i: (i, 0))
    tcol = lambda r: pl.BlockSpec((r, tm), lambda i: (0, i))
    out_specs = [row(ATTN_Q_W), row(2 * ATTN_KV_W), tcol(ATTN_KV_W), row(2 * M_W), tcol(M_W), row(M_W),
                 tcol(N_GATES), row(SGU_W)]
    out_shape = [jax.ShapeDtypeStruct(s, d) for s, d in [
        ((T, ATTN_Q_W), BF16), ((T, 2 * ATTN_KV_W), BF16), ((ATTN_KV_W, T), BF16), ((T, 2 * M_W), BF16),
        ((M_W, T), BF16), ((T, M_W), BF16), ((N_GATES, T), F32), ((T, SGU_W), BF16)]]
    ws_cat = jnp.transpose(sgu_w, (1, 0, 2)).reshape(BLK, SGU_GROUPS * BLK).astype(BF16)
    bs_full = jnp.repeat(jnp.transpose(sgu_b), HEAD_DIM, axis=1)
    return pl.pallas_call(
        functools.partial(_mixin_kernel, layer=layer),
        grid=(T // tm,),
        in_specs=[row(D_MODEL), _const_spec((1, D_MODEL)), pl.BlockSpec(memory_space=pl.ANY),
                  _const_spec((1, ATTN_Q_W)), _const_spec((1, ATTN_KV_W)),
                  row(LANES), row(LANES), _const_spec(hmat.shape),
                  _const_spec((1, SGU_W)), _const_spec(ws_cat.shape), _const_spec(bs_full.shape)],
        out_specs=out_specs,
        out_shape=out_shape,
        scratch_shapes=[pltpu.VMEM((d_in, d_model), BF16),
                        pltpu.VMEM((W_SLOTS, chunk, d_model), F32),
                        pltpu.SemaphoreType.DMA((W_SLOTS,))],
        compiler_params=_params("arbitrary"),
        name="mix_in",
    )(x, g.reshape(1, D_MODEL), w_in_t,
      jnp.tile(q_g, N_Q_HEADS).reshape(1, ATTN_Q_W), jnp.tile(k_g, N_KV_HEADS).reshape(1, ATTN_KV_W),
      cos, sin, hmat, sgu_g.reshape(1, SGU_W), ws_cat, bs_full)


def _attn_blocks(sink_ref, q_ref, kd_ref, vt_ref, o_ref):
    S = q_ref.shape[0]
    nb = S // BLK
    kc = lax.broadcasted_iota(jnp.int32, (BLK, BLK), 0)
    qi = lax.broadcasted_iota(jnp.int32, (BLK, BLK), 1)
    prev_bias = jnp.tile(jnp.where(kc >= qi, 0.0, NEG), (1, Q_PER_KV))
    next_bias = jnp.tile(jnp.where(kc <= qi, 0.0, NEG), (1, Q_PER_KV))
    half_mask = (qi < HEAD_DIM, qi >= HEAD_DIM)
    head_lane = lax.broadcasted_iota(jnp.int32, (1, Q_PER_KV * BLK), 1) >> 7
    sink_rows = []
    for kvh in range(N_KV_HEADS):
        row = jnp.zeros((1, Q_PER_KV * BLK), F32)
        for g in range(Q_PER_KV):
            row = jnp.where(head_lane == g, sink_ref[kvh * Q_PER_KV + g] * LOG2E, row)
        sink_rows.append(row)

    def block(q0, start, biases, with_scores):
        nk = len(biases)
        scores = []
        for kvh in range(N_KV_HEADS if with_scores else 0):
            kd = kd_ref[pl.ds(start, nk * BLK), kvh * LANES:(kvh + 1) * LANES]
            qs = []
            for g in range(Q_PER_KV):
                t = kvh * Q_PER_KV + g
                qt = q_ref[pl.ds(q0, BLK), (t // 2) * LANES:(t // 2 + 1) * LANES]
                qs.append(jnp.where(half_mask[t & 1], qt, jnp.zeros_like(qt)))
            scores.append(_dot_nt(kd, jnp.concatenate(qs, axis=0)))

        def weighted_values(scores):
            ones = jnp.ones((HEAD_DIM, nk * BLK), BF16)
            outs = []
            for kvh in range(N_KV_HEADS):
                v_ext = jnp.concatenate([vt_ref[kvh * HEAD_DIM:(kvh + 1) * HEAD_DIM, pl.ds(start, nk * BLK)], ones],
                                        axis=0)
                parts = [scores[kvh][j * BLK:(j + 1) * BLK] for j in range(nk)]
                parts = [p if b is None else p + b for p, b in zip(parts, biases)]
                mx = jnp.max(functools.reduce(jnp.maximum, parts), axis=0, keepdims=True)
                m = jnp.maximum(mx, sink_rows[kvh])
                p = jnp.concatenate([jnp.exp2(p - m) for p in parts], axis=0).astype(BF16)
                oe = _dot(v_ext, p)
                denom = oe[HEAD_DIM:HEAD_DIM + 1] + jnp.exp2(sink_rows[kvh] - m)
                on = oe[0:HEAD_DIM] / denom
                for pair in range(Q_PER_KV // 2):
                    two = jnp.concatenate([on[:, (2 * pair) * BLK:(2 * pair + 1) * BLK],
                                           on[:, (2 * pair + 1) * BLK:(2 * pair + 2) * BLK]], axis=0)
                    outs.append(two.T)
            result = jnp.concatenate(outs, axis=1).astype(BF16)

            def commit():
                o_ref[pl.ds(q0, BLK), :] = result
            return commit

        return (scores if with_scores else None), weighted_values

    def attend(n, with_scores=True):
        if isinstance(n, int) and n == 0:
            return block(0, 0, [None, next_bias], with_scores)
        if isinstance(n, int) and n == nb - 1:
            return block(n * BLK, (n - 1) * BLK, [prev_bias, None], with_scores)
        return block(_aligned(n * BLK, BLK), _aligned((n - 1) * BLK, BLK), [prev_bias, None, next_bias], with_scores)

    return attend


def _log_sigmoid(x):
    return jnp.minimum(x, 0.0) - jnp.log(1.0 + jnp.exp(-jnp.abs(x)))


def _mlstm_chunks(qk_ref, vt_ref, mo_ref, mgt_ref, gb_ref, hg_ref, o_ref,
                  gt_s, b_s, w_s, mloc_s, tot_s, rcol_s, cloc_s, cprev_s, mprev_s, keep_s, add_s):
    S = qk_ref.shape[0]
    L = BLK
    nc = S // L
    ng = N_GATES
    sub = lax.broadcasted_iota(jnp.int32, (L, L), 0)
    lane = lax.broadcasted_iota(jnp.int32, (L, L), 1)

    for c in range(nc):
        gt_s[c * ng:(c + 1) * ng, :] = mgt_ref[:, c * L:(c + 1) * L] + gb_ref[...]
    gt = gt_s[...]
    row = lax.broadcasted_iota(jnp.int32, gt.shape, 0)
    gt = jnp.where(((row >> 2) & 1) == 1, _log_sigmoid(gt), gt)
    gt = gt * LOG2E
    upper = jnp.where(sub <= lane, 1.0, 0.0).astype(BF16)
    g1 = gt.astype(BF16)
    r1 = gt - g1.astype(F32)
    g2 = r1.astype(BF16)
    g3 = (r1 - g2.astype(F32)).astype(BF16)
    pre = _dot(g1, upper) + _dot(g2, upper) + _dot(g3, upper)
    tot = jnp.broadcast_to(pre[:, L - 1:L], pre.shape)
    bsum =jnp.where((row & (ng - 1)) < ng // 2, pre, tot - pre + gt)
    li = pltpu.roll(gt, M_HEADS, 0)
    a = tot - bsum + li
    mloc = jnp.broadcast_to(jnp.max(a, axis=1, keepdims=True), a.shape)
    b_s[...] = bsum
    w_s[...] = jnp.exp2(a - mloc)
    mloc_s[...] = mloc
    tot_s[...] = tot
    rdiff = li - bsum
    zpad = jnp.zeros((L - ng, L), F32)
    for c in range(nc):
        rcol_s[c * L:(c + 1) * L, :] = jnp.concatenate([rdiff[c * ng:(c + 1) * ng, :], zpad], axis=0).T

    def value_rows(hd, r0, dtype):
        vt = vt_ref[hd * HEAD_DIM:(hd + 1) * HEAD_DIM, pl.ds(r0, L)].astype(dtype)
        return jnp.concatenate([vt, jnp.ones((V_ROWS - HEAD_DIM, L), dtype)], axis=0)

    def local_state(c, carry):
        r0 = pl.multiple_of(c * L, L)
        g0 = pl.multiple_of(c * ng, ng)
        k = qk_ref[pl.ds(r0, L), M_W:2 * M_W]
        w = w_s[pl.ds(g0, ng), :]
        for hd in range(M_HEADS):
            vte = value_rows(hd, r0, F32)
            lhs = jnp.concatenate([vte * w[4 + hd:5 + hd, :], vte * w[12 + hd:13 + hd, :]], axis=0)
            cl = _dot(lhs.astype(BF16), k[:, (hd // 2) * LANES:(hd // 2 + 1) * LANES])
            cloc_s[0, hd, c] = cl[0:V_ROWS]
            cloc_s[1, hd, c] = cl[V_ROWS:2 * V_ROWS]
        return carry

    lax.fori_loop(0, nc, local_state, 0, unroll=8)

    jrow = lax.broadcasted_iota(jnp.int32, (ng, L), 0)

    def scan_stabilisers(i, m):
        cf = i
        cb = nc - 1 - i
        gf = pl.multiple_of(cf * ng, ng)
        gb = pl.multiple_of(cb * ng, ng)
        is_fwd = jrow < ng // 2
        blast = jnp.where(is_fwd, tot_s[pl.ds(gf, ng), :], tot_s[pl.ds(gb, ng), :])
        mloc_i = jnp.where(is_fwd, mloc_s[pl.ds(gf, ng), :], mloc_s[pl.ds(gb, ng), :])
        mprev_s[0, pl.ds(gf, ng), :] = m
        mprev_s[1, pl.ds(gb, ng), :] = m
        m_new = jnp.maximum(blast + m, mloc_i)
        gi = pl.multiple_of(i * ng, ng)
        keep_s[pl.ds(gi, ng), :] = jnp.exp2(blast + m - m_new)
        add_s[pl.ds(gi, ng), :] = jnp.exp2(mloc_i - m_new)
        return m_new

    lax.fori_loop(0, nc, scan_stabilisers, jnp.zeros((ng, L), F32))

    for h0 in range(0, M_HEADS, SCAN_HEADS):
        def scan_states(i, states, h0=h0):
            gi = pl.multiple_of(i * ng, ng)
            keep = keep_s[pl.ds(gi, ng), :]
            add = add_s[pl.ds(gi, ng), :]
            new_states = []
            for j in range(SCAN_HEADS):
                for d, ci in enumerate((i, nc - 1 - i)):
                    hd = h0 + j
                    r = 4 + hd + 8 * d
                    st = states[2 * j + d]
                    cprev_s[d, hd, ci] = st.astype(BF16)
                    new_states.append(keep[r:r + 1, :] * st + add[r:r + 1, :] * cloc_s[d, hd, ci])
            return tuple(new_states)

        lax.fori_loop(0, nc, scan_states, tuple(jnp.zeros((V_ROWS, LANES), F32) for _ in range(2 * SCAN_HEADS)))

    causal = (sub <= lane, sub >= lane)

    def outputs(c, with_scores=True):
        r0 = _aligned(c * L, L)
        g0 = _aligned(c * ng, ng)
        q = qk_ref[pl.ds(r0, L), 0:M_W]
        k = qk_ref[pl.ds(r0, L), M_W:2 * M_W]
        bc = b_s[pl.ds(g0, ng), :]
        mprev = (mprev_s[0, pl.ds(g0, ng), :], mprev_s[1, pl.ds(g0, ng), :])
        rc = rcol_s[pl.ds(r0, L), :]
        ys = []
        for hd in range(M_HEADS if with_scores else 0):
            t = hd // 2
            qp = q[:, t * LANES:(t + 1) * LANES]
            qm = jnp.where((lane >> 6) == (hd & 1), qp, jnp.zeros_like(qp))
            x = jnp.concatenate([k[:, t * LANES:(t + 1) * LANES], cprev_s[0, hd, c], cprev_s[1, hd, c]], axis=0)
            ys.append(_dot_nt(x, qm))

        def combine(ys):
            houts = []
            for hd in range(M_HEADS):
                y = ys[hd]
                st = y[0:L]
                pts, scs, mts = [], [], []
                for d in range(2):
                    r = 4 + hd + 8 * d
                    b_row = bc[r:r + 1, :]
                    dm = jnp.where(causal[d], rc[:, r:r + 1] + b_row, NEG)
                    inter = b_row + mprev[d][r:r + 1, :]
                    m_t = jnp.maximum(inter, jnp.max(dm, axis=0, keepdims=True))
                    pts.append((jnp.exp2(dm - m_t) * st).astype(BF16))
                    scs.append(jnp.exp2(inter - m_t))
                    mts.append(m_t)
                n2 = _dot(value_rows(hd, r0, BF16), jnp.concatenate(pts, axis=1))
                ht = None
                for d in range(2):
                    tt = n2[:, d * L:(d + 1) * L] + scs[d] * y[L + d * V_ROWS:L + (d + 1) * V_ROWS]
                    den = tt[HEAD_DIM:HEAD_DIM + 1, :]
                    hd_out = tt[0:HEAD_DIM] / jnp.maximum(jnp.abs(den), jnp.exp2(-mts[d]))
                    ht = hd_out if ht is None else ht + hd_out
                ms = jnp.mean(ht * ht, axis=0, keepdims=True)
                houts.append(ht * lax.rsqrt(ms + EPS))
            hn = (jnp.concatenate(houts, axis=0) * hg_ref[...]).T
            result = (jax.nn.sigmoid(mo_ref[pl.ds(r0, L), :].astype(F32)) * hn).astype(BF16)

            def commit():
                o_ref[pl.ds(r0, L), :] = result
            return commit

        return (ys if with_scores else None), combine

    return outputs


def _mixers_kernel(sink_ref, q_ref, kd_ref, vat_ref, qk_ref, mvt_ref, mo_ref, mgt_ref, gb_ref, hg_ref,
                   ya_ref, ym_ref, *scratch):
    *mlstm_scratch, s_buf, y_buf = scratch
    nb = q_ref.shape[0] // BLK
    n_groups = (nb - 2) // 2
    attend = _attn_blocks(sink_ref, q_ref, kd_ref, vat_ref, ya_ref)
    outputs = _mlstm_chunks(qk_ref, mvt_ref, mo_ref, mgt_ref, gb_ref, hg_ref, ym_ref, *mlstm_scratch)
    group = lambda g: (1 + 2 * g, 2 + 2 * g)

    def first(blocks):
        return [f(n)[0] for n in blocks for f in (attend, outputs)]

    def second(blocks, results):
        finishes = [f(n, with_scores=False)[1] for n in blocks for f in (attend, outputs)]
        commits = [finish(r) for finish, r in zip(finishes, results)]
        for commit in commits:
            commit()

    def park(slot, results):
        for b in range(2):
            for i, s in enumerate(results[2 * b]):
                s_buf[slot, N_KV_HEADS * b + i, 0:s.shape[0], :] = s
            for i, y in enumerate(results[2 * b + 1]):
                y_buf[slot, M_HEADS * b + i] = y

    def fetch(slot, key_blocks):
        results = []
        for b in range(2):
            results.append([s_buf[slot, N_KV_HEADS * b + i, 0:key_blocks * BLK, :] for i in range(N_KV_HEADS)])
            results.append([y_buf[slot, M_HEADS * b + i] for i in range(M_HEADS)])
        return results

    edges = (0, nb - 1)

    def advance(g, slot):
        park(slot, first(group(g)))
        if isinstance(g, int) and g == 0:
            second(edges, fetch(1 - slot, 2))
        else:
            second(group(g - 1), fetch(1 - slot, 3))

    def two_steps(j, carry):
        advance(1 + 2 * j, 0)
        advance(2 + 2 * j, 1)
        return carry

    park(0, first(edges))
    advance(0, 1)
    lax.fori_loop(0, (n_groups - 1) // 2, two_steps, 0)
    last_slot = 1
    if (n_groups - 1) % 2:
        advance(n_groups - 1, 0)
        last_slot = 0
    second(group(n_groups - 1), fetch(last_slot, 3))


def _mixers(qa, kd, vat, sink, mqk, mvt, mo, mgt, gate_b, head_g, B, S):
    seq = lambda w: pl.BlockSpec((S, w), lambda b: (b, 0))
    tseq = lambda r: pl.BlockSpec((r, S), lambda b: (0, b))
    nc = S // BLK
    assert nc % 2 == 0 and nc >= 4
    gb = jnp.broadcast_to(gate_b.reshape(N_GATES, 1), (N_GATES, BLK))
    hg_t = jnp.broadcast_to(head_g.reshape(M_W, 1), (M_W, BLK))
    rows = nc * N_GATES
    return pl.pallas_call(
        _mixers_kernel,
        grid=(B,),
        in_specs=[pl.BlockSpec(memory_space=pltpu.SMEM), seq(ATTN_Q_W), seq(2 * ATTN_KV_W), tseq(ATTN_KV_W),
                  seq(2 * M_W), tseq(M_W), seq(M_W), tseq(N_GATES), _const_spec((N_GATES, BLK)),
                  _const_spec((M_W, BLK))],
        out_specs=[seq(ATTN_Q_W), seq(M_W)],
        out_shape=[jax.ShapeDtypeStruct((B * S, ATTN_Q_W), BF16), jax.ShapeDtypeStruct((B * S, M_W), BF16)],
        scratch_shapes=[pltpu.VMEM((rows, BLK), F32),
                        pltpu.VMEM((rows, BLK), F32),
                        pltpu.VMEM((rows, BLK), F32),
                        pltpu.VMEM((rows, BLK), F32),
                        pltpu.VMEM((rows, BLK), F32),
                        pltpu.VMEM((S, LANES), F32),
                        pltpu.VMEM((2, M_HEADS, nc, V_ROWS, LANES), F32),
                        pltpu.VMEM((2, M_HEADS, nc, V_ROWS, LANES), BF16),
                        pltpu.VMEM((2, rows, BLK), F32),
                        pltpu.VMEM((rows, BLK), F32),
                        pltpu.VMEM((rows, BLK), F32),
                        pltpu.VMEM((2, 2 * N_KV_HEADS, 3 * BLK, Q_PER_KV * BLK), F32),
                        pltpu.VMEM((2, 2 * M_HEADS, BLK + 2 * V_ROWS, BLK), F32)],
        compiler_params=_params("parallel"),
        name="mixers",
    )(sink, qa, kd, vat, mqk, mvt, mo, mgt, gb, hg_t)


def kernel(x, positions, norm_ffn1_g, ffn1_w_gate, ffn1_w_up, ffn1_w_down, norm_mix_g, w_in, q_norm_g, k_norm_g, attn_sink, mlstm_gate_b, mlstm_head_g, sgu_norm_g, sgu_w_s, sgu_b, w_out, norm_ffn2_g, ffn2_w_gate, ffn2_w_up, ffn2_w_down, norm_out_g):
    B, S, D = x.shape
    depth = w_in.shape[0]
    T = B * S
    xt = x.reshape(T, D)
    group = np.arange(M_W) // HEAD_DIM
    hmat = jnp.asarray((group[:, None] == group[None, :]) / HEAD_DIM, BF16)
    w_in_t = jnp.swapaxes(w_in, 1, 2)
    for l in range(depth):
        if l == 0:
            xt, cos, sin = _ffn(xt, l, norm_ffn1_g[l], ffn1_w_gate, ffn1_w_up, ffn1_w_down,
                                rope=_rope_inputs(positions))
        else:
            xt = _ffn(xt, l, norm_ffn1_g[l], ffn1_w_gate, ffn1_w_up, ffn1_w_down)
        qa, kd, vat, mqk, mvt, mo, mgt, ys = _mix_in(xt, l, norm_mix_g[l], w_in_t, q_norm_g[l],
                                                     k_norm_g[l], cos, sin, hmat, sgu_norm_g[l], sgu_w_s[l], sgu_b[l])
        ya, ym = _mixers(qa, kd, vat, attn_sink[l], mqk, mvt, mo, mgt, mlstm_gate_b[l], mlstm_head_g[l], B, S)
        xt = _ffn(xt, l, norm_ffn2_g[l], ffn2_w_gate, ffn2_w_up, ffn2_w_down, proj=(ya, ym, ys, w_out),
                  gout=norm_out_g[l])
    return xt.reshape(B, S, D)
```

```python
import functools

import numpy as np
import jax
import jax.numpy as jnp
from jax import lax
from jax.experimental import pallas as pl
from jax.experimental.pallas import tpu as pltpu

F32 = jnp.float32
BF16 = jnp.bfloat16

D_MODEL = 1024
HEAD_DIM = 64
N_Q_HEADS = 8
N_KV_HEADS = 2
Q_PER_KV = N_Q_HEADS // N_KV_HEADS
ATTN_Q_W = N_Q_HEADS * HEAD_DIM
ATTN_KV_W = N_KV_HEADS * HEAD_DIM
WINDOW = 128
BLK = 128
ROPE_THETA = 10000.0
M_HEADS = 4
M_W = M_HEADS * HEAD_DIM
N_GATES = 4 * M_HEADS
V_ROWS = HEAD_DIM + 16
SCAN_HEADS = 2
SGU_GROUPS = 4
SGU_W = SGU_GROUPS * HEAD_DIM
EPS = 1e-6
LANES = 128
NEG = -1e30
LOG2E = float(np.log2(np.e))

VMEM_LIMIT = 56 * 1024 * 1024
FFN_TM = 512
FFN_SPLIT = 2
W_CHUNKS = 8
W_SLOTS = 4
W_OUT_CHUNKS = 4
W_IN_CHUNKS = 5
MIX_TM = 1024
MIX_SPLIT = 2


def _dot(a, b):
    return jnp.dot(a, b, preferred_element_type=F32)


def _dot_nt(a, b):
    return lax.dot_general(a, b, (((1,), (1,)), ((), ())), preferred_element_type=F32)


def _aligned(x, m):
    return x if isinstance(x, int) else pl.multiple_of(x, m)


def _rms(x, g):
    ms = jnp.mean(x * x, axis=-1, keepdims=True)
    return x * lax.rsqrt(ms + EPS) * g


def _group_mean_sq(t, hmat):
    return _dot((t * t).astype(BF16), hmat)


def _const_spec(shape):
    nd = len(shape)
    return pl.BlockSpec(shape, lambda *_: (0,) * nd, pipeline_mode=pl.Buffered(1))


def _params(semantics):
    return pltpu.CompilerParams(dimension_semantics=(semantics,), vmem_limit_bytes=VMEM_LIMIT)


ROPE_HALF = HEAD_DIM // 2
ROPE_PACK = LANES // ROPE_HALF


def _rope_kernel(pos_ref, freq_ref, cos_ref, sin_ref):
    ang = pos_ref[...] * freq_ref[...]
    rows = ang.shape[0]
    token = lax.broadcasted_iota(jnp.int32, ang.shape, 1) >> 5
    for table, out_ref in ((jnp.cos(ang), cos_ref), (jnp.sin(ang), sin_ref)):
        for i in range(ROPE_PACK):
            x = jnp.where(token == i, table, 0.0)
            y = x + pltpu.roll(x, 2 * ROPE_HALF, 1)
            out_ref[pl.ds(i, rows, stride=ROPE_PACK), :] = y + pltpu.roll(y, ROPE_HALF, 1)


def _rope_inputs(positions):
    T = positions.size
    freqs = ROPE_THETA ** (-jnp.arange(0, HEAD_DIM, 2, dtype=F32) / HEAD_DIM)
    pos = jnp.repeat(positions.reshape(T // ROPE_PACK, ROPE_PACK).astype(F32), ROPE_HALF, axis=1)
    return pos, jnp.tile(freqs, ROPE_PACK).reshape(1, LANES)


def _load_weight_bf16(w_hbm, layer, dst, stage, sems):
    n_slots, rows, cols = stage.shape
    n_chunks = dst.shape[0] // rows

    def copy(c):
        slot = c % n_slots
        return pltpu.make_async_copy(w_hbm.at[layer, pl.ds(c * rows, rows), pl.ds(0, cols)], stage.at[slot],
                                     sems.at[slot])

    for c in range(min(n_slots - 1, n_chunks)):
        copy(c).start()
    for c in range(n_chunks):
        if c + n_slots - 1 < n_chunks:
            copy(c + n_slots - 1).start()
        copy(c).wait()
        dst[c * rows:(c + 1) * rows, :] = stage[c % n_slots].astype(BF16)


def _ffn_kernel(*refs, layer, has_rope, has_proj, has_final):
    refs = list(refs)
    x_ref = refs.pop(0)
    if has_rope:
        pos_ref, freq_ref = refs.pop(0), refs.pop(0)
    if has_proj:
        ya_ref, ym_ref, ys_ref, wo_hbm = refs[:4]
        refs = refs[4:]
    g_ref, wg_hbm, wu_hbm, wd_hbm = refs[:4]
    refs = refs[4:]
    if has_final:
        gout_ref = refs.pop(0)
    out_ref = refs.pop(0)
    if has_rope:
        cos_ref, sin_ref = refs.pop(0), refs.pop(0)
    wg_ref, wu_ref, wd_ref, stage_in, stage_out, sems = refs[:6]
    if has_proj:
        wo_ref = refs[6]

    @pl.when(pl.program_id(0) == 0)
    def _():
        if has_proj:
            rows = wo_ref.shape[0] // W_OUT_CHUNKS
            _load_weight_bf16(wo_hbm, layer, wo_ref, stage_out.at[:, 0:rows, :], sems)
        _load_weight_bf16(wg_hbm, layer, wg_ref, stage_in, sems)
        _load_weight_bf16(wu_hbm, layer, wu_ref, stage_in, sems)
        _load_weight_bf16(wd_hbm, layer, wd_ref, stage_out, sems)

    hs = x_ref.shape[0] // FFN_SPLIT
    rows = [slice(i * hs, (i + 1) * hs) for i in range(FFN_SPLIT)]
    xs = []
    for r in rows:
        x = x_ref[r, :]
        if has_proj:
            x = (x + _dot(ya_ref[r, :], wo_ref[0:ATTN_Q_W, :])
                 + _dot(ym_ref[r, :], wo_ref[ATTN_Q_W:ATTN_Q_W + M_W, :])
                 + _dot(ys_ref[r, :], wo_ref[ATTN_Q_W + M_W:, :]))
        xs.append(x)
    hidden = [_rms(x, g_ref[...]).astype(BF16) for x in xs]
    gate_up = [(_dot(h, wg_ref[...]), _dot(h, wu_ref[...])) for h in hidden]
    for r, x, (gate, up) in zip(rows, xs, gate_up):
        act = (gate * jax.nn.sigmoid(gate) * up).astype(BF16)
        y = x + 0.5 * _dot(act, wd_ref[...])
        if has_final:
            y = _rms(y, gout_ref[...])
        out_ref[r, :] = y

    if has_rope:
        _rope_kernel(pos_ref, freq_ref, cos_ref, sin_ref)


def _ffn(x, layer, g, wg, wu, wd, rope=None, proj=None, gout=None, tm=FFN_TM):
    T = x.shape[0]
    d_model, d_ff = wg.shape[1:]
    row = lambda w: pl.BlockSpec((tm, w), lambda i: (i, 0))
    hbm = pl.BlockSpec(memory_space=pl.ANY)
    args, specs = [x], [row(D_MODEL)]
    out_specs, out_shape = row(D_MODEL), jax.ShapeDtypeStruct((T, D_MODEL), F32)
    if rope is not None:
        args += list(rope)
        specs += [pl.BlockSpec((tm // ROPE_PACK, LANES), lambda i: (i, 0)), _const_spec((1, LANES))]
        out_specs = [out_specs, row(LANES), row(LANES)]
        out_shape = [out_shape] + [jax.ShapeDtypeStruct((T, LANES), F32)] * 2
    if proj is not None:
        ya, ym, ys, wo = proj
        args += [ya, ym, ys, wo]
        specs += [row(ATTN_Q_W), row(M_W), row(SGU_W), hbm]
    args += [g.reshape(1, D_MODEL), wg, wu, wd]
    specs += [_const_spec((1, D_MODEL)), hbm, hbm, hbm]
    if gout is not None:
        args.append(gout.reshape(1, D_MODEL))
        specs.append(_const_spec((1, D_MODEL)))
    return pl.pallas_call(
        functools.partial(_ffn_kernel, layer=layer, has_rope=rope is not None, has_proj=proj is not None,
                          has_final=gout is not None),
        grid=(T // tm,),
        in_specs=specs,
        out_specs=out_specs,
        out_shape=out_shape,
        scratch_shapes=[pltpu.VMEM((d_model, d_ff), BF16), pltpu.VMEM((d_model, d_ff), BF16),
                        pltpu.VMEM((d_ff, d_model), BF16),
                        pltpu.VMEM((W_SLOTS, d_model // W_CHUNKS, d_ff), F32),
                        pltpu.VMEM((W_SLOTS, d_ff // W_CHUNKS, d_model), F32),
                        pltpu.SemaphoreType.DMA((W_SLOTS,))]
        + ([pltpu.VMEM(proj[3].shape[1:], BF16)] if proj is not None else []),
        compiler_params=_params("arbitrary"),
        name="ffn",
    )(*args)


def _first_half(shape):
    lane = lax.broadcasted_iota(jnp.int32, shape, 1)
    return (lane & (HEAD_DIM - 1)) < HEAD_DIM // 2


def _rope(t, cos, sin_signed):
    n = t.shape[-1]
    reps = n // LANES
    swapped = jnp.where(_first_half(t.shape),
                        pltpu.roll(t, n - HEAD_DIM // 2, 1), pltpu.roll(t, HEAD_DIM // 2, 1))
    return t * jnp.tile(cos, (1, reps)) + swapped * jnp.tile(sin_signed, (1, reps))


def _gelu(x):
    return 0.5 * x * (1.0 + lax.erf(x * np.float32(np.sqrt(0.5))))


def _mixin_kernel(x_ref, g_ref, w_hbm, qg_ref, kg_ref, cos_ref, sin_ref, hmat_ref, ng_ref, ws_ref,
                  bs_ref, qa_ref, kd_ref, vat_ref, mqk_ref, mvt_ref, mo_ref, mgt_ref, ys_ref, w_ref, stage, sems,
                  *, layer):
    @pl.when(pl.program_id(0) == 0)
    def _():
        _load_weight_bf16(w_hbm, layer, w_ref, stage, sems)

    scale = HEAD_DIM ** -0.5
    hmat = hmat_ref[...]
    gw = hmat.shape[0]
    c_q, c_kv, c_mq, c_mk, c_mv, c_mo, c_g, c_su, c_sv = (int(c) for c in np.cumsum(
        [0, ATTN_Q_W, 2 * ATTN_KV_W, M_W, M_W, M_W, M_W, N_GATES, SGU_W]))
    lane_group = lax.broadcasted_iota(jnp.int32, (BLK, SGU_W), 1) >> 6

    def projections(r0, nrows):
        r = slice(r0, r0 + nrows)
        h = _rms(x_ref[r, :], g_ref[...]).astype(BF16)
        proj = lambda start, width: _dot_nt(h, w_ref[start:start + width, :])
        su = proj(c_su, SGU_W)
        sv = proj(c_sv, SGU_W)
        aq = proj(c_q, ATTN_Q_W)
        akv = proj(c_kv, 2 * ATTN_KV_W)
        ak, av = akv[:, :ATTN_KV_W], akv[:, ATTN_KV_W:]
        v = _gelu(sv)
        v_ms = _group_mean_sq(v, hmat)
        mqk_ref[r, 0:M_W] = proj(c_mq, M_W).astype(BF16)
        mqk_ref[r, M_W:2 * M_W] = (proj(c_mk, M_W) * scale).astype(BF16)
        q_ms = jnp.concatenate([_group_mean_sq(aq[:, i:i + gw], hmat) for i in range(0, ATTN_Q_W, gw)], axis=1)
        k_ms = _group_mean_sq(ak, hmat[0:ATTN_KV_W, 0:ATTN_KV_W])
        mvt_ref[:, r] = proj(c_mv, M_W).T.astype(BF16)
        mo_ref[r, :] = proj(c_mo, M_W).astype(BF16)
        mgt_ref[:, r] = proj(c_g, LANES).T[0:N_GATES, :]

        def tail():
            cos = cos_ref[r, :]
            sin_signed = jnp.where(_first_half(cos.shape), -sin_ref[r, :], sin_ref[r, :])
            qn = aq * lax.rsqrt(q_ms + EPS) * qg_ref[...]
            qa_ref[r, :] = (_rope(qn, cos, sin_signed) * (scale * LOG2E)).astype(BF16)
            kn = ak * lax.rsqrt(k_ms + EPS) * kg_ref[...]
            kr = _rope(kn, cos, sin_signed)
            ks = pltpu.roll(kr, HEAD_DIM, 1)
            low = lax.broadcasted_iota(jnp.int32, kr.shape, 1) < HEAD_DIM
            kd_ref[r, 0:LANES] = jnp.where(low, kr, ks).astype(BF16)
            kd_ref[r, LANES:2 * LANES] = jnp.where(low, ks, kr).astype(BF16)
            vat_ref[:, r] = av.T.astype(BF16)
            u = _gelu(su)
            vn = (v * lax.rsqrt(v_ms + EPS) * ng_ref[...]).astype(BF16)
            for c in range(nrows // BLK):
                vc = vn[c * BLK:(c + 1) * BLK]
                stack = jnp.concatenate(
                    [jnp.where(lane_group == g, vc, jnp.zeros_like(vc)) for g in range(SGU_GROUPS)], axis=0)
                mixed = _dot(ws_ref[...], stack)
                ys_ref[r0 + c * BLK:r0 + (c + 1) * BLK, :] = (
                    u[c * BLK:(c + 1) * BLK] * (mixed + bs_ref[...])).astype(BF16)

        return tail

    nrows = x_ref.shape[0] // MIX_SPLIT
    tails = [projections(i * nrows, nrows) for i in range(MIX_SPLIT)]
    for tail in tails:
        tail()


def _mix_in(x, layer, g, w_in_t, q_g, k_g, cos, sin, hmat, sgu_g, sgu_w, sgu_b, tm=MIX_TM):
    T = x.shape[0]
    d_in, d_model = w_in_t.shape[1:]
    chunk = d_in // W_IN_CHUNKS
    assert chunk * W_IN_CHUNKS == d_in and chunk % 16 == 0
    row = lambda w: pl.BlockSpec((tm, w), lambda i: (i, 0))
    tcol = lambda r: pl.BlockSpec((r, tm), lambda i: (0, i))
    out_specs = [row(ATTN_Q_W), row(2 * ATTN_KV_W), tcol(ATTN_KV_W), row(2 * M_W), tcol(M_W), row(M_W),
                 tcol(N_GATES), row(SGU_W)]
    out_shape = [jax.ShapeDtypeStruct(s, d) for s, d in [
        ((T, ATTN_Q_W), BF16), ((T, 2 * ATTN_KV_W), BF16), ((ATTN_KV_W, T), BF16), ((T, 2 * M_W), BF16),
        ((M_W, T), BF16), ((T, M_W), BF16), ((N_GATES, T), F32), ((T, SGU_W), BF16)]]
    ws_cat = jnp.transpose(sgu_w, (1, 0, 2)).reshape(BLK, SGU_GROUPS * BLK).astype(BF16)
    bs_full = jnp.repeat(jnp.transpose(sgu_b), HEAD_DIM, axis=1)
    return pl.pallas_call(
        functools.partial(_mixin_kernel, layer=layer),
        grid=(T // tm,),
        in_specs=[row(D_MODEL), _const_spec((1, D_MODEL)), pl.BlockSpec(memory_space=pl.ANY),
                  _const_spec((1, ATTN_Q_W)), _const_spec((1, ATTN_KV_W)),
                  row(LANES), row(LANES), _const_spec(hmat.shape),
                  _const_spec((1, SGU_W)), _const_spec(ws_cat.shape), _const_spec(bs_full.shape)],
        out_specs=out_specs,
        out_shape=out_shape,
        scratch_shapes=[pltpu.VMEM((d_in, d_model), BF16),
                        pltpu.VMEM((W_SLOTS, chunk, d_model), F32),
                        pltpu.SemaphoreType.DMA((W_SLOTS,))],
        compiler_params=_params("arbitrary"),
        name="mix_in",
    )(x, g.reshape(1, D_MODEL), w_in_t,
      jnp.tile(q_g, N_Q_HEADS).reshape(1, ATTN_Q_W), jnp.tile(k_g, N_KV_HEADS).reshape(1, ATTN_KV_W),
      cos, sin, hmat, sgu_g.reshape(1, SGU_W), ws_cat, bs_full)


def _attn_blocks(sink_ref, q_ref, kd_ref, vt_ref, o_ref):
    S = q_ref.shape[0]
    nb = S // BLK
    kc = lax.broadcasted_iota(jnp.int32, (BLK, BLK), 0)
    qi = lax.broadcasted_iota(jnp.int32, (BLK, BLK), 1)
    prev_bias = jnp.tile(jnp.where(kc >= qi, 0.0, NEG), (1, Q_PER_KV))
    next_bias = jnp.tile(jnp.where(kc <= qi, 0.0, NEG), (1, Q_PER_KV))
    half_mask = (qi < HEAD_DIM, qi >= HEAD_DIM)
    head_lane = lax.broadcasted_iota(jnp.int32, (1, Q_PER_KV * BLK), 1) >> 7
    sink_rows = []
    for kvh in range(N_KV_HEADS):
        row = jnp.zeros((1, Q_PER_KV * BLK), F32)
        for g in range(Q_PER_KV):
            row = jnp.where(head_lane == g, sink_ref[kvh * Q_PER_KV + g] * LOG2E, row)
        sink_rows.append(row)

    def block(q0, start, biases, with_scores):
        nk = len(biases)
        scores = []
        for kvh in range(N_KV_HEADS if with_scores else 0):
            kd = kd_ref[pl.ds(start, nk * BLK), kvh * LANES:(kvh + 1) * LANES]
            qs = []
            for g in range(Q_PER_KV):
                t = kvh * Q_PER_KV + g
                qt = q_ref[pl.ds(q0, BLK), (t // 2) * LANES:(t // 2 + 1) * LANES]
                qs.append(jnp.where(half_mask[t & 1], qt, jnp.zeros_like(qt)))
            scores.append(_dot_nt(kd, jnp.concatenate(qs, axis=0)))

        def weighted_values(scores):
            ones = jnp.ones((HEAD_DIM, nk * BLK), BF16)
            outs = []
            for kvh in range(N_KV_HEADS):
                v_ext = jnp.concatenate([vt_ref[kvh * HEAD_DIM:(kvh + 1) * HEAD_DIM, pl.ds(start, nk * BLK)], ones],
                                        axis=0)
                parts = [scores[kvh][j * BLK:(j + 1) * BLK] for j in range(nk)]
                parts = [p if b is None else p + b for p, b in zip(parts, biases)]
                mx = jnp.max(functools.reduce(jnp.maximum, parts), axis=0, keepdims=True)
                m = jnp.maximum(mx, sink_rows[kvh])
                p = jnp.concatenate([jnp.exp2(p - m) for p in parts], axis=0).astype(BF16)
                oe = _dot(v_ext, p)
                denom = oe[HEAD_DIM:HEAD_DIM + 1] + jnp.exp2(sink_rows[kvh] - m)
                on = oe[0:HEAD_DIM] / denom
                for pair in range(Q_PER_KV // 2):
                    two = jnp.concatenate([on[:, (2 * pair) * BLK:(2 * pair + 1) * BLK],
                                           on[:, (2 * pair + 1) * BLK:(2 * pair + 2) * BLK]], axis=0)
                    outs.append(two.T)
            result = jnp.concatenate(outs, axis=1).astype(BF16)

            def commit():
                o_ref[pl.ds(q0, BLK), :] = result
            return commit

        return (scores if with_scores else None), weighted_values

    def attend(n, with_scores=True):
        if isinstance(n, int) and n == 0:
            return block(0, 0, [None, next_bias], with_scores)
        if isinstance(n, int) and n == nb - 1:
            return block(n * BLK, (n - 1) * BLK, [prev_bias, None], with_scores)
        return block(_aligned(n * BLK, BLK), _aligned((n - 1) * BLK, BLK), [prev_bias, None, next_bias], with_scores)

    return attend


def _log_sigmoid(x):
    return jnp.minimum(x, 0.0) - jnp.log(1.0 + jnp.exp(-jnp.abs(x)))


def _mlstm_chunks(qk_ref, vt_ref, mo_ref, mgt_ref, gb_ref, hg_ref, o_ref,
                  gt_s, b_s, w_s, mloc_s, tot_s, rcol_s, cloc_s, cprev_s, mprev_s, keep_s, add_s):
    S = qk_ref.shape[0]
    L = BLK
    nc = S // L
    ng = N_GATES
    sub = lax.broadcasted_iota(jnp.int32, (L, L), 0)
    lane = lax.broadcasted_iota(jnp.int32, (L, L), 1)

    for c in range(nc):
        gt_s[c * ng:(c + 1) * ng, :] = mgt_ref[:, c * L:(c + 1) * L] + gb_ref[...]
    gt = gt_s[...]
    row = lax.broadcasted_iota(jnp.int32, gt.shape, 0)
    gt = jnp.where(((row >> 2) & 1) == 1, _log_sigmoid(gt), gt)
    gt = gt * LOG2E
    upper = jnp.where(sub <= lane, 1.0, 0.0).astype(BF16)
    g1 = gt.astype(BF16)
    r1 = gt - g1.astype(F32)
    g2 = r1.astype(BF16)
    g3 = (r1 - g2.astype(F32)).astype(BF16)
    pre = _dot(g1, upper) + _dot(g2, upper) + _dot(g3, upper)
    tot = jnp.broadcast_to(pre[:, L - 1:L], pre.shape)
    bsum = jnp.where((row & (ng - 1)) < ng // 2, pre, tot - pre + gt)
    li = pltpu.roll(gt, M_HEADS, 0)
    a = tot - bsum + li
    mloc = jnp.broadcast_to(jnp.max(a, axis=1, keepdims=True), a.shape)
    b_s[...] = bsum
    w_s[...] = jnp.exp2(a - mloc)
    mloc_s[...] = mloc
    tot_s[...] = tot
    rdiff = li - bsum
    zpad = jnp.zeros((L - ng, L), F32)
    for c in range(nc):
        rcol_s[c * L:(c + 1) * L, :] = jnp.concatenate([rdiff[c * ng:(c + 1) * ng, :], zpad], axis=0).T

    def value_rows(hd, r0, dtype):
        vt = vt_ref[hd * HEAD_DIM:(hd + 1) * HEAD_DIM, pl.ds(r0, L)].astype(dtype)
        return jnp.concatenate([vt, jnp.ones((V_ROWS - HEAD_DIM, L), dtype)], axis=0)

    def local_state(c, carry):
        r0 = pl.multiple_of(c * L, L)
        g0 = pl.multiple_of(c * ng, ng)
        k = qk_ref[pl.ds(r0, L), M_W:2 * M_W]
        w = w_s[pl.ds(g0, ng), :]
        for hd in range(M_HEADS):
            vte = value_rows(hd, r0, F32)
            lhs = jnp.concatenate([vte * w[4 + hd:5 + hd, :], vte * w[12 + hd:13 + hd, :]], axis=0)
            cl = _dot(lhs.astype(BF16), k[:, (hd // 2) * LANES:(hd // 2 + 1) * LANES])
            cloc_s[0, hd, c] = cl[0:V_ROWS]
            cloc_s[1, hd, c] = cl[V_ROWS:2 * V_ROWS]
        return carry

    lax.fori_loop(0, nc, local_state, 0, unroll=8)

    jrow = lax.broadcasted_iota(jnp.int32, (ng, L), 0)

    def scan_stabilisers(i, m):
        cf = i
        cb = nc - 1 - i
        gf = pl.multiple_of(cf * ng, ng)
        gb = pl.multiple_of(cb * ng, ng)
        is_fwd = jrow < ng // 2
        blast = jnp.where(is_fwd, tot_s[pl.ds(gf, ng), :], tot_s[pl.ds(gb, ng), :])
        mloc_i = jnp.where(is_fwd, mloc_s[pl.ds(gf, ng), :], mloc_s[pl.ds(gb, ng), :])
        mprev_s[0, pl.ds(gf, ng), :] = m
        mprev_s[1, pl.ds(gb, ng), :] = m
        m_new = jnp.maximum(blast + m, mloc_i)
        gi = pl.multiple_of(i * ng, ng)
        keep_s[pl.ds(gi, ng), :] = jnp.exp2(blast + m - m_new)
        add_s[pl.ds(gi, ng), :] = jnp.exp2(mloc_i - m_new)
        return m_new

    lax.fori_loop(0, nc, scan_stabilisers, jnp.zeros((ng, L), F32))

    for h0 in range(0, M_HEADS, SCAN_HEADS):
        def scan_states(i, states, h0=h0):
            gi = pl.multiple_of(i * ng, ng)
            keep = keep_s[pl.ds(gi, ng), :]
            add = add_s[pl.ds(gi, ng), :]
            new_states = []
            for j in range(SCAN_HEADS):
                for d, ci in enumerate((i, nc - 1 - i)):
                    hd = h0 + j
                    r = 4 + hd + 8 * d
                    st = states[2 * j + d]
                    cprev_s[d, hd, ci] = st.astype(BF16)
                    new_states.append(keep[r:r + 1, :] * st + add[r:r + 1, :] * cloc_s[d, hd, ci])
            return tuple(new_states)

        lax.fori_loop(0, nc, scan_states, tuple(jnp.zeros((V_ROWS, LANES), F32) for _ in range(2 * SCAN_HEADS)))

    causal = (sub <= lane, sub >= lane)

    def outputs(c, with_scores=True):
        r0 = _aligned(c * L, L)
        g0 = _aligned(c * ng, ng)
        q = qk_ref[pl.ds(r0, L), 0:M_W]
        k = qk_ref[pl.ds(r0, L), M_W:2 * M_W]
        bc = b_s[pl.ds(g0, ng), :]
        mprev = (mprev_s[0, pl.ds(g0, ng), :], mprev_s[1, pl.ds(g0, ng), :])
        rc = rcol_s[pl.ds(r0, L), :]
        ys = []
        for hd in range(M_HEADS if with_scores else 0):
            t = hd // 2
            qp = q[:, t * LANES:(t + 1) * LANES]
            qm = jnp.where((lane >> 6) == (hd & 1), qp, jnp.zeros_like(qp))
            x = jnp.concatenate([k[:, t * LANES:(t + 1) * LANES], cprev_s[0, hd, c], cprev_s[1, hd, c]], axis=0)
            ys.append(_dot_nt(x, qm))

        def combine(ys):
            houts = []
            for hd in range(M_HEADS):
                y = ys[hd]
                st = y[0:L]
                pts, scs, mts = [], [], []
                for d in range(2):
                    r = 4 + hd + 8 * d
                    b_row = bc[r:r + 1, :]
                    dm = jnp.where(causal[d], rc[:, r:r + 1] + b_row, NEG)
                    inter = b_row + mprev[d][r:r + 1, :]
                    m_t = jnp.maximum(inter, jnp.max(dm, axis=0, keepdims=True))
                    pts.append((jnp.exp2(dm - m_t) * st).astype(BF16))
                    scs.append(jnp.exp2(inter - m_t))
                    mts.append(m_t)
                n2 = _dot(value_rows(hd, r0, BF16), jnp.concatenate(pts, axis=1))
                ht = None
                for d in range(2):
                    tt = n2[:, d * L:(d + 1) * L] + scs[d] * y[L + d * V_ROWS:L + (d + 1) * V_ROWS]
                    den = tt[HEAD_DIM:HEAD_DIM + 1, :]
                    hd_out = tt[0:HEAD_DIM] / jnp.maximum(jnp.abs(den), jnp.exp2(-mts[d]))
                    ht = hd_out if ht is None else ht + hd_out
                ms = jnp.mean(ht * ht, axis=0, keepdims=True)
                houts.append(ht * lax.rsqrt(ms + EPS))
            hn = (jnp.concatenate(houts, axis=0) * hg_ref[...]).T
            result = (jax.nn.sigmoid(mo_ref[pl.ds(r0, L), :].astype(F32)) * hn).astype(BF16)

            def commit():
                o_ref[pl.ds(r0, L), :] = result
            return commit

        return (ys if with_scores else None), combine

    return outputs


def _mixers_kernel(sink_ref, q_ref, kd_ref, vat_ref, qk_ref, mvt_ref, mo_ref, mgt_ref, gb_ref, hg_ref,
                   ya_ref, ym_ref, *scratch):
    *mlstm_scratch, s_buf, y_buf = scratch
    nb = q_ref.shape[0] // BLK
    n_groups = (nb - 2) // 2
    attend = _attn_blocks(sink_ref, q_ref, kd_ref, vat_ref, ya_ref)
    outputs = _mlstm_chunks(qk_ref, mvt_ref, mo_ref, mgt_ref, gb_ref, hg_ref, ym_ref, *mlstm_scratch)
    group = lambda g: (1 + 2 * g, 2 + 2 * g)

    def first(blocks):
        return [f(n)[0] for n in blocks for f in (attend, outputs)]

    def second(blocks, results):
        finishes = [f(n, with_scores=False)[1] for n in blocks for f in (attend, outputs)]
        commits = [finish(r) for finish, r in zip(finishes, results)]
        for commit in commits:
            commit()

    def park(slot, results):
        for b in range(2):
            for i, s in enumerate(results[2 * b]):
                s_buf[slot, N_KV_HEADS * b + i, 0:s.shape[0], :] = s
            for i, y in enumerate(results[2 * b + 1]):
                y_buf[slot, M_HEADS * b + i] = y

    def fetch(slot, key_blocks):
        results = []
        for b in range(2):
            results.append([s_buf[slot, N_KV_HEADS * b + i, 0:key_blocks * BLK, :] for i in range(N_KV_HEADS)])
            results.append([y_buf[slot, M_HEADS * b + i] for i in range(M_HEADS)])
        return results

    edges = (0, nb - 1)

    def advance(g, slot):
        park(slot, first(group(g)))
        if isinstance(g, int) and g == 0:
            second(edges, fetch(1 - slot, 2))
        else:
            second(group(g - 1), fetch(1 - slot, 3))

    def two_steps(j, carry):
        advance(1 + 2 * j, 0)
        advance(2 + 2 * j, 1)
        return carry

    park(0, first(edges))
    advance(0, 1)
    lax.fori_loop(0, (n_groups - 1) // 2, two_steps, 0)
    last_slot = 1
    if (n_groups - 1) % 2:
        advance(n_groups - 1, 0)
        last_slot = 0
    second(group(n_groups - 1), fetch(last_slot, 3))


def _mixers(qa, kd, vat, sink, mqk, mvt, mo, mgt, gate_b, head_g, B, S):
    seq = lambda w: pl.BlockSpec((S, w), lambda b: (b, 0))
    tseq = lambda r: pl.BlockSpec((r, S), lambda b: (0, b))
    nc = S // BLK
    assert nc % 2 == 0 and nc >= 4
    gb = jnp.broadcast_to(gate_b.reshape(N_GATES, 1), (N_GATES, BLK))
    hg_t = jnp.broadcast_to(head_g.reshape(M_W, 1), (M_W, BLK))
    rows = nc * N_GATES
    return pl.pallas_call(
        _mixers_kernel,
        grid=(B,),
        in_specs=[pl.BlockSpec(memory_space=pltpu.SMEM), seq(ATTN_Q_W), seq(2 * ATTN_KV_W), tseq(ATTN_KV_W),
                  seq(2 * M_W), tseq(M_W), seq(M_W), tseq(N_GATES), _const_spec((N_GATES, BLK)),
                  _const_spec((M_W, BLK))],
        out_specs=[seq(ATTN_Q_W), seq(M_W)],
        out_shape=[jax.ShapeDtypeStruct((B * S, ATTN_Q_W), BF16), jax.ShapeDtypeStruct((B * S, M_W), BF16)],
        scratch_shapes=[pltpu.VMEM((rows, BLK), F32),
                        pltpu.VMEM((rows, BLK), F32),
                        pltpu.VMEM((rows, BLK), F32),
                        pltpu.VMEM((rows, BLK), F32),
                        pltpu.VMEM((rows, BLK), F32),
                        pltpu.VMEM((S, LANES), F32),
                        pltpu.VMEM((2, M_HEADS, nc, V_ROWS, LANES), F32),
                        pltpu.VMEM((2, M_HEADS, nc, V_ROWS, LANES), BF16),
                        pltpu.VMEM((2, rows, BLK), F32),
                        pltpu.VMEM((rows, BLK), F32),
                        pltpu.VMEM((rows, BLK), F32),
                        pltpu.VMEM((2, 2 * N_KV_HEADS, 3 * BLK, Q_PER_KV * BLK), F32),
                        pltpu.VMEM((2, 2 * M_HEADS, BLK + 2 * V_ROWS, BLK), F32)],
        compiler_params=_params("parallel"),
        name="mixers",
    )(sink, qa, kd, vat, mqk, mvt, mo, mgt, gb, hg_t)


def kernel(x, positions, norm_ffn1_g, ffn1_w_gate, ffn1_w_up, ffn1_w_down, norm_mix_g, w_in, q_norm_g, k_norm_g, attn_sink, mlstm_gate_b, mlstm_head_g, sgu_norm_g, sgu_w_s, sgu_b, w_out, norm_ffn2_g, ffn2_w_gate, ffn2_w_up, ffn2_w_down, norm_out_g):
    B, S, D = x.shape
    depth = w_in.shape[0]
    T = B * S
    xt = x.reshape(T, D)
    group = np.arange(M_W) // HEAD_DIM
    hmat = jnp.asarray((group[:, None] == group[None, :]) / HEAD_DIM, BF16)
    w_in_t = jnp.swapaxes(w_in, 1, 2)
    for l in range(depth):
        if l == 0:
            xt, cos, sin = _ffn(xt, l, norm_ffn1_g[l], ffn1_w_gate, ffn1_w_up, ffn1_w_down,
                                rope=_rope_inputs(positions))
        else:
            xt = _ffn(xt, l, norm_ffn1_g[l], ffn1_w_gate, ffn1_w_up, ffn1_w_down)
        qa, kd, vat, mqk, mvt, mo, mgt, ys = _mix_in(xt, l, norm_mix_g[l], w_in_t, q_norm_g[l],
                                                     k_norm_g[l], cos, sin, hmat, sgu_norm_g[l], sgu_w_s[l], sgu_b[l])
        ya, ym = _mixers(qa, kd, vat, attn_sink[l], mqk, mvt, mo, mgt, mlstm_gate_b[l], mlstm_head_g[l], B, S)
        xt = _ffn(xt, l, norm_ffn2_g[l], ffn2_w_gate, ffn2_w_up, ffn2_w_down, proj=(ya, ym, ys, w_out),
                  gout=norm_out_g[l])
    return xt.reshape(B, S, D)
```

```python
import functools

import numpy as np
import jax
import jax.numpy as jnp
from jax import lax
from jax.experimental import pallas as pl
from jax.experimental.pallas import tpu as pltpu

F32 = jnp.float32
BF16 = jnp.bfloat16

D_MODEL = 1024
HEAD_DIM = 64
N_Q_HEADS = 8
N_KV_HEADS = 2
Q_PER_KV = N_Q_HEADS // N_KV_HEADS
ATTN_Q_W = N_Q_HEADS * HEAD_DIM
ATTN_KV_W = N_KV_HEADS * HEAD_DIM
WINDOW = 128
BLK = 128
ROPE_THETA = 10000.0
M_HEADS = 4
M_W = M_HEADS * HEAD_DIM
N_GATES = 4 * M_HEADS
V_ROWS = HEAD_DIM + 16
SCAN_HEADS = 2
SGU_GROUPS = 4
SGU_W = SGU_GROUPS * HEAD_DIM
EPS = 1e-6
LANES = 128
NEG = -1e30
LOG2E = float(np.log2(np.e))

VMEM_LIMIT = 56 * 1024 * 1024
FFN_TM = 512
FFN_SPLIT = 2
W_CHUNKS = 16
W_SLOTS = 8
W_OUT_CHUNKS = 8
W_IN_CHUNKS = 29
MIX_TM = 1024
MIX_SPLIT = 2


def _dot(a, b):
    return jnp.dot(a, b, preferred_element_type=F32)


def _dot_nt(a, b):
    return lax.dot_general(a, b, (((1,), (1,)), ((), ())), preferred_element_type=F32)


def _aligned(x, m):
    return x if isinstance(x, int) else pl.multiple_of(x, m)


def _rms(x, g):
    ms = jnp.mean(x * x, axis=-1, keepdims=True)
    return x * lax.rsqrt(ms + EPS) * g


def _group_mean_sq(t, hmat):
    return _dot((t * t).astype(BF16), hmat)


def _const_spec(shape):
    nd = len(shape)
    return pl.BlockSpec(shape, lambda *_: (0,) * nd, pipeline_mode=pl.Buffered(1))


def _params(semantics):
    return pltpu.CompilerParams(dimension_semantics=(semantics,), vmem_limit_bytes=VMEM_LIMIT)


ROPE_HALF = HEAD_DIM // 2
ROPE_PACK = LANES // ROPE_HALF


def _rope_kernel(pos_ref, freq_ref, cos_ref, sin_ref):
    ang = pos_ref[...] * freq_ref[...]
    rows = ang.shape[0]
    token = lax.broadcasted_iota(jnp.int32, ang.shape, 1) >> 5
    for table, out_ref in ((jnp.cos(ang), cos_ref), (jnp.sin(ang), sin_ref)):
        for i in range(ROPE_PACK):
            x = jnp.where(token == i, table, 0.0)
            y = x + pltpu.roll(x, 2 * ROPE_HALF, 1)
            out_ref[pl.ds(i, rows, stride=ROPE_PACK), :] = y + pltpu.roll(y, ROPE_HALF, 1)


def _rope_inputs(positions):
    T = positions.size
    freqs = ROPE_THETA ** (-jnp.arange(0, HEAD_DIM, 2, dtype=F32) / HEAD_DIM)
    pos = jnp.repeat(positions.reshape(T // ROPE_PACK, ROPE_PACK).astype(F32), ROPE_HALF, axis=1)
    return pos, jnp.tile(freqs, ROPE_PACK).reshape(1, LANES)


def _load_weight_bf16(w_hbm, layer, dst, stage, sems):
    n_slots, rows, cols = stage.shape
    n_chunks = dst.shape[0] // rows

    def copy(c):
        slot = c % n_slots
        return pltpu.make_async_copy(w_hbm.at[layer, pl.ds(c * rows, rows), pl.ds(0, cols)], stage.at[slot],
                                     sems.at[slot])

    for c in range(min(n_slots - 1, n_chunks)):
        copy(c).start()
    for c in range(n_chunks):
        if c + n_slots - 1 < n_chunks:
            copy(c + n_slots - 1).start()
        copy(c).wait()
        dst[c * rows:(c + 1) * rows, :] = stage[c % n_slots].astype(BF16)


def _ffn_kernel(*refs, layer, has_rope, has_proj, has_final):
    refs = list(refs)
    x_ref = refs.pop(0)
    if has_rope:
        pos_ref, freq_ref = refs.pop(0), refs.pop(0)
    if has_proj:
        ya_ref, ym_ref, ys_ref, wo_hbm = refs[:4]
        refs = refs[4:]
    g_ref, wg_hbm, wu_hbm, wd_hbm = refs[:4]
    refs = refs[4:]
    if has_final:
        gout_ref = refs.pop(0)
    out_ref = refs.pop(0)
    if has_rope:
        cos_ref, sin_ref = refs.pop(0), refs.pop(0)
    wg_ref, wu_ref, wd_ref, stage_in, stage_out, sems = refs[:6]
    if has_proj:
        wo_ref = refs[6]

    @pl.when(pl.program_id(0) == 0)
    def _():
        if has_proj:
            rows = wo_ref.shape[0] // W_OUT_CHUNKS
            _load_weight_bf16(wo_hbm, layer, wo_ref, stage_out.at[:, 0:rows, :], sems)
        _load_weight_bf16(wg_hbm, layer, wg_ref, stage_in, sems)
        _load_weight_bf16(wu_hbm, layer, wu_ref, stage_in, sems)
        _load_weight_bf16(wd_hbm, layer, wd_ref, stage_out, sems)

    hs = x_ref.shape[0] // FFN_SPLIT
    rows = [slice(i * hs, (i + 1) * hs) for i in range(FFN_SPLIT)]
    xs = []
    for r in rows:
        x = x_ref[r, :]
        if has_proj:
            x = (x + _dot(ya_ref[r, :], wo_ref[0:ATTN_Q_W, :])
                 + _dot(ym_ref[r, :], wo_ref[ATTN_Q_W:ATTN_Q_W + M_W, :])
                 + _dot(ys_ref[r, :], wo_ref[ATTN_Q_W + M_W:, :]))
        xs.append(x)
    hidden = [_rms(x, g_ref[...]).astype(BF16) for x in xs]
    gate_up = [(_dot(h, wg_ref[...]), _dot(h, wu_ref[...])) for h in hidden]
    for r, x, (gate, up) in zip(rows, xs, gate_up):
        act = (gate * jax.nn.sigmoid(gate) * up).astype(BF16)
        y = x + 0.5 * _dot(act, wd_ref[...])
        if has_final:
            y = _rms(y, gout_ref[...])
        out_ref[r, :] = y

    if has_rope:
        _rope_kernel(pos_ref, freq_ref, cos_ref, sin_ref)


def _ffn(x, layer, g, wg, wu, wd, rope=None, proj=None, gout=None, tm=FFN_TM):
    T = x.shape[0]
    d_model, d_ff = wg.shape[1:]
    row = lambda w: pl.BlockSpec((tm, w), lambda i: (i, 0))
    hbm = pl.BlockSpec(memory_space=pl.ANY)
    args, specs = [x], [row(D_MODEL)]
    out_specs, out_shape = row(D_MODEL), jax.ShapeDtypeStruct((T, D_MODEL), F32)
    if rope is not None:
        args += list(rope)
        specs += [pl.BlockSpec((tm // ROPE_PACK, LANES), lambda i: (i, 0)), _const_spec((1, LANES))]
        out_specs = [out_specs, row(LANES), row(LANES)]
        out_shape = [out_shape] + [jax.ShapeDtypeStruct((T, LANES), F32)] * 2
    if proj is not None:
        ya, ym, ys, wo = proj
        args += [ya, ym, ys, wo]
        specs += [row(ATTN_Q_W), row(M_W), row(SGU_W), hbm]
    args += [g.reshape(1, D_MODEL), wg, wu, wd]
    specs += [_const_spec((1, D_MODEL)), hbm, hbm, hbm]
    if gout is not None:
        args.append(gout.reshape(1, D_MODEL))
        specs.append(_const_spec((1, D_MODEL)))
    return pl.pallas_call(
        functools.partial(_ffn_kernel, layer=layer, has_rope=rope is not None, has_proj=proj is not None,
                          has_final=gout is not None),
        grid=(T // tm,),
        in_specs=specs,
        out_specs=out_specs,
        out_shape=out_shape,
        scratch_shapes=[pltpu.VMEM((d_model, d_ff), BF16), pltpu.VMEM((d_model, d_ff), BF16),
                        pltpu.VMEM((d_ff, d_model), BF16),
                        pltpu.VMEM((W_SLOTS, d_model // W_CHUNKS, d_ff), F32),
                        pltpu.VMEM((W_SLOTS, d_ff // W_CHUNKS, d_model), F32),
                        pltpu.SemaphoreType.DMA((W_SLOTS,))]
        + ([pltpu.VMEM(proj[3].shape[1:], BF16)] if proj is not None else []),
        compiler_params=_params("arbitrary"),
        name="ffn",
    )(*args)


def _first_half(shape):
    lane = lax.broadcasted_iota(jnp.int32, shape, 1)
    return (lane & (HEAD_DIM - 1)) < HEAD_DIM // 2


def _rope(t, cos, sin_signed):
    n = t.shape[-1]
    reps = n // LANES
    swapped = jnp.where(_first_half(t.shape),
                        pltpu.roll(t, n - HEAD_DIM // 2, 1), pltpu.roll(t, HEAD_DIM // 2, 1))
    return t * jnp.tile(cos, (1, reps)) + swapped * jnp.tile(sin_signed, (1, reps))


def _gelu(x):
    return 0.5 * x * (1.0 + lax.erf(x * np.float32(np.sqrt(0.5))))


def _mixin_kernel(x_ref, g_ref, w_hbm, qg_ref, kg_ref, cos_ref, sin_ref, hmat_ref, ng_ref, ws_ref,
                  bs_ref, qa_ref, kd_ref, vat_ref, mqk_ref, mvt_ref, mo_ref, mgt_ref, ys_ref, w_ref, stage, sems,
                  *, layer):
    @pl.when(pl.program_id(0) == 0)
    def _():
        _load_weight_bf16(w_hbm, layer, w_ref, stage, sems)

    scale = HEAD_DIM ** -0.5
    hmat = hmat_ref[...]
    gw = hmat.shape[0]
    c_q, c_kv, c_mq, c_mk, c_mv, c_mo, c_g, c_su, c_sv = (int(c) for c in np.cumsum(
        [0, ATTN_Q_W, 2 * ATTN_KV_W, M_W, M_W, M_W, M_W, N_GATES, SGU_W]))
    lane_group = lax.broadcasted_iota(jnp.int32, (BLK, SGU_W), 1) >> 6

    def projections(r0, nrows):
        r = slice(r0, r0 + nrows)
        h = _rms(x_ref[r, :], g_ref[...]).astype(BF16)
        proj = lambda start, width: _dot_nt(h, w_ref[start:start + width, :])
        su = proj(c_su, SGU_W)
        sv = proj(c_sv, SGU_W)
        aq = proj(c_q, ATTN_Q_W)
        akv = proj(c_kv, 2 * ATTN_KV_W)
        ak, av = akv[:, :ATTN_KV_W], akv[:, ATTN_KV_W:]
        v = _gelu(sv)
        v_ms = _group_mean_sq(v, hmat)
        mqk_ref[r, 0:M_W] = proj(c_mq, M_W).astype(BF16)
        mqk_ref[r, M_W:2 * M_W] = (proj(c_mk, M_W) * scale).astype(BF16)
        q_ms = jnp.concatenate([_group_mean_sq(aq[:, i:i + gw], hmat) for i in range(0, ATTN_Q_W, gw)], axis=1)
        k_ms = _group_mean_sq(ak, hmat[0:ATTN_KV_W, 0:ATTN_KV_W])
        mvt_ref[:, r] = proj(c_mv, M_W).T.astype(BF16)
        mo_ref[r, :] = proj(c_mo, M_W).astype(BF16)
        mgt_ref[:, r] = proj(c_g, LANES).T[0:N_GATES, :]

        def tail():
            cos = cos_ref[r, :]
            sin_signed = jnp.where(_first_half(cos.shape), -sin_ref[r, :], sin_ref[r, :])
            qn = aq * lax.rsqrt(q_ms + EPS) * qg_ref[...]
            qa_ref[r, :] = (_rope(qn, cos, sin_signed) * (scale * LOG2E)).astype(BF16)
            kn = ak * lax.rsqrt(k_ms + EPS) * kg_ref[...]
            kr = _rope(kn, cos, sin_signed)
            ks = pltpu.roll(kr, HEAD_DIM, 1)
            low = lax.broadcasted_iota(jnp.int32, kr.shape, 1) < HEAD_DIM
            kd_ref[r, 0:LANES] = jnp.where(low, kr, ks).astype(BF16)
            kd_ref[r, LANES:2 * LANES] = jnp.where(low, ks, kr).astype(BF16)
            vat_ref[:, r] = av.T.astype(BF16)
            u = _gelu(su)
            vn = (v * lax.rsqrt(v_ms + EPS) * ng_ref[...]).astype(BF16)
            for c in range(nrows // BLK):
                vc = vn[c * BLK:(c + 1) * BLK]
                stack = jnp.concatenate(
                    [jnp.where(lane_group == g, vc, jnp.zeros_like(vc)) for g in range(SGU_GROUPS)], axis=0)
                mixed = _dot(ws_ref[...], stack)
                ys_ref[r0 + c * BLK:r0 + (c + 1) * BLK, :] = (
                    u[c * BLK:(c + 1) * BLK] * (mixed + bs_ref[...])).astype(BF16)

        return tail

    nrows = x_ref.shape[0] // MIX_SPLIT
    tails = [projections(i * nrows, nrows) for i in range(MIX_SPLIT)]
    for tail in tails:
        tail()


def _mix_in(x, layer, g, w_in_t, q_g, k_g, cos, sin, hmat, sgu_g, sgu_w, sgu_b, tm=MIX_TM):
    T = x.shape[0]
    d_in, d_model = w_in_t.shape[1:]
    chunk = d_in // W_IN_CHUNKS
    assert chunk * W_IN_CHUNKS == d_in and chunk % 16 == 0
    row = lambda w: pl.BlockSpec((tm, w), lambda i: (i, 0))
    tcol = lambda r: pl.BlockSpec((r, tm), lambda i: (0, i))
    out_specs = [row(ATTN_Q_W), row(2 * ATTN_KV_W), tcol(ATTN_KV_W), row(2 * M_W), tcol(M_W), row(M_W),
                 tcol(N_GATES), row(SGU_W)]
    out_shape = [jax.ShapeDtypeStruct(s, d) for s, d in [
        ((T, ATTN_Q_W), BF16), ((T, 2 * ATTN_KV_W), BF16), ((ATTN_KV_W, T), BF16), ((T, 2 * M_W), BF16),
        ((M_W, T), BF16), ((T, M_W), BF16), ((N_GATES, T), F32), ((T, SGU_W), BF16)]]
    ws_cat = jnp.transpose(sgu_w, (1, 0, 2)).reshape(BLK, SGU_GROUPS * BLK).astype(BF16)
    bs_full = jnp.repeat(jnp.transpose(sgu_b), HEAD_DIM, axis=1)
    return pl.pallas_call(
        functools.partial(_mixin_kernel, layer=layer),
        grid=(T // tm,),
        in_specs=[row(D_MODEL), _const_spec((1, D_MODEL)), pl.BlockSpec(memory_space=pl.ANY),
                  _const_spec((1, ATTN_Q_W)), _const_spec((1, ATTN_KV_W)),
                  row(LANES), row(LANES), _const_spec(hmat.shape),
                  _const_spec((1, SGU_W)), _const_spec(ws_cat.shape), _const_spec(bs_full.shape)],
        out_specs=out_specs,
        out_shape=out_shape,
        scratch_shapes=[pltpu.VMEM((d_in, d_model), BF16),
                        pltpu.VMEM((W_SLOTS, chunk, d_model), F32),
                        pltpu.SemaphoreType.DMA((W_SLOTS,))],
        compiler_params=_params("arbitrary"),
        name="mix_in",
    )(x, g.reshape(1, D_MODEL), w_in_t,
      jnp.tile(q_g, N_Q_HEADS).reshape(1, ATTN_Q_W), jnp.tile(k_g, N_KV_HEADS).reshape(1, ATTN_KV_W),
      cos, sin, hmat, sgu_g.reshape(1, SGU_W), ws_cat, bs_full)


def _attn_blocks(sink_ref, q_ref, kd_ref, vt_ref, o_ref):
    S = q_ref.shape[0]
    nb = S // BLK
    kc = lax.broadcasted_iota(jnp.int32, (BLK, BLK), 0)
    qi = lax.broadcasted_iota(jnp.int32, (BLK, BLK), 1)
    prev_bias = jnp.tile(jnp.where(kc >= qi, 0.0, NEG), (1, Q_PER_KV))
    next_bias = jnp.tile(jnp.where(kc <= qi, 0.0, NEG), (1, Q_PER_KV))
    half_mask = (qi < HEAD_DIM, qi >= HEAD_DIM)
    head_lane = lax.broadcasted_iota(jnp.int32, (1, Q_PER_KV * BLK), 1) >> 7
    sink_rows = []
    for kvh in range(N_KV_HEADS):
        row = jnp.zeros((1, Q_PER_KV * BLK), F32)
        for g in range(Q_PER_KV):
            row = jnp.where(head_lane == g, sink_ref[kvh * Q_PER_KV + g] * LOG2E, row)
        sink_rows.append(row)

    def block(q0, start, biases, with_scores):
        nk = len(biases)
        scores = []
        for kvh in range(N_KV_HEADS if with_scores else 0):
            kd = kd_ref[pl.ds(start, nk * BLK), kvh * LANES:(kvh + 1) * LANES]
            qs = []
            for g in range(Q_PER_KV):
                t = kvh * Q_PER_KV + g
                qt = q_ref[pl.ds(q0, BLK), (t // 2) * LANES:(t // 2 + 1) * LANES]
                qs.append(jnp.where(half_mask[t & 1], qt, jnp.zeros_like(qt)))
            scores.append(_dot_nt(kd, jnp.concatenate(qs, axis=0)))

        def weighted_values(scores):
            ones = jnp.ones((HEAD_DIM, nk * BLK), BF16)
            outs = []
            for kvh in range(N_KV_HEADS):
                v_ext = jnp.concatenate([vt_ref[kvh * HEAD_DIM:(kvh + 1) * HEAD_DIM, pl.ds(start, nk * BLK)], ones],
                                        axis=0)
                parts = [scores[kvh][j * BLK:(j + 1) * BLK] for j in range(nk)]
                parts = [p if b is None else p + b for p, b in zip(parts, biases)]
                mx = jnp.max(functools.reduce(jnp.maximum, parts), axis=0, keepdims=True)
                m = jnp.maximum(mx, sink_rows[kvh])
                p = jnp.concatenate([jnp.exp2(p - m) for p in parts], axis=0).astype(BF16)
                oe = _dot(v_ext, p)
                denom = oe[HEAD_DIM:HEAD_DIM + 1] + jnp.exp2(sink_rows[kvh] - m)
                on = oe[0:HEAD_DIM] / denom
                for pair in range(Q_PER_KV // 2):
                    two = jnp.concatenate([on[:, (2 * pair) * BLK:(2 * pair + 1) * BLK],
                                           on[:, (2 * pair + 1) * BLK:(2 * pair + 2) * BLK]], axis=0)
                    outs.append(two.T)
            result = jnp.concatenate(outs, axis=1).astype(BF16)

            def commit():
                o_ref[pl.ds(q0, BLK), :] = result
            return commit

        return (scores if with_scores else None), weighted_values

    def attend(n, with_scores=True):
        if isinstance(n, int) and n == 0:
            return block(0, 0, [None, next_bias], with_scores)
        if isinstance(n, int) and n == nb - 1:
            return block(n * BLK, (n - 1) * BLK, [prev_bias, None], with_scores)
        return block(_aligned(n * BLK, BLK), _aligned((n - 1) * BLK, BLK), [prev_bias, None, next_bias], with_scores)

    return attend


def _log_sigmoid(x):
    return jnp.minimum(x, 0.0) - jnp.log(1.0 + jnp.exp(-jnp.abs(x)))


def _mlstm_chunks(qk_ref, vt_ref, mo_ref, mgt_ref, gb_ref, hg_ref, o_ref,
                  gt_s, b_s, w_s, mloc_s, tot_s, rcol_s, cloc_s, cprev_s, mprev_s, keep_s, add_s):
    S = qk_ref.shape[0]
    L = BLK
    nc = S // L
    ng = N_GATES
    sub = lax.broadcasted_iota(jnp.int32, (L, L), 0)
    lane = lax.broadcasted_iota(jnp.int32, (L, L), 1)

    for c in range(nc):
        gt_s[c * ng:(c + 1) * ng, :] = mgt_ref[:, c * L:(c + 1) * L] + gb_ref[...]
    gt = gt_s[...]
    row = lax.broadcasted_iota(jnp.int32, gt.shape, 0)
    gt = jnp.where(((row >> 2) & 1) == 1, _log_sigmoid(gt), gt)
    gt = gt * LOG2E
    upper = jnp.where(sub <= lane, 1.0, 0.0).astype(BF16)
    g1 = gt.astype(BF16)
    r1 = gt - g1.astype(F32)
    g2 = r1.astype(BF16)
    g3 = (r1 - g2.astype(F32)).astype(BF16)
    pre = _dot(g1, upper) + _dot(g2, upper) + _dot(g3, upper)
    tot = jnp.broadcast_to(pre[:, L - 1:L], pre.shape)
    bsum = jnp.where((row & (ng - 1)) < ng // 2, pre, tot - pre + gt)
    li = pltpu.roll(gt, M_HEADS, 0)
    a = tot - bsum + li
    mloc = jnp.broadcast_to(jnp.max(a, axis=1, keepdims=True), a.shape)
    b_s[...] = bsum
    w_s[...] = jnp.exp2(a - mloc)
    mloc_s[...] = mloc
    tot_s[...] = tot
    rdiff = li - bsum
    zpad = jnp.zeros((L - ng, L), F32)
    for c in range(nc):
        rcol_s[c * L:(c + 1) * L, :] = jnp.concatenate([rdiff[c * ng:(c + 1) * ng, :], zpad], axis=0).T

    def value_rows(hd, r0, dtype):
        vt = vt_ref[hd * HEAD_DIM:(hd + 1) * HEAD_DIM, pl.ds(r0, L)].astype(dtype)
        return jnp.concatenate([vt, jnp.ones((V_ROWS - HEAD_DIM, L), dtype)], axis=0)

    def local_state(c, carry):
        r0 = pl.multiple_of(c * L, L)
        g0 = pl.multiple_of(c * ng, ng)
        k = qk_ref[pl.ds(r0, L), M_W:2 * M_W]
        w = w_s[pl.ds(g0, ng), :]
        for hd in range(M_HEADS):
            vte = value_rows(hd, r0, F32)
            lhs = jnp.concatenate([vte * w[4 + hd:5 + hd, :], vte * w[12 + hd:13 + hd, :]], axis=0)
            cl = _dot(lhs.astype(BF16), k[:, (hd // 2) * LANES:(hd // 2 + 1) * LANES])
            cloc_s[0, hd, c] = cl[0:V_ROWS]
            cloc_s[1, hd, c] = cl[V_ROWS:2 * V_ROWS]
        return carry

    lax.fori_loop(0, nc, local_state, 0, unroll=8)

    jrow = lax.broadcasted_iota(jnp.int32, (ng, L), 0)

    def scan_stabilisers(i, m):
        cf = i
        cb = nc - 1 - i
        gf = pl.multiple_of(cf * ng, ng)
        gb = pl.multiple_of(cb * ng, ng)
        is_fwd = jrow < ng // 2
        blast = jnp.where(is_fwd, tot_s[pl.ds(gf, ng), :], tot_s[pl.ds(gb, ng), :])
        mloc_i = jnp.where(is_fwd, mloc_s[pl.ds(gf, ng), :], mloc_s[pl.ds(gb, ng), :])
        mprev_s[0, pl.ds(gf, ng), :] = m
        mprev_s[1, pl.ds(gb, ng), :] = m
        m_new = jnp.maximum(blast + m, mloc_i)
        gi = pl.multiple_of(i * ng, ng)
        keep_s[pl.ds(gi, ng), :] = jnp.exp2(blast + m - m_new)
        add_s[pl.ds(gi, ng), :] = jnp.exp2(mloc_i - m_new)
        return m_new

    lax.fori_loop(0, nc, scan_stabilisers, jnp.zeros((ng, L), F32))

    for h0 in range(0, M_HEADS, SCAN_HEADS):
        def scan_states(i, states, h0=h0):
            gi = pl.multiple_of(i * ng, ng)
            keep = keep_s[pl.ds(gi, ng), :]
            add = add_s[pl.ds(gi, ng), :]
            new_states = []
            for j in range(SCAN_HEADS):
                for d, ci in enumerate((i, nc - 1 - i)):
                    hd = h0 + j
                    r = 4 + hd + 8 * d
                    st = states[2 * j + d]
                    cprev_s[d, hd, ci] = st.astype(BF16)
                    new_states.append(keep[r:r + 1, :] * st + add[r:r + 1, :] * cloc_s[d, hd, ci])
            return tuple(new_states)

        lax.fori_loop(0, nc, scan_states, tuple(jnp.zeros((V_ROWS, LANES), F32) for _ in range(2 * SCAN_HEADS)))

    causal = (sub <= lane, sub >= lane)

    def outputs(c, with_scores=True):
        r0 = _aligned(c * L, L)
        g0 = _aligned(c * ng, ng)
        q = qk_ref[pl.ds(r0, L), 0:M_W]
        k = qk_ref[pl.ds(r0, L), M_W:2 * M_W]
        bc = b_s[pl.ds(g0, ng), :]
        mprev = (mprev_s[0, pl.ds(g0, ng), :], mprev_s[1, pl.ds(g0, ng), :])
        rc = rcol_s[pl.ds(r0, L), :]
        ys = []
        for hd in range(M_HEADS if with_scores else 0):
            t = hd // 2
            qp = q[:, t * LANES:(t + 1) * LANES]
            qm = jnp.where((lane >> 6) == (hd & 1), qp, jnp.zeros_like(qp))
            x = jnp.concatenate([k[:, t * LANES:(t + 1) * LANES], cprev_s[0, hd, c], cprev_s[1, hd, c]], axis=0)
            ys.append(_dot_nt(x, qm))

        def combine(ys):
            houts = []
            for hd in range(M_HEADS):
                y = ys[hd]
                st = y[0:L]
                pts, scs, mts = [], [], []
                for d in range(2):
                    r = 4 + hd + 8 * d
                    b_row = bc[r:r + 1, :]
                    dm = jnp.where(causal[d], rc[:, r:r + 1] + b_row, NEG)
                    inter = b_row + mprev[d][r:r + 1, :]
                    m_t = jnp.maximum(inter, jnp.max(dm, axis=0, keepdims=True))
                    pts.append((jnp.exp2(dm - m_t) * st).astype(BF16))
                    scs.append(jnp.exp2(inter - m_t))
                    mts.append(m_t)
                n2 = _dot(value_rows(hd, r0, BF16), jnp.concatenate(pts, axis=1))
                ht = None
                for d in range(2):
                    tt = n2[:, d * L:(d + 1) * L] + scs[d] * y[L + d * V_ROWS:L + (d + 1) * V_ROWS]
                    den = tt[HEAD_DIM:HEAD_DIM + 1, :]
                    hd_out = tt[0:HEAD_DIM] / jnp.maximum(jnp.abs(den), jnp.exp2(-mts[d]))
                    ht = hd_out if ht is None else ht + hd_out
                ms = jnp.mean(ht * ht, axis=0, keepdims=True)
                houts.append(ht * lax.rsqrt(ms + EPS))
            hn = (jnp.concatenate(houts, axis=0) * hg_ref[...]).T
            result = (jax.nn.sigmoid(mo_ref[pl.ds(r0, L), :].astype(F32)) * hn).astype(BF16)

            def commit():
                o_ref[pl.ds(r0, L), :] = result
            return commit

        return (ys if with_scores else None), combine

    return outputs


def _mixers_kernel(sink_ref, q_ref, kd_ref, vat_ref, qk_ref, mvt_ref, mo_ref, mgt_ref, gb_ref, hg_ref,
                   ya_ref, ym_ref, *scratch):
    *mlstm_scratch, s_buf, y_buf = scratch
    nb = q_ref.shape[0] // BLK
    n_groups = (nb - 2) // 2
    attend = _attn_blocks(sink_ref, q_ref, kd_ref, vat_ref, ya_ref)
    outputs = _mlstm_chunks(qk_ref, mvt_ref, mo_ref, mgt_ref, gb_ref, hg_ref, ym_ref, *mlstm_scratch)
    group = lambda g: (1 + 2 * g, 2 + 2 * g)

    def first(blocks):
        return [f(n)[0] for n in blocks for f in (attend, outputs)]

    def second(blocks, results):
        finishes = [f(n, with_scores=False)[1] for n in blocks for f in (attend, outputs)]
        commits = [finish(r) for finish, r in zip(finishes, results)]
        for commit in commits:
            commit()

    def park(slot, results):
        for b in range(2):
            for i, s in enumerate(results[2 * b]):
                s_buf[slot, N_KV_HEADS * b + i, 0:s.shape[0], :] = s
            for i, y in enumerate(results[2 * b + 1]):
                y_buf[slot, M_HEADS * b + i] = y

    def fetch(slot, key_blocks):
        results = []
        for b in range(2):
            results.append([s_buf[slot, N_KV_HEADS * b + i, 0:key_blocks * BLK, :] for i in range(N_KV_HEADS)])
            results.append([y_buf[slot, M_HEADS * b + i] for i in range(M_HEADS)])
        return results

    edges = (0, nb - 1)

    def advance(g, slot):
        park(slot, first(group(g)))
        if isinstance(g, int) and g == 0:
            second(edges, fetch(1 - slot, 2))
        else:
            second(group(g - 1), fetch(1 - slot, 3))

    def two_steps(j, carry):
        advance(1 + 2 * j, 0)
        advance(2 + 2 * j, 1)
        return carry

    park(0, first(edges))
    advance(0, 1)
    lax.fori_loop(0, (n_groups - 1) // 2, two_steps, 0)
    last_slot = 1
    if (n_groups - 1) % 2:
        advance(n_groups - 1, 0)
        last_slot = 0
    second(group(n_groups - 1), fetch(last_slot, 3))


def _mixers(qa, kd, vat, sink, mqk, mvt, mo, mgt, gate_b, head_g, B, S):
    seq = lambda w: pl.BlockSpec((S, w), lambda b: (b, 0))
    tseq = lambda r: pl.BlockSpec((r, S), lambda b: (0, b))
    nc = S // BLK
    assert nc % 2 == 0 and nc >= 4
    gb = jnp.broadcast_to(gate_b.reshape(N_GATES, 1), (N_GATES, BLK))
    hg_t = jnp.broadcast_to(head_g.reshape(M_W, 1), (M_W, BLK))
    rows = nc * N_GATES
    return pl.pallas_call(
        _mixers_kernel,
        grid=(B,),
        in_specs=[pl.BlockSpec(memory_space=pltpu.SMEM), seq(ATTN_Q_W), seq(2 * ATTN_KV_W), tseq(ATTN_KV_W),
                  seq(2 * M_W), tseq(M_W), seq(M_W), tseq(N_GATES), _const_spec((N_GATES, BLK)),
                  _const_spec((M_W, BLK))],
        out_specs=[seq(ATTN_Q_W), seq(M_W)],
        out_shape=[jax.ShapeDtypeStruct((B * S, ATTN_Q_W), BF16), jax.ShapeDtypeStruct((B * S, M_W), BF16)],
        scratch_shapes=[pltpu.VMEM((rows, BLK), F32),
                        pltpu.VMEM((rows, BLK), F32),
                        pltpu.VMEM((rows, BLK), F32),
                        pltpu.VMEM((rows, BLK), F32),
                        pltpu.VMEM((rows, BLK), F32),
                        pltpu.VMEM((S, LANES), F32),
                        pltpu.VMEM((2, M_HEADS, nc, V_ROWS, LANES), F32),
                        pltpu.VMEM((2, M_HEADS, nc, V_ROWS, LANES), BF16),
                        pltpu.VMEM((2, rows, BLK), F32),
                        pltpu.VMEM((rows, BLK), F32),
                        pltpu.VMEM((rows, BLK), F32),
                        pltpu.VMEM((2, 2 * N_KV_HEADS, 3 * BLK, Q_PER_KV * BLK), F32),
                        pltpu.VMEM((2, 2 * M_HEADS, BLK + 2 * V_ROWS, BLK), F32)],
        compiler_params=_params("parallel"),
        name="mixers",
    )(sink, qa, kd, vat, mqk, mvt, mo, mgt, gb, hg_t)


def kernel(x, positions, norm_ffn1_g, ffn1_w_gate, ffn1_w_up, ffn1_w_down, norm_mix_g, w_in, q_norm_g, k_norm_g, attn_sink, mlstm_gate_b, mlstm_head_g, sgu_norm_g, sgu_w_s, sgu_b, w_out, norm_ffn2_g, ffn2_w_gate, ffn2_w_up, ffn2_w_down, norm_out_g):
    B, S, D = x.shape
    depth = w_in.shape[0]
    T = B * S
    xt = x.reshape(T, D)
    group = np.arange(M_W) // HEAD_DIM
    hmat = jnp.asarray((group[:, None] == group[None, :]) / HEAD_DIM, BF16)
    w_in_t = jnp.swapaxes(w_in, 1, 2)
    for l in range(depth):
        if l == 0:
            xt, cos, sin = _ffn(xt, l, norm_ffn1_g[l], ffn1_w_gate, ffn1_w_up, ffn1_w_down,
                                rope=_rope_inputs(positions))
        else:
            xt = _ffn(xt, l, norm_ffn1_g[l], ffn1_w_gate, ffn1_w_up, ffn1_w_down)
        qa, kd, vat, mqk, mvt, mo, mgt, ys = _mix_in(xt, l, norm_mix_g[l], w_in_t, q_norm_g[l],
                                                     k_norm_g[l], cos, sin, hmat, sgu_norm_g[l], sgu_w_s[l], sgu_b[l])
        ya, ym = _mixers(qa, kd, vat, attn_sink[l], mqk, mvt, mo, mgt, mlstm_gate_b[l], mlstm_head_g[l], B, S)
        xt = _ffn(xt, l, norm_ffn2_g[l], ffn2_w_gate, ffn2_w_up, ffn2_w_down, proj=(ya, ym, ys, w_out),
                  gout=norm_out_g[l])
    return xt.reshape(B, S, D)
```

```python
import functools

import numpy as np
import jax
import jax.numpy as jnp
from jax import lax
from jax.experimental import pallas as pl
from jax.experimental.pallas import tpu as pltpu

F32 = jnp.float32
BF16 = jnp.bfloat16

D_MODEL = 1024
HEAD_DIM = 64
N_Q_HEADS = 8
N_KV_HEADS = 2
Q_PER_KV = N_Q_HEADS // N_KV_HEADS
ATTN_Q_W = N_Q_HEADS * HEAD_DIM
ATTN_KV_W = N_KV_HEADS * HEAD_DIM
WINDOW = 128
BLK = 128
ROPE_THETA = 10000.0
M_HEADS = 4
M_W = M_HEADS * HEAD_DIM
N_GATES = 4 * M_HEADS
V_ROWS = HEAD_DIM + 16
SCAN_HEADS = 2
SGU_GROUPS = 4
SGU_W = SGU_GROUPS * HEAD_DIM
EPS = 1e-6
LANES = 128
NEG = -1e30
LOG2E = float(np.log2(np.e))

VMEM_LIMIT = 56 * 1024 * 1024
FFN_TM = 512
FFN_SPLIT = 2
W_CHUNKS = 8
W_SLOTS = 4
W_OUT_CHUNKS = 4
W_IN_CHUNKS = 5
MIX_TM = 1024
MIX_SPLIT = 2


def _dot(a, b):
    return jnp.dot(a, b, preferred_element_type=F32)


def _dot_nt(a, b):
    return lax.dot_general(a, b, (((1,), (1,)), ((), ())), preferred_element_type=F32)


def _aligned(x, m):
    return x if isinstance(x, int) else pl.multiple_of(x, m)


def _rms(x, g):
    ms = jnp.mean(x * x, axis=-1, keepdims=True)
    return x * lax.rsqrt(ms + EPS) * g


def _group_mean_sq(t, hmat):
    return _dot((t * t).astype(BF16), hmat)


def _const_spec(shape):
    nd = len(shape)
    return pl.BlockSpec(shape, lambda *_: (0,) * nd, pipeline_mode=pl.Buffered(1))


def _params(semantics):
    return pltpu.CompilerParams(dimension_semantics=(semantics,), vmem_limit_bytes=VMEM_LIMIT)


ROPE_HALF = HEAD_DIM // 2
ROPE_PACK = LANES // ROPE_HALF


def _rope_kernel(pos_ref, freq_ref, cos_ref, sin_ref):
    ang = pos_ref[...] * freq_ref[...]
    rows = ang.shape[0]
    token = lax.broadcasted_iota(jnp.int32, ang.shape, 1) >> 5
    for table, out_ref in ((jnp.cos(ang), cos_ref), (jnp.sin(ang), sin_ref)):
        for i in range(ROPE_PACK):
            x = jnp.where(token == i, table, 0.0)
            y = x + pltpu.roll(x, 2 * ROPE_HALF, 1)
            out_ref[pl.ds(i, rows, stride=ROPE_PACK), :] = y + pltpu.roll(y, ROPE_HALF, 1)


def _rope_inputs(positions):
    T = positions.size
    freqs = ROPE_THETA ** (-jnp.arange(0, HEAD_DIM, 2, dtype=F32) / HEAD_DIM)
    pos = jnp.repeat(positions.reshape(T // ROPE_PACK, ROPE_PACK).astype(F32), ROPE_HALF, axis=1)
    return pos, jnp.tile(freqs, ROPE_PACK).reshape(1, LANES)


def _weight_stream(jobs, layer, stage, sems):
    n_slots, rows, cols = stage.shape
    chunks = [(w_hbm, dst, c) for w_hbm, dst in jobs for c in range(dst.shape[0] // rows)]

    def copy(i):
        w_hbm, _, c = chunks[i]
        slot = i % n_slots
        return pltpu.make_async_copy(w_hbm.at[layer, pl.ds(c * rows, rows), pl.ds(0, cols)], stage.at[slot],
                                     sems.at[slot])

    def prefetch():
        for i in range(min(n_slots - 1, len(chunks))):
            copy(i).start()

    def drain():
        for i, (_, dst, c) in enumerate(chunks):
            if i + n_slots - 1 < len(chunks):
                copy(i + n_slots - 1).start()
            copy(i).wait()
            dst[c * rows:(c + 1) * rows, :] = stage[i % n_slots].astype(BF16)

    return prefetch, drain


def _load_weight_bf16(w_hbm, layer, dst, stage, sems):
    prefetch, drain = _weight_stream([(w_hbm, dst)], layer, stage, sems)
    prefetch()
    drain()


def _ffn_kernel(*refs, layer, has_rope, has_proj, has_final):
    refs = list(refs)
    x_ref = refs.pop(0)
    if has_rope:
        pos_ref, freq_ref = refs.pop(0), refs.pop(0)
    if has_proj:
        ya_ref, ym_ref, ys_ref, wo_hbm = refs[:4]
        refs = refs[4:]
    g_ref, wg_hbm, wu_hbm, wd_hbm = refs[:4]
    refs = refs[4:]
    if has_final:
        gout_ref = refs.pop(0)
    out_ref = refs.pop(0)
    if has_rope:
        cos_ref, sin_ref = refs.pop(0), refs.pop(0)
    wg_ref, wu_ref, wd_ref, stage_in, stage_out, sems = refs[:6]
    if has_proj:
        wo_ref = refs[6]

    @pl.when(pl.program_id(0) == 0)
    def _():
        if has_proj:
            rows = wo_ref.shape[0] // W_OUT_CHUNKS
            _load_weight_bf16(wo_hbm, layer, wo_ref, stage_out.at[:, 0:rows, :], sems.at[1])
        start_gu, finish_gu = _weight_stream([(wg_hbm, wg_ref), (wu_hbm, wu_ref)], layer, stage_in, sems.at[0])
        start_d, finish_d = _weight_stream([(wd_hbm, wd_ref)], layer, stage_out, sems.at[1])
        start_gu()
        start_d()
        finish_gu()
        finish_d()

    hs = x_ref.shape[0] // FFN_SPLIT
    rows = [slice(i * hs, (i + 1) * hs) for i in range(FFN_SPLIT)]
    xs = []
    for r in rows:
        x = x_ref[r, :]
        if has_proj:
            x = (x + _dot(ya_ref[r, :], wo_ref[0:ATTN_Q_W, :])
                 + _dot(ym_ref[r, :], wo_ref[ATTN_Q_W:ATTN_Q_W + M_W, :])
                 + _dot(ys_ref[r, :], wo_ref[ATTN_Q_W + M_W:, :]))
        xs.append(x)
    hidden = [_rms(x, g_ref[...]).astype(BF16) for x in xs]
    gate_up = [(_dot(h, wg_ref[...]), _dot(h, wu_ref[...])) for h in hidden]
    for r, x, (gate, up) in zip(rows, xs, gate_up):
        act = (gate * jax.nn.sigmoid(gate) * up).astype(BF16)
        y = x + 0.5 * _dot(act, wd_ref[...])
        if has_final:
            y = _rms(y, gout_ref[...])
        out_ref[r, :] = y

    if has_rope:
        _rope_kernel(pos_ref, freq_ref, cos_ref, sin_ref)


def _ffn(x, layer, g, wg, wu, wd, rope=None, proj=None, gout=None, tm=FFN_TM):
    T = x.shape[0]
    d_model, d_ff = wg.shape[1:]
    row = lambda w: pl.BlockSpec((tm, w), lambda i: (i, 0))
    hbm = pl.BlockSpec(memory_space=pl.ANY)
    args, specs = [x], [row(D_MODEL)]
    out_specs, out_shape = row(D_MODEL), jax.ShapeDtypeStruct((T, D_MODEL), F32)
    if rope is not None:
        args += list(rope)
        specs += [pl.BlockSpec((tm // ROPE_PACK, LANES), lambda i: (i, 0)), _const_spec((1, LANES))]
        out_specs = [out_specs, row(LANES), row(LANES)]
        out_shape = [out_shape] + [jax.ShapeDtypeStruct((T, LANES), F32)] * 2
    if proj is not None:
        ya, ym, ys, wo = proj
        args += [ya, ym, ys, wo]
        specs += [row(ATTN_Q_W), row(M_W), row(SGU_W), hbm]
    args += [g.reshape(1, D_MODEL), wg, wu, wd]
    specs += [_const_spec((1, D_MODEL)), hbm, hbm, hbm]
    if gout is not None:
        args.append(gout.reshape(1, D_MODEL))
        specs.append(_const_spec((1, D_MODEL)))
    return pl.pallas_call(
        functools.partial(_ffn_kernel, layer=layer, has_rope=rope is not None, has_proj=proj is not None,
                          has_final=gout is not None),
        grid=(T // tm,),
        in_specs=specs,
        out_specs=out_specs,
        out_shape=out_shape,
        scratch_shapes=[pltpu.VMEM((d_model, d_ff), BF16), pltpu.VMEM((d_model, d_ff), BF16),
                        pltpu.VMEM((d_ff, d_model), BF16),
                        pltpu.VMEM((W_SLOTS, d_model // W_CHUNKS, d_ff), F32),
                        pltpu.VMEM((W_SLOTS, d_ff // W_CHUNKS, d_model), F32),
                        pltpu.SemaphoreType.DMA((2, W_SLOTS))]
        + ([pltpu.VMEM(proj[3].shape[1:], BF16)] if proj is not None else []),
        compiler_params=_params("arbitrary"),
        name="ffn",
    )(*args)


def _first_half(shape):
    lane = lax.broadcasted_iota(jnp.int32, shape, 1)
    return (lane & (HEAD_DIM - 1)) < HEAD_DIM // 2


def _rope(t, cos, sin_signed):
    n = t.shape[-1]
    reps = n // LANES
    swapped = jnp.where(_first_half(t.shape),
                        pltpu.roll(t, n - HEAD_DIM // 2, 1), pltpu.roll(t, HEAD_DIM // 2, 1))
    return t * jnp.tile(cos, (1, reps)) + swapped * jnp.tile(sin_signed, (1, reps))


def _gelu(x):
    return 0.5 * x * (1.0 + lax.erf(x * np.float32(np.sqrt(0.5))))


def _mixin_kernel(x_ref, g_ref, w_hbm, qg_ref, kg_ref, cos_ref, sin_ref, hmat_ref, ng_ref, ws_ref,
                  bs_ref, qa_ref, kd_ref, vat_ref, mqk_ref, mvt_ref, mo_ref, mgt_ref, ys_ref, w_ref, stage, sems,
                  *, layer):
    @pl.when(pl.program_id(0) == 0)
    def _():
        _load_weight_bf16(w_hbm, layer, w_ref, stage, sems)

    scale = HEAD_DIM ** -0.5
    hmat = hmat_ref[...]
    gw = hmat.shape[0]
    c_q, c_kv, c_mq, c_mk, c_mv, c_mo, c_g, c_su, c_sv = (int(c) for c in np.cumsum(
        [0, ATTN_Q_W, 2 * ATTN_KV_W, M_W, M_W, M_W, M_W, N_GATES, SGU_W]))
    lane_group = lax.broadcasted_iota(jnp.int32, (BLK, SGU_W), 1) >> 6

    def projections(r0, nrows):
        r = slice(r0, r0 + nrows)
        h = _rms(x_ref[r, :], g_ref[...]).astype(BF16)
        proj = lambda start, width: _dot_nt(h, w_ref[start:start + width, :])
        su = proj(c_su, SGU_W)
        sv = proj(c_sv, SGU_W)
        aq = proj(c_q, ATTN_Q_W)
        akv = proj(c_kv, 2 * ATTN_KV_W)
        ak, av = akv[:, :ATTN_KV_W], akv[:, ATTN_KV_W:]
        v = _gelu(sv)
        v_ms = _group_mean_sq(v, hmat)
        mqk_ref[r, 0:M_W] = proj(c_mq, M_W).astype(BF16)
        mqk_ref[r, M_W:2 * M_W] = (proj(c_mk, M_W) * scale).astype(BF16)
        q_ms = jnp.concatenate([_group_mean_sq(aq[:, i:i + gw], hmat) for i in range(0, ATTN_Q_W, gw)], axis=1)
        k_ms = _group_mean_sq(ak, hmat[0:ATTN_KV_W, 0:ATTN_KV_W])
        mvt_ref[:, r] = proj(c_mv, M_W).T.astype(BF16)
        mo_ref[r, :] = proj(c_mo, M_W).astype(BF16)
        mgt_ref[:, r] = proj(c_g, LANES).T[0:N_GATES, :]

        def tail():
            cos = cos_ref[r, :]
            sin_signed = jnp.where(_first_half(cos.shape), -sin_ref[r, :], sin_ref[r, :])
            qn = aq * lax.rsqrt(q_ms + EPS) * qg_ref[...]
            qa_ref[r, :] = (_rope(qn, cos, sin_signed) * (scale * LOG2E)).astype(BF16)
            kn = ak * lax.rsqrt(k_ms + EPS) * kg_ref[...]
            kr = _rope(kn, cos, sin_signed)
            ks = pltpu.roll(kr, HEAD_DIM, 1)
            low = lax.broadcasted_iota(jnp.int32, kr.shape, 1) < HEAD_DIM
            kd_ref[r, 0:LANES] = jnp.where(low, kr, ks).astype(BF16)
            kd_ref[r, LANES:2 * LANES] = jnp.where(low, ks, kr).astype(BF16)
            vat_ref[:, r] = av.T.astype(BF16)
            u = _gelu(su)
            vn = (v * lax.rsqrt(v_ms + EPS) * ng_ref[...]).astype(BF16)
            for c in range(nrows // BLK):
                vc = vn[c * BLK:(c + 1) * BLK]
                stack = jnp.concatenate(
                    [jnp.where(lane_group == g, vc, jnp.zeros_like(vc)) for g in range(SGU_GROUPS)], axis=0)
                mixed = _dot(ws_ref[...], stack)
                ys_ref[r0 + c * BLK:r0 + (c + 1) * BLK, :] = (
                    u[c * BLK:(c + 1) * BLK] * (mixed + bs_ref[...])).astype(BF16)

        return tail

    nrows = x_ref.shape[0] // MIX_SPLIT
    tails = [projections(i * nrows, nrows) for i in range(MIX_SPLIT)]
    for tail in tails:
        tail()


def _mix_in(x, layer, g, w_in_t, q_g, k_g, cos, sin, hmat, sgu_g, sgu_w, sgu_b, tm=MIX_TM):
    T = x.shape[0]
    d_in, d_model = w_in_t.shape[1:]
    chunk = d_in // W_IN_CHUNKS
    assert chunk * W_IN_CHUNKS == d_in and chunk % 16 == 0
    row = lambda w: pl.BlockSpec((tm, w), lambda i: (i, 0))
    tcol = lambda r: pl.BlockSpec((r, tm), lambda i: (0, i))
    out_specs = [row(ATTN_Q_W), row(2 * ATTN_KV_W), tcol(ATTN_KV_W), row(2 * M_W), tcol(M_W), row(M_W),
                 tcol(N_GATES), row(SGU_W)]
    out_shape = [jax.ShapeDtypeStruct(s, d) for s, d in [
        ((T, ATTN_Q_W), BF16), ((T, 2 * ATTN_KV_W), BF16), ((ATTN_KV_W, T), BF16), ((T, 2 * M_W), BF16),
        ((M_W, T), BF16), ((T, M_W), BF16), ((N_GATES, T), F32), ((T, SGU_W), BF16)]]
    ws_cat = jnp.transpose(sgu_w, (1, 0, 2)).reshape(BLK, SGU_GROUPS * BLK).astype(BF16)
    bs_full = jnp.repeat(jnp.transpose(sgu_b), HEAD_DIM, axis=1)
    return pl.pallas_call(
        functools.partial(_mixin_kernel, layer=layer),
        grid=(T // tm,),
        in_specs=[row(D_MODEL), _const_spec((1, D_MODEL)), pl.BlockSpec(memory_space=pl.ANY),
                  _const_spec((1, ATTN_Q_W)), _const_spec((1, ATTN_KV_W)),
                  row(LANES), row(LANES), _const_spec(hmat.shape),
                  _const_spec((1, SGU_W)), _const_spec(ws_cat.shape), _const_spec(bs_full.shape)],
        out_specs=out_specs,
        out_shape=out_shape,
        scratch_shapes=[pltpu.VMEM((d_in, d_model), BF16),
                        pltpu.VMEM((W_SLOTS, chunk, d_model), F32),
                        pltpu.SemaphoreType.DMA((W_SLOTS,))],
        compiler_params=_params("arbitrary"),
        name="mix_in",
    )(x, g.reshape(1, D_MODEL), w_in_t,
      jnp.tile(q_g, N_Q_HEADS).reshape(1, ATTN_Q_W), jnp.tile(k_g, N_KV_HEADS).reshape(1, ATTN_KV_W),
      cos, sin, hmat, sgu_g.reshape(1, SGU_W), ws_cat, bs_full)


def _attn_blocks(sink_ref, q_ref, kd_ref, vt_ref, o_ref):
    S = q_ref.shape[0]
    nb = S // BLK
    kc = lax.broadcasted_iota(jnp.int32, (BLK, BLK), 0)
    qi = lax.broadcasted_iota(jnp.int32, (BLK, BLK), 1)
    prev_bias = jnp.tile(jnp.where(kc >= qi, 0.0, NEG), (1, Q_PER_KV))
    next_bias = jnp.tile(jnp.where(kc <= qi, 0.0, NEG), (1, Q_PER_KV))
    half_mask = (qi < HEAD_DIM, qi >= HEAD_DIM)
    head_lane = lax.broadcasted_iota(jnp.int32, (1, Q_PER_KV * BLK), 1) >> 7
    sink_rows = []
    for kvh in range(N_KV_HEADS):
        row = jnp.zeros((1, Q_PER_KV * BLK), F32)
        for g in range(Q_PER_KV):
            row = jnp.where(head_lane == g, sink_ref[kvh * Q_PER_KV + g] * LOG2E, row)
        sink_rows.append(row)

    def block(q0, start, biases, with_scores):
        nk = len(biases)
        scores = []
        for kvh in range(N_KV_HEADS if with_scores else 0):
            kd = kd_ref[pl.ds(start, nk * BLK), kvh * LANES:(kvh + 1) * LANES]
            qs = []
            for g in range(Q_PER_KV):
                t = kvh * Q_PER_KV + g
                qt = q_ref[pl.ds(q0, BLK), (t // 2) * LANES:(t // 2 + 1) * LANES]
                qs.append(jnp.where(half_mask[t & 1], qt, jnp.zeros_like(qt)))
            scores.append(_dot_nt(kd, jnp.concatenate(qs, axis=0)))

        def weighted_values(scores):
            ones = jnp.ones((HEAD_DIM, nk * BLK), BF16)
            outs = []
            for kvh in range(N_KV_HEADS):
                v_ext = jnp.concatenate([vt_ref[kvh * HEAD_DIM:(kvh + 1) * HEAD_DIM, pl.ds(start, nk * BLK)], ones],
                                        axis=0)
                parts = [scores[kvh][j * BLK:(j + 1) * BLK] for j in range(nk)]
                parts = [p if b is None else p + b for p, b in zip(parts, biases)]
                mx = jnp.max(functools.reduce(jnp.maximum, parts), axis=0, keepdims=True)
                m = jnp.maximum(mx, sink_rows[kvh])
                p = jnp.concatenate([jnp.exp2(p - m) for p in parts], axis=0).astype(BF16)
                oe = _dot(v_ext, p)
                denom = oe[HEAD_DIM:HEAD_DIM + 1] + jnp.exp2(sink_rows[kvh] - m)
                on = oe[0:HEAD_DIM] / denom
                for pair in range(Q_PER_KV // 2):
                    two = jnp.concatenate([on[:, (2 * pair) * BLK:(2 * pair + 1) * BLK],
                                           on[:, (2 * pair + 1) * BLK:(2 * pair + 2) * BLK]], axis=0)
                    outs.append(two.T)
            result = jnp.concatenate(outs, axis=1).astype(BF16)

            def commit():
                o_ref[pl.ds(q0, BLK), :] = result
            return commit

        return (scores if with_scores else None), weighted_values

    def attend(n, with_scores=True):
        if isinstance(n, int) and n == 0:
            return block(0, 0, [None, next_bias], with_scores)
        if isinstance(n, int) and n == nb - 1:
            return block(n * BLK, (n - 1) * BLK, [prev_bias, None], with_scores)
        return block(_aligned(n * BLK, BLK), _aligned((n - 1) * BLK, BLK), [prev_bias, None, next_bias], with_scores)

    return attend


def _log_sigmoid(x):
    return jnp.minimum(x, 0.0) - jnp.log(1.0 + jnp.exp(-jnp.abs(x)))


def _mlstm_chunks(qk_ref, vt_ref, mo_ref, mgt_ref, gb_ref, hg_ref, o_ref,
                  gt_s, b_s, w_s, mloc_s, tot_s, rcol_s, cloc_s, cprev_s, mprev_s, keep_s, add_s):
    S = qk_ref.shape[0]
    L = BLK
    nc = S // L
    ng = N_GATES
    sub = lax.broadcasted_iota(jnp.int32, (L, L), 0)
    lane = lax.broadcasted_iota(jnp.int32, (L, L), 1)

    for c in range(nc):
        gt_s[c * ng:(c + 1) * ng, :] = mgt_ref[:, c * L:(c + 1) * L] + gb_ref[...]
    gt = gt_s[...]
    row = lax.broadcasted_iota(jnp.int32, gt.shape, 0)
    gt = jnp.where(((row >> 2) & 1) == 1, _log_sigmoid(gt), gt)
    gt = gt * LOG2E
    upper = jnp.where(sub <= lane, 1.0, 0.0).astype(BF16)
    g1 = gt.astype(BF16)
    r1 = gt - g1.astype(F32)
    g2 = r1.astype(BF16)
    g3 = (r1 - g2.astype(F32)).astype(BF16)
    pre = _dot(g1, upper) + _dot(g2, upper) + _dot(g3, upper)
    tot = jnp.broadcast_to(pre[:, L - 1:L], pre.shape)
    bsum = jnp.where((row & (ng - 1)) < ng // 2, pre, tot - pre + gt)
    li = pltpu.roll(gt, M_HEADS, 0)
    a = tot - bsum + li
    mloc = jnp.broadcast_to(jnp.max(a, axis=1, keepdims=True), a.shape)
    b_s[...] = bsum
    w_s[...] = jnp.exp2(a - mloc)
    mloc_s[...] = mloc
    tot_s[...] = tot
    rdiff = li - bsum
    zpad = jnp.zeros((L - ng, L), F32)
    for c in range(nc):
        rcol_s[c * L:(c + 1) * L, :] = jnp.concatenate([rdiff[c * ng:(c + 1) * ng, :], zpad], axis=0).T

    def value_rows(hd, r0, dtype):
        vt = vt_ref[hd * HEAD_DIM:(hd + 1) * HEAD_DIM, pl.ds(r0, L)].astype(dtype)
        return jnp.concatenate([vt, jnp.ones((V_ROWS - HEAD_DIM, L), dtype)], axis=0)

    def local_state(c, carry):
        r0 = pl.multiple_of(c * L, L)
        g0 = pl.multiple_of(c * ng, ng)
        k = qk_ref[pl.ds(r0, L), M_W:2 * M_W]
        w = w_s[pl.ds(g0, ng), :]
        for hd in range(M_HEADS):
            vte = value_rows(hd, r0, F32)
            lhs = jnp.concatenate([vte * w[4 + hd:5 + hd, :], vte * w[12 + hd:13 + hd, :]], axis=0)
            cl = _dot(lhs.astype(BF16), k[:, (hd // 2) * LANES:(hd // 2 + 1) * LANES])
            cloc_s[0, hd, c] = cl[0:V_ROWS]
            cloc_s[1, hd, c] = cl[V_ROWS:2 * V_ROWS]
        return carry

    lax.fori_loop(0, nc, local_state, 0, unroll=8)

    jrow = lax.broadcasted_iota(jnp.int32, (ng, L), 0)

    def scan_stabilisers(i, m):
        cf = i
        cb = nc - 1 - i
        gf = pl.multiple_of(cf * ng, ng)
        gb = pl.multiple_of(cb * ng, ng)
        is_fwd = jrow < ng // 2
        blast = jnp.where(is_fwd, tot_s[pl.ds(gf, ng), :], tot_s[pl.ds(gb, ng), :])
        mloc_i = jnp.where(is_fwd, mloc_s[pl.ds(gf, ng), :], mloc_s[pl.ds(gb, ng), :])
        mprev_s[0, pl.ds(gf, ng), :] = m
        mprev_s[1, pl.ds(gb, ng), :] = m
        m_new = jnp.maximum(blast + m, mloc_i)
        gi = pl.multiple_of(i * ng, ng)
        keep_s[pl.ds(gi, ng), :] = jnp.exp2(blast + m - m_new)
        add_s[pl.ds(gi, ng), :] = jnp.exp2(mloc_i - m_new)
        return m_new

    lax.fori_loop(0, nc, scan_stabilisers, jnp.zeros((ng, L), F32))

    for h0 in range(0, M_HEADS, SCAN_HEADS):
        def scan_states(i, states, h0=h0):
            gi = pl.multiple_of(i * ng, ng)
            keep = keep_s[pl.ds(gi, ng), :]
            add = add_s[pl.ds(gi, ng), :]
            new_states = []
            for j in range(SCAN_HEADS):
                for d, ci in enumerate((i, nc - 1 - i)):
                    hd = h0 + j
                    r = 4 + hd + 8 * d
                    st = states[2 * j + d]
                    cprev_s[d, hd, ci] = st.astype(BF16)
                    new_states.append(keep[r:r + 1, :] * st + add[r:r + 1, :] * cloc_s[d, hd, ci])
            return tuple(new_states)

        lax.fori_loop(0, nc, scan_states, tuple(jnp.zeros((V_ROWS, LANES), F32) for _ in range(2 * SCAN_HEADS)))

    causal = (sub <= lane, sub >= lane)

    def outputs(c, with_scores=True):
        r0 = _aligned(c * L, L)
        g0 = _aligned(c * ng, ng)
        q = qk_ref[pl.ds(r0, L), 0:M_W]
        k = qk_ref[pl.ds(r0, L), M_W:2 * M_W]
        bc = b_s[pl.ds(g0, ng), :]
        mprev = (mprev_s[0, pl.ds(g0, ng), :], mprev_s[1, pl.ds(g0, ng), :])
        rc = rcol_s[pl.ds(r0, L), :]
        ys = []
        for hd in range(M_HEADS if with_scores else 0):
            t = hd // 2
            qp = q[:, t * LANES:(t + 1) * LANES]
            qm = jnp.where((lane >> 6) == (hd & 1), qp, jnp.zeros_like(qp))
            x = jnp.concatenate([k[:, t * LANES:(t + 1) * LANES], cprev_s[0, hd, c], cprev_s[1, hd, c]], axis=0)
            ys.append(_dot_nt(x, qm))

        def combine(ys):
            houts = []
            for hd in range(M_HEADS):
                y = ys[hd]
                st = y[0:L]
                pts, scs, mts = [], [], []
                for d in range(2):
                    r = 4 + hd + 8 * d
                    b_row = bc[r:r + 1, :]
                    dm = jnp.where(causal[d], rc[:, r:r + 1] + b_row, NEG)
                    inter = b_row + mprev[d][r:r + 1, :]
                    m_t = jnp.maximum(inter, jnp.max(dm, axis=0, keepdims=True))
                    pts.append((jnp.exp2(dm - m_t) * st).astype(BF16))
                    scs.append(jnp.exp2(inter - m_t))
                    mts.append(m_t)
                n2 = _dot(value_rows(hd, r0, BF16), jnp.concatenate(pts, axis=1))
                ht = None
                for d in range(2):
                    tt = n2[:, d * L:(d + 1) * L] + scs[d] * y[L + d * V_ROWS:L + (d + 1) * V_ROWS]
                    den = tt[HEAD_DIM:HEAD_DIM + 1, :]
                    hd_out = tt[0:HEAD_DIM] / jnp.maximum(jnp.abs(den), jnp.exp2(-mts[d]))
                    ht = hd_out if ht is None else ht + hd_out
                ms = jnp.mean(ht * ht, axis=0, keepdims=True)
                houts.append(ht * lax.rsqrt(ms + EPS))
            hn = (jnp.concatenate(houts, axis=0) * hg_ref[...]).T
            result = (jax.nn.sigmoid(mo_ref[pl.ds(r0, L), :].astype(F32)) * hn).astype(BF16)

            def commit():
                o_ref[pl.ds(r0, L), :] = result
            return commit

        return (ys if with_scores else None), combine

    return outputs


def _mixers_kernel(sink_ref, q_ref, kd_ref, vat_ref, qk_ref, mvt_ref, mo_ref, mgt_ref, gb_ref, hg_ref,
                   ya_ref, ym_ref, *scratch):
    *mlstm_scratch, s_buf, y_buf = scratch
    nb = q_ref.shape[0] // BLK
    n_groups = (nb - 2) // 2
    attend = _attn_blocks(sink_ref, q_ref, kd_ref, vat_ref, ya_ref)
    outputs = _mlstm_chunks(qk_ref, mvt_ref, mo_ref, mgt_ref, gb_ref, hg_ref, ym_ref, *mlstm_scratch)
    group = lambda g: (1 + 2 * g, 2 + 2 * g)

    def first(blocks):
        return [f(n)[0] for n in blocks for f in (attend, outputs)]

    def second(blocks, results):
        finishes = [f(n, with_scores=False)[1] for n in blocks for f in (attend, outputs)]
        commits = [finish(r) for finish, r in zip(finishes, results)]
        for commit in commits:
            commit()

    def park(slot, results):
        for b in range(2):
            for i, s in enumerate(results[2 * b]):
                s_buf[slot, N_KV_HEADS * b + i, 0:s.shape[0], :] = s
            for i, y in enumerate(results[2 * b + 1]):
                y_buf[slot, M_HEADS * b + i] = y

    def fetch(slot, key_blocks):
        results = []
        for b in range(2):
            results.append([s_buf[slot, N_KV_HEADS * b + i, 0:key_blocks * BLK, :] for i in range(N_KV_HEADS)])
            results.append([y_buf[slot, M_HEADS * b + i] for i in range(M_HEADS)])
        return results

    edges = (0, nb - 1)

    def advance(g, slot):
        park(slot, first(group(g)))
        if isinstance(g, int) and g == 0:
            second(edges, fetch(1 - slot, 2))
        else:
            second(group(g - 1), fetch(1 - slot, 3))

    def two_steps(j, carry):
        advance(1 + 2 * j, 0)
        advance(2 + 2 * j, 1)
        return carry

    park(0, first(edges))
    advance(0, 1)
    lax.fori_loop(0, (n_groups - 1) // 2, two_steps, 0)
    last_slot = 1
    if (n_groups - 1) % 2:
        advance(n_groups - 1, 0)
        last_slot = 0
    second(group(n_groups - 1), fetch(last_slot, 3))


def _mixers(qa, kd, vat, sink, mqk, mvt, mo, mgt, gate_b, head_g, B, S):
    seq = lambda w: pl.BlockSpec((S, w), lambda b: (b, 0))
    tseq = lambda r: pl.BlockSpec((r, S), lambda b: (0, b))
    nc = S // BLK
    assert nc % 2 == 0 and nc >= 4
    gb = jnp.broadcast_to(gate_b.reshape(N_GATES, 1), (N_GATES, BLK))
    hg_t = jnp.broadcast_to(head_g.reshape(M_W, 1), (M_W, BLK))
    rows = nc * N_GATES
    return pl.pallas_call(
        _mixers_kernel,
        grid=(B,),
        in_specs=[pl.BlockSpec(memory_space=pltpu.SMEM), seq(ATTN_Q_W), seq(2 * ATTN_KV_W), tseq(ATTN_KV_W),
                  seq(2 * M_W), tseq(M_W), seq(M_W), tseq(N_GATES), _const_spec((N_GATES, BLK)),
                  _const_spec((M_W, BLK))],
        out_specs=[seq(ATTN_Q_W), seq(M_W)],
        out_shape=[jax.ShapeDtypeStruct((B * S, ATTN_Q_W), BF16), jax.ShapeDtypeStruct((B * S, M_W), BF16)],
        scratch_shapes=[pltpu.VMEM((rows, BLK), F32),
                        pltpu.VMEM((rows, BLK), F32),
                        pltpu.VMEM((rows, BLK), F32),
                        pltpu.VMEM((rows, BLK), F32),
                        pltpu.VMEM((rows, BLK), F32),
                        pltpu.VMEM((S, LANES), F32),
                        pltpu.VMEM((2, M_HEADS, nc, V_ROWS, LANES), F32),
                        pltpu.VMEM((2, M_HEADS, nc, V_ROWS, LANES), BF16),
                        pltpu.VMEM((2, rows, BLK), F32),
                        pltpu.VMEM((rows, BLK), F32),
                        pltpu.VMEM((rows, BLK), F32),
                        pltpu.VMEM((2, 2 * N_KV_HEADS, 3 * BLK, Q_PER_KV * BLK), F32),
                        pltpu.VMEM((2, 2 * M_HEADS, BLK + 2 * V_ROWS, BLK), F32)],
        compiler_params=_params("parallel"),
        name="mixers",
    )(sink, qa, kd, vat, mqk, mvt, mo, mgt, gb, hg_t)


def kernel(x, positions, norm_ffn1_g, ffn1_w_gate, ffn1_w_up, ffn1_w_down, norm_mix_g, w_in, q_norm_g, k_norm_g, attn_sink, mlstm_gate_b, mlstm_head_g, sgu_norm_g, sgu_w_s, sgu_b, w_out, norm_ffn2_g, ffn2_w_gate, ffn2_w_up, ffn2_w_down, norm_out_g):
    B, S, D = x.shape
    depth = w_in.shape[0]
    T = B * S
    xt = x.reshape(T, D)
    group = np.arange(M_W) // HEAD_DIM
    hmat = jnp.asarray((group[:, None] == group[None, :]) / HEAD_DIM, BF16)
    w_in_t = jnp.swapaxes(w_in, 1, 2)
    for l in range(depth):
        if l == 0:
            xt, cos, sin = _ffn(xt, l, norm_ffn1_g[l], ffn1_w_gate, ffn1_w_up, ffn1_w_down,
                                rope=_rope_inputs(positions))
        else:
            xt = _ffn(xt, l, norm_ffn1_g[l], ffn1_w_gate, ffn1_w_up, ffn1_w_down)
        qa, kd, vat, mqk, mvt, mo, mgt, ys = _mix_in(xt, l, norm_mix_g[l], w_in_t, q_norm_g[l],
                                                     k_norm_g[l], cos, sin, hmat, sgu_norm_g[l], sgu_w_s[l], sgu_b[l])
        ya, ym = _mixers(qa, kd, vat, attn_sink[l], mqk, mvt, mo, mgt, mlstm_gate_b[l], mlstm_head_g[l], B, S)
        xt = _ffn(xt, l, norm_ffn2_g[l], ffn2_w_gate, ffn2_w_up, ffn2_w_down, proj=(ya, ym, ys, w_out),
                  gout=norm_out_g[l])
    return xt.reshape(B, S, D)
```

```python
import functools

import numpy as np
import jax
import jax.numpy as jnp
from jax import lax
from jax.experimental import pallas as pl
from jax.experimental.pallas import tpu as pltpu

F32 = jnp.float32
BF16 = jnp.bfloat16

D_MODEL = 1024
HEAD_DIM = 64
N_Q_HEADS = 8
N_KV_HEADS = 2
Q_PER_KV = N_Q_HEADS // N_KV_HEADS
ATTN_Q_W = N_Q_HEADS * HEAD_DIM
ATTN_KV_W = N_KV_HEADS * HEAD_DIM
WINDOW = 128
BLK = 128
ROPE_THETA = 10000.0
M_HEADS = 4
M_W = M_HEADS * HEAD_DIM
N_GATES = 4 * M_HEADS
V_ROWS = HEAD_DIM + 16
SCAN_HEADS = 2
SGU_GROUPS = 4
SGU_W = SGU_GROUPS * HEAD_DIM
EPS = 1e-6
LANES = 128
NEG = -1e30
LOG2E = float(np.log2(np.e))

VMEM_LIMIT = 56 * 1024 * 1024
FFN_TM = 512
FFN_SPLIT = 2
W_CHUNKS = 8
W_SLOTS = 6
W_OUT_CHUNKS = 4
W_IN_CHUNKS = 5
MIX_TM = 1024
MIX_SPLIT = 2


def _dot(a, b):
    return jnp.dot(a, b, preferred_element_type=F32)


def _dot_nt(a, b):
    return lax.dot_general(a, b, (((1,), (1,)), ((), ())), preferred_element_type=F32)


def _aligned(x, m):
    return x if isinstance(x, int) else pl.multiple_of(x, m)


def _rms(x, g):
    ms = jnp.mean(x * x, axis=-1, keepdims=True)
    return x * lax.rsqrt(ms + EPS) * g


def _group_mean_sq(t, hmat):
    return _dot((t * t).astype(BF16), hmat)


def _const_spec(shape):
    nd = len(shape)
    return pl.BlockSpec(shape, lambda *_: (0,) * nd, pipeline_mode=pl.Buffered(1))


def _params(semantics):
    return pltpu.CompilerParams(dimension_semantics=(semantics,), vmem_limit_bytes=VMEM_LIMIT)


ROPE_HALF = HEAD_DIM // 2
ROPE_PACK = LANES // ROPE_HALF


def _rope_kernel(pos_ref, freq_ref, cos_ref, sin_ref):
    ang = pos_ref[...] * freq_ref[...]
    rows = ang.shape[0]
    token = lax.broadcasted_iota(jnp.int32, ang.shape, 1) >> 5
    for table, out_ref in ((jnp.cos(ang), cos_ref), (jnp.sin(ang), sin_ref)):
        for i in range(ROPE_PACK):
            x = jnp.where(token == i, table, 0.0)
            y = x + pltpu.roll(x, 2 * ROPE_HALF, 1)
            out_ref[pl.ds(i, rows, stride=ROPE_PACK), :] = y + pltpu.roll(y, ROPE_HALF, 1)


def _rope_inputs(positions):
    T = positions.size
    freqs = ROPE_THETA ** (-jnp.arange(0, HEAD_DIM, 2, dtype=F32) / HEAD_DIM)
    pos = jnp.repeat(positions.reshape(T // ROPE_PACK, ROPE_PACK).astype(F32), ROPE_HALF, axis=1)
    return pos, jnp.tile(freqs, ROPE_PACK).reshape(1, LANES)


def _load_weight_bf16(w_hbm, layer, dst, stage, sems):
    n_slots, rows, cols = stage.shape
    n_chunks = dst.shape[0] // rows

    def copy(c):
        slot = c % n_slots
        return pltpu.make_async_copy(w_hbm.at[layer, pl.ds(c * rows, rows), pl.ds(0, cols)], stage.at[slot],
                                     sems.at[slot])

    for c in range(min(n_slots - 1, n_chunks)):
        copy(c).start()
    for c in range(n_chunks):
        if c + n_slots - 1 < n_chunks:
            copy(c + n_slots - 1).start()
        copy(c).wait()
        dst[c * rows:(c + 1) * rows, :] = stage[c % n_slots].astype(BF16)


def _ffn_kernel(*refs, layer, has_rope, has_proj, has_final):
    refs = list(refs)
    x_ref = refs.pop(0)
    if has_rope:
        pos_ref, freq_ref = refs.pop(0), refs.pop(0)
    if has_proj:
        ya_ref, ym_ref, ys_ref, wo_hbm = refs[:4]
        refs = refs[4:]
    g_ref, wg_hbm, wu_hbm, wd_hbm = refs[:4]
    refs = refs[4:]
    if has_final:
        gout_ref = refs.pop(0)
    out_ref = refs.pop(0)
    if has_rope:
        cos_ref, sin_ref = refs.pop(0), refs.pop(0)
    wg_ref, wu_ref, wd_ref, stage_in, stage_out, sems = refs[:6]
    if has_proj:
        wo_ref = refs[6]

    @pl.when(pl.program_id(0) == 0)
    def _():
        if has_proj:
            rows = wo_ref.shape[0] // W_OUT_CHUNKS
            _load_weight_bf16(wo_hbm, layer, wo_ref, stage_out.at[:, 0:rows, :], sems)
        _load_weight_bf16(wg_hbm, layer, wg_ref, stage_in, sems)
        _load_weight_bf16(wu_hbm, layer, wu_ref, stage_in, sems)
        _load_weight_bf16(wd_hbm, layer, wd_ref, stage_out, sems)

    hs = x_ref.shape[0] // FFN_SPLIT
    rows = [slice(i * hs, (i + 1) * hs) for i in range(FFN_SPLIT)]
    xs = []
    for r in rows:
        x = x_ref[r, :]
        if has_proj:
            x = (x + _dot(ya_ref[r, :], wo_ref[0:ATTN_Q_W, :])
                 + _dot(ym_ref[r, :], wo_ref[ATTN_Q_W:ATTN_Q_W + M_W, :])
                 + _dot(ys_ref[r, :], wo_ref[ATTN_Q_W + M_W:, :]))
        xs.append(x)
    hidden = [_rms(x, g_ref[...]).astype(BF16) for x in xs]
    gate_up = [(_dot(h, wg_ref[...]), _dot(h, wu_ref[...])) for h in hidden]
    for r, x, (gate, up) in zip(rows, xs, gate_up):
        act = (gate * jax.nn.sigmoid(gate) * up).astype(BF16)
        y = x + 0.5 * _dot(act, wd_ref[...])
        if has_final:
            y = _rms(y, gout_ref[...])
        out_ref[r, :] = y

    if has_rope:
        _rope_kernel(pos_ref, freq_ref, cos_ref, sin_ref)


def _ffn(x, layer, g, wg, wu, wd, rope=None, proj=None, gout=None, tm=FFN_TM):
    T = x.shape[0]
    d_model, d_ff = wg.shape[1:]
    row = lambda w: pl.BlockSpec((tm, w), lambda i: (i, 0))
    hbm = pl.BlockSpec(memory_space=pl.ANY)
    args, specs = [x], [row(D_MODEL)]
    out_specs, out_shape = row(D_MODEL), jax.ShapeDtypeStruct((T, D_MODEL), F32)
    if rope is not None:
        args += list(rope)
        specs += [pl.BlockSpec((tm // ROPE_PACK, LANES), lambda i: (i, 0)), _const_spec((1, LANES))]
        out_specs = [out_specs, row(LANES), row(LANES)]
        out_shape = [out_shape] + [jax.ShapeDtypeStruct((T, LANES), F32)] * 2
    if proj is not None:
        ya, ym, ys, wo = proj
        args += [ya, ym, ys, wo]
        specs += [row(ATTN_Q_W), row(M_W), row(SGU_W), hbm]
    args += [g.reshape(1, D_MODEL), wg, wu, wd]
    specs += [_const_spec((1, D_MODEL)), hbm, hbm, hbm]
    if gout is not None:
        args.append(gout.reshape(1, D_MODEL))
        specs.append(_const_spec((1, D_MODEL)))
    return pl.pallas_call(
        functools.partial(_ffn_kernel, layer=layer, has_rope=rope is not None, has_proj=proj is not None,
                          has_final=gout is not None),
        grid=(T // tm,),
        in_specs=specs,
        out_specs=out_specs,
        out_shape=out_shape,
        scratch_shapes=[pltpu.VMEM((d_model, d_ff), BF16), pltpu.VMEM((d_model, d_ff), BF16),
                        pltpu.VMEM((d_ff, d_model), BF16),
                        pltpu.VMEM((W_SLOTS, d_model // W_CHUNKS, d_ff), F32),
                        pltpu.VMEM((W_SLOTS, d_ff // W_CHUNKS, d_model), F32),
                        pltpu.SemaphoreType.DMA((W_SLOTS,))]
        + ([pltpu.VMEM(proj[3].shape[1:], BF16)] if proj is not None else []),
        compiler_params=_params("arbitrary"),
        name="ffn",
    )(*args)


def _first_half(shape):
    lane = lax.broadcasted_iota(jnp.int32, shape, 1)
    return (lane & (HEAD_DIM - 1)) < HEAD_DIM // 2


def _rope(t, cos, sin_signed):
    n = t.shape[-1]
    reps = n // LANES
    swapped = jnp.where(_first_half(t.shape),
                        pltpu.roll(t, n - HEAD_DIM // 2, 1), pltpu.roll(t, HEAD_DIM // 2, 1))
    return t * jnp.tile(cos, (1, reps)) + swapped * jnp.tile(sin_signed, (1, reps))


def _gelu(x):
    return 0.5 * x * (1.0 + lax.erf(x * np.float32(np.sqrt(0.5))))


def _mixin_kernel(x_ref, g_ref, w_hbm, qg_ref, kg_ref, cos_ref, sin_ref, hmat_ref, ng_ref, ws_ref,
                  bs_ref, qa_ref, kd_ref, vat_ref, mqk_ref, mvt_ref, mo_ref, mgt_ref, ys_ref, w_ref, stage, sems,
                  *, layer):
    @pl.when(pl.program_id(0) == 0)
    def _():
        _load_weight_bf16(w_hbm, layer, w_ref, stage, sems)

    scale = HEAD_DIM ** -0.5
    hmat = hmat_ref[...]
    gw = hmat.shape[0]
    c_q, c_kv, c_mq, c_mk, c_mv, c_mo, c_g, c_su, c_sv = (int(c) for c in np.cumsum(
        [0, ATTN_Q_W, 2 * ATTN_KV_W, M_W, M_W, M_W, M_W, N_GATES, SGU_W]))
    lane_group = lax.broadcasted_iota(jnp.int32, (BLK, SGU_W), 1) >> 6

    def projections(r0, nrows):
        r = slice(r0, r0 + nrows)
        h = _rms(x_ref[r, :], g_ref[...]).astype(BF16)
        proj = lambda start, width: _dot_nt(h, w_ref[start:start + width, :])
        su = proj(c_su, SGU_W)
        sv = proj(c_sv, SGU_W)
        aq = proj(c_q, ATTN_Q_W)
        akv = proj(c_kv, 2 * ATTN_KV_W)
        ak, av = akv[:, :ATTN_KV_W], akv[:, ATTN_KV_W:]
        v = _gelu(sv)
        v_ms = _group_mean_sq(v, hmat)
        mqk_ref[r, 0:M_W] = proj(c_mq, M_W).astype(BF16)
        mqk_ref[r, M_W:2 * M_W] = (proj(c_mk, M_W) * scale).astype(BF16)
        q_ms = jnp.concatenate([_group_mean_sq(aq[:, i:i + gw], hmat) for i in range(0, ATTN_Q_W, gw)], axis=1)
        k_ms = _group_mean_sq(ak, hmat[0:ATTN_KV_W, 0:ATTN_KV_W])
        mvt_ref[:, r] = proj(c_mv, M_W).T.astype(BF16)
        mo_ref[r, :] = proj(c_mo, M_W).astype(BF16)
        mgt_ref[:, r] = proj(c_g, LANES).T[0:N_GATES, :]

        def tail():
            cos = cos_ref[r, :]
            sin_signed = jnp.where(_first_half(cos.shape), -sin_ref[r, :], sin_ref[r, :])
            qn = aq * lax.rsqrt(q_ms + EPS) * qg_ref[...]
            qa_ref[r, :] = (_rope(qn, cos, sin_signed) * (scale * LOG2E)).astype(BF16)
            kn = ak * lax.rsqrt(k_ms + EPS) * kg_ref[...]
            kr = _rope(kn, cos, sin_signed)
            ks = pltpu.roll(kr, HEAD_DIM, 1)
            low = lax.broadcasted_iota(jnp.int32, kr.shape, 1) < HEAD_DIM
            kd_ref[r, 0:LANES] = jnp.where(low, kr, ks).astype(BF16)
            kd_ref[r, LANES:2 * LANES] = jnp.where(low, ks, kr).astype(BF16)
            vat_ref[:, r] = av.T.astype(BF16)
            u = _gelu(su)
            vn = (v * lax.rsqrt(v_ms + EPS) * ng_ref[...]).astype(BF16)
            for c in range(nrows // BLK):
                vc = vn[c * BLK:(c + 1) * BLK]
                stack = jnp.concatenate(
                    [jnp.where(lane_group == g, vc, jnp.zeros_like(vc)) for g in range(SGU_GROUPS)], axis=0)
                mixed = _dot(ws_ref[...], stack)
                ys_ref[r0 + c * BLK:r0 + (c + 1) * BLK, :] = (
                    u[c * BLK:(c + 1) * BLK] * (mixed + bs_ref[...])).astype(BF16)

        return tail

    nrows = x_ref.shape[0] // MIX_SPLIT
    tails = [projections(i * nrows, nrows) for i in range(MIX_SPLIT)]
    for tail in tails:
        tail()


def _mix_in(x, layer, g, w_in_t, q_g, k_g, cos, sin, hmat, sgu_g, sgu_w, sgu_b, tm=MIX_TM):
    T = x.shape[0]
    d_in, d_model = w_in_t.shape[1:]
    chunk = d_in // W_IN_CHUNKS
    assert chunk * W_IN_CHUNKS == d_in and chunk % 16 == 0
    row = lambda w: pl.BlockSpec((tm, w), lambda i: (i, 0))
    tcol = lambda r: pl.BlockSpec((r, tm), lambda i: (0, i))
    out_specs = [row(ATTN_Q_W), row(2 * ATTN_KV_W), tcol(ATTN_KV_W), row(2 * M_W), tcol(M_W), row(M_W),
                 tcol(N_GATES), row(SGU_W)]
    out_shape = [jax.ShapeDtypeStruct(s, d) for s, d in [
        ((T, ATTN_Q_W), BF16), ((T, 2 * ATTN_KV_W), BF16), ((ATTN_KV_W, T), BF16), ((T, 2 * M_W), BF16),
        ((M_W, T), BF16), ((T, M_W), BF16), ((N_GATES, T), F32), ((T, SGU_W), BF16)]]
    ws_cat = jnp.transpose(sgu_w, (1, 0, 2)).reshape(BLK, SGU_GROUPS * BLK).astype(BF16)
    bs_full = jnp.repeat(jnp.transpose(sgu_b), HEAD_DIM, axis=1)
    return pl.pallas_call(
        functools.partial(_mixin_kernel, layer=layer),
        grid=(T // tm,),
        in_specs=[row(D_MODEL), _const_spec((1, D_MODEL)), pl.BlockSpec(memory_space=pl.ANY),
                  _const_spec((1, ATTN_Q_W)), _const_spec((1, ATTN_KV_W)),
                  row(LANES), row(LANES), _const_spec(hmat.shape),
                  _const_spec((1, SGU_W)), _const_spec(ws_cat.shape), _const_spec(bs_full.shape)],
        out_specs=out_specs,
        out_shape=out_shape,
        scratch_shapes=[pltpu.VMEM((d_in, d_model), BF16),
                        pltpu.VMEM((W_SLOTS, chunk, d_model), F32),
                        pltpu.SemaphoreType.DMA((W_SLOTS,))],
        compiler_params=_params("arbitrary"),
        name="mix_in",
    )(x, g.reshape(1, D_MODEL), w_in_t,
      jnp.tile(q_g, N_Q_HEADS).reshape(1, ATTN_Q_W), jnp.tile(k_g, N_KV_HEADS).reshape(1, ATTN_KV_W),
      cos, sin, hmat, sgu_g.reshape(1, SGU_W), ws_cat, bs_full)


def _attn_blocks(sink_ref, q_ref, kd_ref, vt_ref, o_ref):
    S = q_ref.shape[0]
    nb = S // BLK
    kc = lax.broadcasted_iota(jnp.int32, (BLK, BLK), 0)
    qi = lax.broadcasted_iota(jnp.int32, (BLK, BLK), 1)
    prev_bias = jnp.tile(jnp.where(kc >= qi, 0.0, NEG), (1, Q_PER_KV))
    next_bias = jnp.tile(jnp.where(kc <= qi, 0.0, NEG), (1, Q_PER_KV))
    half_mask = (qi < HEAD_DIM, qi >= HEAD_DIM)
    head_lane = lax.broadcasted_iota(jnp.int32, (1, Q_PER_KV * BLK), 1) >> 7
    sink_rows = []
    for kvh in range(N_KV_HEADS):
        row = jnp.zeros((1, Q_PER_KV * BLK), F32)
        for g in range(Q_PER_KV):
            row = jnp.where(head_lane == g, sink_ref[kvh * Q_PER_KV + g] * LOG2E, row)
        sink_rows.append(row)

    def block(q0, start, biases, with_scores):
        nk = len(biases)
        scores = []
        for kvh in range(N_KV_HEADS if with_scores else 0):
            kd = kd_ref[pl.ds(start, nk * BLK), kvh * LANES:(kvh + 1) * LANES]
            qs = []
            for g in range(Q_PER_KV):
                t = kvh * Q_PER_KV + g
                qt = q_ref[pl.ds(q0, BLK), (t // 2) * LANES:(t // 2 + 1) * LANES]
                qs.append(jnp.where(half_mask[t & 1], qt, jnp.zeros_like(qt)))
            scores.append(_dot_nt(kd, jnp.concatenate(qs, axis=0)))

        def weighted_values(scores):
            ones = jnp.ones((HEAD_DIM, nk * BLK), BF16)
            outs = []
            for kvh in range(N_KV_HEADS):
                v_ext = jnp.concatenate([vt_ref[kvh * HEAD_DIM:(kvh + 1) * HEAD_DIM, pl.ds(start, nk * BLK)], ones],
                                        axis=0)
                parts = [scores[kvh][j * BLK:(j + 1) * BLK] for j in range(nk)]
                parts = [p if b is None else p + b for p, b in zip(parts, biases)]
                mx = jnp.max(functools.reduce(jnp.maximum, parts), axis=0, keepdims=True)
                m = jnp.maximum(mx, sink_rows[kvh])
                p = jnp.concatenate([jnp.exp2(p - m) for p in parts], axis=0).astype(BF16)
                oe = _dot(v_ext, p)
                denom = oe[HEAD_DIM:HEAD_DIM + 1] + jnp.exp2(sink_rows[kvh] - m)
                on = oe[0:HEAD_DIM] / denom
                for pair in range(Q_PER_KV // 2):
                    two = jnp.concatenate([on[:, (2 * pair) * BLK:(2 * pair + 1) * BLK],
                                           on[:, (2 * pair + 1) * BLK:(2 * pair + 2) * BLK]], axis=0)
                    outs.append(two.T)
            result = jnp.concatenate(outs, axis=1).astype(BF16)

            def commit():
                o_ref[pl.ds(q0, BLK), :] = result
            return commit

        return (scores if with_scores else None), weighted_values

    def attend(n, with_scores=True):
        if isinstance(n, int) and n == 0:
            return block(0, 0, [None, next_bias], with_scores)
        if isinstance(n, int) and n == nb - 1:
            return block(n * BLK, (n - 1) * BLK, [prev_bias, None], with_scores)
        return block(_aligned(n * BLK, BLK), _aligned((n - 1) * BLK, BLK), [prev_bias, None, next_bias], with_scores)

    return attend


def _log_sigmoid(x):
    return jnp.minimum(x, 0.0) - jnp.log(1.0 + jnp.exp(-jnp.abs(x)))


def _mlstm_chunks(qk_ref, vt_ref, mo_ref, mgt_ref, gb_ref, hg_ref, o_ref,
                  gt_s, b_s, w_s, mloc_s, tot_s, rcol_s, cloc_s, cprev_s, mprev_s, keep_s, add_s):
    S = qk_ref.shape[0]
    L = BLK
    nc = S // L
    ng = N_GATES
    sub = lax.broadcasted_iota(jnp.int32, (L, L), 0)
    lane = lax.broadcasted_iota(jnp.int32, (L, L), 1)

    for c in range(nc):
        gt_s[c * ng:(c + 1) * ng, :] = mgt_ref[:, c * L:(c + 1) * L] + gb_ref[...]
    gt = gt_s[...]
    row = lax.broadcasted_iota(jnp.int32, gt.shape, 0)
    gt = jnp.where(((row >> 2) & 1) == 1, _log_sigmoid(gt), gt)
    gt = gt * LOG2E
    upper = jnp.where(sub <= lane, 1.0, 0.0).astype(BF16)
    g1 = gt.astype(BF16)
    r1 = gt - g1.astype(F32)
    g2 = r1.astype(BF16)
    g3 = (r1 - g2.astype(F32)).astype(BF16)
    pre = _dot(g1, upper) + _dot(g2, upper) + _dot(g3, upper)
    tot = jnp.broadcast_to(pre[:, L - 1:L], pre.shape)
    bsum = jnp.where((row & (ng - 1)) < ng // 2, pre, tot - pre + gt)
    li = pltpu.roll(gt, M_HEADS, 0)
    a = tot - bsum + li
    mloc = jnp.broadcast_to(jnp.max(a, axis=1, keepdims=True), a.shape)
    b_s[...] = bsum
    w_s[...] = jnp.exp2(a - mloc)
    mloc_s[...] = mloc
    tot_s[...] = tot
    rdiff = li - bsum
    zpad = jnp.zeros((L - ng, L), F32)
    for c in range(nc):
        rcol_s[c * L:(c + 1) * L, :] = jnp.concatenate([rdiff[c * ng:(c + 1) * ng, :], zpad], axis=0).T

    def value_rows(hd, r0, dtype):
        vt = vt_ref[hd * HEAD_DIM:(hd + 1) * HEAD_DIM, pl.ds(r0, L)].astype(dtype)
        return jnp.concatenate([vt, jnp.ones((V_ROWS - HEAD_DIM, L), dtype)], axis=0)

    def local_state(c, carry):
        r0 = pl.multiple_of(c * L, L)
        g0 = pl.multiple_of(c * ng, ng)
        k = qk_ref[pl.ds(r0, L), M_W:2 * M_W]
        w = w_s[pl.ds(g0, ng), :]
        for hd in range(M_HEADS):
            vte = value_rows(hd, r0, F32)
            lhs = jnp.concatenate([vte * w[4 + hd:5 + hd, :], vte * w[12 + hd:13 + hd, :]], axis=0)
            cl = _dot(lhs.astype(BF16), k[:, (hd // 2) * LANES:(hd // 2 + 1) * LANES])
            cloc_s[0, hd, c] = cl[0:V_ROWS]
            cloc_s[1, hd, c] = cl[V_ROWS:2 * V_ROWS]
        return carry

    lax.fori_loop(0, nc, local_state, 0, unroll=8)

    jrow = lax.broadcasted_iota(jnp.int32, (ng, L), 0)

    def scan_stabilisers(i, m):
        cf = i
        cb = nc - 1 - i
        gf = pl.multiple_of(cf * ng, ng)
        gb = pl.multiple_of(cb * ng, ng)
        is_fwd = jrow < ng // 2
        blast = jnp.where(is_fwd, tot_s[pl.ds(gf, ng), :], tot_s[pl.ds(gb, ng), :])
        mloc_i = jnp.where(is_fwd, mloc_s[pl.ds(gf, ng), :], mloc_s[pl.ds(gb, ng), :])
        mprev_s[0, pl.ds(gf, ng), :] = m
        mprev_s[1, pl.ds(gb, ng), :] = m
        m_new = jnp.maximum(blast + m, mloc_i)
        gi = pl.multiple_of(i * ng, ng)
        keep_s[pl.ds(gi, ng), :] = jnp.exp2(blast + m - m_new)
        add_s[pl.ds(gi, ng), :] = jnp.exp2(mloc_i - m_new)
        return m_new

    lax.fori_loop(0, nc, scan_stabilisers, jnp.zeros((ng, L), F32))

    for h0 in range(0, M_HEADS, SCAN_HEADS):
        def scan_states(i, states, h0=h0):
            gi = pl.multiple_of(i * ng, ng)
            keep = keep_s[pl.ds(gi, ng), :]
            add = add_s[pl.ds(gi, ng), :]
            new_states = []
            for j in range(SCAN_HEADS):
                for d, ci in enumerate((i, nc - 1 - i)):
                    hd = h0 + j
                    r = 4 + hd + 8 * d
                    st = states[2 * j + d]
                    cprev_s[d, hd, ci] = st.astype(BF16)
                    new_states.append(keep[r:r + 1, :] * st + add[r:r + 1, :] * cloc_s[d, hd, ci])
            return tuple(new_states)

        lax.fori_loop(0, nc, scan_states, tuple(jnp.zeros((V_ROWS, LANES), F32) for _ in range(2 * SCAN_HEADS)))

    causal = (sub <= lane, sub >= lane)

    def outputs(c, with_scores=True):
        r0 = _aligned(c * L, L)
        g0 = _aligned(c * ng, ng)
        q = qk_ref[pl.ds(r0, L), 0:M_W]
        k = qk_ref[pl.ds(r0, L), M_W:2 * M_W]
        bc = b_s[pl.ds(g0, ng), :]
        mprev = (mprev_s[0, pl.ds(g0, ng), :], mprev_s[1, pl.ds(g0, ng), :])
        rc = rcol_s[pl.ds(r0, L), :]
        ys = []
        for hd in range(M_HEADS if with_scores else 0):
            t = hd // 2
            qp = q[:, t * LANES:(t + 1) * LANES]
            qm = jnp.where((lane >> 6) == (hd & 1), qp, jnp.zeros_like(qp))
            x = jnp.concatenate([k[:, t * LANES:(t + 1) * LANES], cprev_s[0, hd, c], cprev_s[1, hd, c]], axis=0)
            ys.append(_dot_nt(x, qm))

        def combine(ys):
            houts = []
            for hd in range(M_HEADS):
                y = ys[hd]
                st = y[0:L]
                pts, scs, mts = [], [], []
                for d in range(2):
                    r = 4 + hd + 8 * d
                    b_row = bc[r:r + 1, :]
                    dm = jnp.where(causal[d], rc[:, r:r + 1] + b_row, NEG)
                    inter = b_row + mprev[d][r:r + 1, :]
                    m_t = jnp.maximum(inter, jnp.max(dm, axis=0, keepdims=True))
                    pts.append((jnp.exp2(dm - m_t) * st).astype(BF16))
                    scs.append(jnp.exp2(inter - m_t))
                    mts.append(m_t)
                n2 = _dot(value_rows(hd, r0, BF16), jnp.concatenate(pts, axis=1))
                ht = None
                for d in range(2):
                    tt = n2[:, d * L:(d + 1) * L] + scs[d] * y[L + d * V_ROWS:L + (d + 1) * V_ROWS]
                    den = tt[HEAD_DIM:HEAD_DIM + 1, :]
                    hd_out = tt[0:HEAD_DIM] / jnp.maximum(jnp.abs(den), jnp.exp2(-mts[d]))
                    ht = hd_out if ht is None else ht + hd_out
                ms = jnp.mean(ht * ht, axis=0, keepdims=True)
                houts.append(ht * lax.rsqrt(ms + EPS))
            hn = (jnp.concatenate(houts, axis=0) * hg_ref[...]).T
            result = (jax.nn.sigmoid(mo_ref[pl.ds(r0, L), :].astype(F32)) * hn).astype(BF16)

            def commit():
                o_ref[pl.ds(r0, L), :] = result
            return commit

        return (ys if with_scores else None), combine

    return outputs


def _mixers_kernel(sink_ref, q_ref, kd_ref, vat_ref, qk_ref, mvt_ref, mo_ref, mgt_ref, gb_ref, hg_ref,
                   ya_ref, ym_ref, *scratch):
    *mlstm_scratch, s_buf, y_buf = scratch
    nb = q_ref.shape[0] // BLK
    n_groups = (nb - 2) // 2
    attend = _attn_blocks(sink_ref, q_ref, kd_ref, vat_ref, ya_ref)
    outputs = _mlstm_chunks(qk_ref, mvt_ref, mo_ref, mgt_ref, gb_ref, hg_ref, ym_ref, *mlstm_scratch)
    group = lambda g: (1 + 2 * g, 2 + 2 * g)

    def first(blocks):
        return [f(n)[0] for n in blocks for f in (attend, outputs)]

    def second(blocks, results):
        finishes = [f(n, with_scores=False)[1] for n in blocks for f in (attend, outputs)]
        commits = [finish(r) for finish, r in zip(finishes, results)]
        for commit in commits:
            commit()

    def park(slot, results):
        for b in range(2):
            for i, s in enumerate(results[2 * b]):
                s_buf[slot, N_KV_HEADS * b + i, 0:s.shape[0], :] = s
            for i, y in enumerate(results[2 * b + 1]):
                y_buf[slot, M_HEADS * b + i] = y

    def fetch(slot, key_blocks):
        results = []
        for b in range(2):
            results.append([s_buf[slot, N_KV_HEADS * b + i, 0:key_blocks * BLK, :] for i in range(N_KV_HEADS)])
            results.append([y_buf[slot, M_HEADS * b + i] for i in range(M_HEADS)])
        return results

    edges = (0, nb - 1)

    def advance(g, slot):
        park(slot, first(group(g)))
        if isinstance(g, int) and g == 0:
            second(edges, fetch(1 - slot, 2))
        else:
            second(group(g - 1), fetch(1 - slot, 3))

    def two_steps(j, carry):
        advance(1 + 2 * j, 0)
        advance(2 + 2 * j, 1)
        return carry

    park(0, first(edges))
    advance(0, 1)
    lax.fori_loop(0, (n_groups - 1) // 2, two_steps, 0)
    last_slot = 1
    if (n_groups - 1) % 2:
        advance(n_groups - 1, 0)
        last_slot = 0
    second(group(n_groups - 1), fetch(last_slot, 3))


def _mixers(qa, kd, vat, sink, mqk, mvt, mo, mgt, gate_b, head_g, B, S):
    seq = lambda w: pl.BlockSpec((S, w), lambda b: (b, 0))
    tseq = lambda r: pl.BlockSpec((r, S), lambda b: (0, b))
    nc = S // BLK
    assert nc % 2 == 0 and nc >= 4
    gb = jnp.broadcast_to(gate_b.reshape(N_GATES, 1), (N_GATES, BLK))
    hg_t = jnp.broadcast_to(head_g.reshape(M_W, 1), (M_W, BLK))
    rows = nc * N_GATES
    return pl.pallas_call(
        _mixers_kernel,
        grid=(B,),
        in_specs=[pl.BlockSpec(memory_space=pltpu.SMEM), seq(ATTN_Q_W), seq(2 * ATTN_KV_W), tseq(ATTN_KV_W),
                  seq(2 * M_W), tseq(M_W), seq(M_W), tseq(N_GATES), _const_spec((N_GATES, BLK)),
                  _const_spec((M_W, BLK))],
        out_specs=[seq(ATTN_Q_W), seq(M_W)],
        out_shape=[jax.ShapeDtypeStruct((B * S, ATTN_Q_W), BF16), jax.ShapeDtypeStruct((B * S, M_W), BF16)],
        scratch_shapes=[pltpu.VMEM((rows, BLK), F32),
                        pltpu.VMEM((rows, BLK), F32),
                        pltpu.VMEM((rows, BLK), F32),
                        pltpu.VMEM((rows, BLK), F32),
                        pltpu.VMEM((rows, BLK), F32),
                        pltpu.VMEM((S, LANES), F32),
                        pltpu.VMEM((2, M_HEADS, nc, V_ROWS, LANES), F32),
                        pltpu.VMEM((2, M_HEADS, nc, V_ROWS, LANES), BF16),
                        pltpu.VMEM((2, rows, BLK), F32),
                        pltpu.VMEM((rows, BLK), F32),
                        pltpu.VMEM((rows, BLK), F32),
                        pltpu.VMEM((2, 2 * N_KV_HEADS, 3 * BLK, Q_PER_KV * BLK), F32),
                        pltpu.VMEM((2, 2 * M_HEADS, BLK + 2 * V_ROWS, BLK), F32)],
        compiler_params=_params("parallel"),
        name="mixers",
    )(sink, qa, kd, vat, mqk, mvt, mo, mgt, gb, hg_t)


def kernel(x, positions, norm_ffn1_g, ffn1_w_gate, ffn1_w_up, ffn1_w_down, norm_mix_g, w_in, q_norm_g, k_norm_g, attn_sink, mlstm_gate_b, mlstm_head_g, sgu_norm_g, sgu_w_s, sgu_b, w_out, norm_ffn2_g, ffn2_w_gate, ffn2_w_up, ffn2_w_down, norm_out_g):
    B, S, D = x.shape
    depth = w_in.shape[0]
    T = B * S
    xt = x.reshape(T, D)
    group = np.arange(M_W) // HEAD_DIM
    hmat = jnp.asarray((group[:, None] == group[None, :]) / HEAD_DIM, BF16)
    w_in_t = jnp.swapaxes(w_in, 1, 2)
    for l in range(depth):
        if l == 0:
            xt, cos, sin = _ffn(xt, l, norm_ffn1_g[l], ffn1_w_gate, ffn1_w_up, ffn1_w_down,
                                rope=_rope_inputs(positions))
        else:
            xt = _ffn(xt, l, norm_ffn1_g[l], ffn1_w_gate, ffn1_w_up, ffn1_w_down)
        qa, kd, vat, mqk, mvt, mo, mgt, ys = _mix_in(xt, l, norm_mix_g[l], w_in_t, q_norm_g[l],
                                                     k_norm_g[l], cos, sin, hmat, sgu_norm_g[l], sgu_w_s[l], sgu_b[l])
        ya, ym = _mixers(qa, kd, vat, attn_sink[l], mqk, mvt, mo, mgt, mlstm_gate_b[l], mlstm_head_g[l], B, S)
        xt = _ffn(xt, l, norm_ffn2_g[l], ffn2_w_gate, ffn2_w_up, ffn2_w_down, proj=(ya, ym, ys, w_out),
                  gout=norm_out_g[l])
    return xt.reshape(B, S, D)
```
